```python
import jax
import jax.numpy as jnp
from jax import lax
import numpy as np

D_MODEL = 2048
BATCH = 32
SEQ = 256
DEPTH = 2
DEC_BATCH = 4
DEC_SEQ = 1024
PAST_LEN = 256

GRID_W = 64
CHUNK = 128
A_HEADS = 4
A_WIDTH = D_MODEL // 4
B_GROUPS = 4
B_WIDTH = D_MODEL // 4
B_GROUP_DIM = B_WIDTH // B_GROUPS
C_WIDTH = D_MODEL // 2
C_HEAD_DIM = 64
C_HEADS = C_WIDTH // C_HEAD_DIM
N_DIR = 2
DECAY_RANK = 64
ICLR_RANK = 64
GATE_RANK = 160
C_IN = 3 * C_WIDTH + N_DIR * DECAY_RANK + N_DIR * ICLR_RANK + GATE_RANK
IN_COLS = 2 * A_WIDTH + B_WIDTH + C_IN
MIX_WIDTH = A_WIDTH + B_WIDTH + C_WIDTH
D_FF = 5632
N_MOD = 9
RMS_EPS = 1e-6
LN_EPS = 1e-5
LNX_EPS = 64e-5

kernel_name = "hybrid_gmlp_fnet_rwkv7_diffusion_step"


def _rmsnorm(x, g):
    xf = x.astype(jnp.float32)
    y = xf * lax.rsqrt(jnp.mean(xf * xf, axis=-1, keepdims=True) + RMS_EPS)
    return (y * g.astype(jnp.float32)).astype(x.dtype)


def _swiglu(h, w_in, w_out):
    gate, up = jnp.split(h @ w_in, 2, axis=-1)
    return (jax.nn.silu(gate) * up) @ w_out


def _shift_seq(z):
    h = z.shape[-1] // 2
    prev = jnp.pad(z[:, :-1, :h], ((0, 0), (1, 0), (0, 0)))
    nxt = jnp.pad(z[:, 1:, h:], ((0, 0), (0, 1), (0, 0)))
    return jnp.concatenate([prev, nxt], axis=-1)


def _shift_grid(z):
    bsz, length, ch = z.shape
    rows = length // GRID_W
    g = z.reshape(bsz, rows, GRID_W, ch)
    q = ch // 4
    left = jnp.pad(g[:, :, :-1, :q], ((0, 0), (0, 0), (1, 0), (0, 0)))
    right = jnp.pad(g[:, :, 1:, q:2 * q], ((0, 0), (0, 0), (0, 1), (0, 0)))
    up = jnp.pad(g[:, :-1, :, 2 * q:3 * q], ((0, 0), (1, 0), (0, 0), (0, 0)))
    down = jnp.pad(g[:, 1:, :, 3 * q:], ((0, 0), (0, 1), (0, 0), (0, 0)))
    return jnp.concatenate([left, right, up, down], axis=-1).reshape(bsz, length, ch)


def _spatial_gating(zu, zv, ln_g, ln_b, w_s, b_s):
    bsz, length, _ = zu.shape
    u = jax.nn.gelu(zu)
    v = jax.nn.gelu(zv).astype(jnp.float32)
    mu = jnp.mean(v, axis=-1, keepdims=True)
    var = jnp.mean(jnp.square(v - mu), axis=-1, keepdims=True)
    v = ((v - mu) * lax.rsqrt(var + LN_EPS) * ln_g + ln_b).astype(zu.dtype)
    v = v.reshape(bsz, length // CHUNK, CHUNK, A_HEADS, A_WIDTH // A_HEADS)
    mixed = jnp.einsum('hts,bnshd->bnthd', w_s, v) + b_s.T[None, None, :, :, None]
    return u * mixed.reshape(bsz, length, A_WIDTH).astype(zu.dtype)


def _fourier_mix(z):
    bsz, length, _ = z.shape
    zf = z.astype(jnp.float32).reshape(bsz, length, B_GROUPS, B_GROUP_DIM)
    zf = jnp.swapaxes(zf, 1, 2)
    out = jnp.fft.fft2(zf, norm='ortho').real
    return jnp.swapaxes(out, 1, 2).reshape(bsz, length, B_WIDTH).astype(z.dtype)


def _wkv_scan(s0, r, dec, k, v, kk, kka, reverse):
    def step(S, inp):
        r_t, d_t, k_t, v_t, kk_t, kka_t = inp
        sa = jnp.einsum('bhvk,bhk->bhv', S, kk_t)
        S = S * d_t[:, :, None, :] - sa[..., None] * kka_t[:, :, None, :] + v_t[..., None] * k_t[:, :, None, :]
        return S, jnp.einsum('bhvk,bhk->bhv', S, r_t)
    xs = tuple(jnp.swapaxes(t, 0, 1) for t in (r, dec, k, v, kk, kka))
    s_final, ys = lax.scan(step, s0, xs, reverse=reverse)
    return s_final, jnp.swapaxes(ys, 0, 1)


def _rwkv7_bidir(zc, grid, s0, mu, w0, w2, a0, a2, k_k, k_a, r_k, g2, lnx_g, lnx_b):
    bsz, length, _ = zc.shape
    f32 = jnp.float32
    zs = _shift_grid(zc) if grid else _shift_seq(zc)
    z = zc + (zs - zc) * mu
    idx = [C_WIDTH, 2 * C_WIDTH, 3 * C_WIDTH, 3 * C_WIDTH + N_DIR * DECAY_RANK,
           3 * C_WIDTH + N_DIR * (DECAY_RANK + ICLR_RANK)]
    r, k, v, wd, ad, gd = jnp.split(z, idx, axis=-1)
    wd = wd.reshape(bsz, length, N_DIR, DECAY_RANK)
    ad = ad.reshape(bsz, length, N_DIR, ICLR_RANK)
    w_log = -jax.nn.softplus(-(w0 + jnp.einsum('bldr,drc->bldc', jnp.tanh(wd), w2)).astype(f32)) - 0.5
    decay = jnp.exp(-jnp.exp(w_log))
    a = jax.nn.sigmoid((a0 + jnp.einsum('bldr,drc->bldc', ad, a2)).astype(f32))
    kf = k.astype(f32)
    kk = (kf * k_k).reshape(bsz, length, C_HEADS, C_HEAD_DIM)
    kk = kk / jnp.maximum(jnp.sqrt(jnp.sum(kk * kk, axis=-1, keepdims=True)), 1e-12)
    kk = kk.reshape(bsz, length, C_WIDTH)
    k_mod = kf[:, :, None, :] * (1.0 + (a - 1.0) * k_a)

    def heads(t):
        return t.reshape(bsz, length, C_HEADS, C_HEAD_DIM)

    rf = heads(r.astype(f32))
    vf = heads(v.astype(f32))
    s0f = s0.astype(f32)
    s_f, y_f = _wkv_scan(s0f[:, 0], rf, heads(decay[:, :, 0]), heads(k_mod[:, :, 0]), vf,
                         heads(kk), heads(kk * a[:, :, 0]), reverse=False)
    s_b, y_b = _wkv_scan(s0f[:, 1], rf, heads(decay[:, :, 1]), heads(k_mod[:, :, 1]), vf,
                         heads(kk), heads(kk * a[:, :, 1]), reverse=True)
    y = y_f + y_b
    m = jnp.mean(y, axis=-1, keepdims=True)
    var = jnp.mean(jnp.square(y - m), axis=-1, keepdims=True)
    y = ((y - m) * lax.rsqrt(var + LNX_EPS)).reshape(bsz, length, C_WIDTH) * lnx_g + lnx_b
    bonus = jnp.einsum('blhn,bldhn,hn->blh', rf,
                       k_mod.reshape(bsz, length, N_DIR, C_HEADS, C_HEAD_DIM), r_k.astype(f32))
    y = y + (bonus[..., None] * vf).reshape(bsz, length, C_WIDTH)
    g = jax.nn.sigmoid(gd) @ g2
    out = y.astype(zc.dtype) * g
    state = jnp.stack([s_f, s_b], axis=1).astype(s0.dtype)
    return out, state


def _token_mix(h, grid, s0, lp):
    z = h @ lp['w_in']
    zu, zv, zb, zc = jnp.split(z, [A_WIDTH, 2 * A_WIDTH, 2 * A_WIDTH + B_WIDTH], axis=-1)
    ya = _spatial_gating(zu, zv, lp['sgu_ln_g'], lp['sgu_ln_b'], lp['sgu_w'], lp['sgu_b'])
    yb = _fourier_mix(zb)
    yc, state = _rwkv7_bidir(zc, grid, s0, lp['shift_mu'], lp['decay_w0'], lp['decay_w2'],
                             lp['iclr_a0'], lp['iclr_a2'], lp['k_k'], lp['k_a'], lp['r_k'],
                             lp['gate_w2'], lp['lnx_g'], lp['lnx_b'])
    return jnp.concatenate([ya, yb, yc], axis=-1) @ lp['w_out'], state


def _layer(x, cond, s0, grid, lp):
    mod = (jax.nn.silu(cond) @ lp['w_mod'] + lp['b_mod']).reshape(cond.shape[0], 1, N_MOD, D_MODEL)
    norm_g = lp['norm_g']

    def modulated(x_, i):
        return _rmsnorm(x_, norm_g[i]) * (1.0 + mod[:, :, 3 * i + 1]) + mod[:, :, 3 * i]

    x = x + 0.5 * mod[:, :, 2] * _swiglu(modulated(x, 0), lp['ffn_w_in'][0], lp['ffn_w_out'][0])
    y, state = _token_mix(modulated(x, 1), grid, s0, lp)
    x = x + mod[:, :, 5] * y
    x = x + 0.5 * mod[:, :, 8] * _swiglu(modulated(x, 2), lp['ffn_w_in'][1], lp['ffn_w_out'][1])
    return x, state


def setup_inputs(seed: int = 0) -> dict:
    key = jax.random.key(seed)
    ks = jax.random.split(key, 28)
    f32 = jnp.float32
    d = D_MODEL

    def nrm(k, shape, scale):
        return jax.random.normal(k, shape, f32) * scale

    return {
        'x_prompt': nrm(ks[0], (BATCH, SEQ, d), 1.0),
        'x_sample': nrm(ks[1], (DEC_BATCH, DEC_SEQ, d), 1.0),
        'state_wkv': nrm(ks[2], (DEC_BATCH, DEPTH, N_DIR, C_HEADS, C_HEAD_DIM, C_HEAD_DIM), 0.5),
        'c': nrm(ks[3], (DEC_BATCH, d), 1.0),
        'c_ctx': nrm(ks[4], (d,), 1.0),
        'norm_g': 1.0 + nrm(ks[5], (DEPTH, 3, d), 0.02),
        'w_mod': nrm(ks[6], (DEPTH, d, N_MOD * d), 0.5 * d ** -0.5),
        'b_mod': nrm(ks[7], (DEPTH, N_MOD * d), 0.02),
        'ffn_w_in': nrm(ks[8], (DEPTH, 2, d, 2 * D_FF), d ** -0.5),
        'ffn_w_out': nrm(ks[9], (DEPTH, 2, D_FF, d), D_FF ** -0.5),
        'w_in': nrm(ks[10], (DEPTH, d, IN_COLS), d ** -0.5),
        'w_out': nrm(ks[11], (DEPTH, MIX_WIDTH, d), MIX_WIDTH ** -0.5),
        'sgu_ln_g': 1.0 + nrm(ks[12], (DEPTH, A_WIDTH), 0.02),
        'sgu_ln_b': nrm(ks[13], (DEPTH, A_WIDTH), 0.02),
        'sgu_w': nrm(ks[14], (DEPTH, A_HEADS, CHUNK, CHUNK), CHUNK ** -0.5),
        'sgu_b': 1.0 + nrm(ks[15], (DEPTH, A_HEADS, CHUNK), 0.1),
        'shift_mu': jax.random.uniform(ks[16], (DEPTH, C_IN), f32, 0.2, 0.8),
        'decay_w0': nrm(ks[17], (DEPTH, N_DIR, C_WIDTH), 0.5),
        'decay_w2': nrm(ks[18], (DEPTH, N_DIR, DECAY_RANK, C_WIDTH), 0.5 * DECAY_RANK ** -0.5),
        'iclr_a0': nrm(ks[19], (DEPTH, N_DIR, C_WIDTH), 0.1),
        'iclr_a2': nrm(ks[20], (DEPTH, N_DIR, ICLR_RANK, C_WIDTH), ICLR_RANK ** -0.5),
        'k_k': 0.85 + nrm(ks[21], (DEPTH, C_WIDTH), 0.05),
        'k_a': 1.0 + nrm(ks[22], (DEPTH, C_WIDTH), 0.05),
        'r_k': nrm(ks[23], (DEPTH, C_HEADS, C_HEAD_DIM), 0.1),
        'gate_w2': nrm(ks[24], (DEPTH, GATE_RANK, C_WIDTH), GATE_RANK ** -0.5),
        'lnx_g': 1.0 + nrm(ks[25], (DEPTH, C_WIDTH), 0.02),
        'lnx_b': nrm(ks[26], (DEPTH, C_WIDTH), 0.02),
        'final_g': 1.0 + nrm(ks[27], (d,), 0.02),
    }


def reference(x_prompt, x_sample, state_wkv, c, c_ctx, norm_g, w_mod, b_mod, ffn_w_in, ffn_w_out,
              w_in, w_out, sgu_ln_g, sgu_ln_b, sgu_w, sgu_b, shift_mu, decay_w0, decay_w2,
              iclr_a0, iclr_a2, k_k, k_a, r_k, gate_w2, lnx_g, lnx_b, final_g):
    xp = x_prompt
    xs = x_sample
    zero_state = jnp.zeros((x_prompt.shape[0], N_DIR, C_HEADS, C_HEAD_DIM, C_HEAD_DIM), x_prompt.dtype)
    ctx_cond = c_ctx[None, :]
    ctx_states = []
    for l in range(DEPTH):
        lp = {
            'norm_g': norm_g[l], 'w_mod': w_mod[l], 'b_mod': b_mod[l],
            'ffn_w_in': ffn_w_in[l], 'ffn_w_out': ffn_w_out[l],
            'w_in': w_in[l], 'w_out': w_out[l],
            'sgu_ln_g': sgu_ln_g[l], 'sgu_ln_b': sgu_ln_b[l], 'sgu_w': sgu_w[l], 'sgu_b': sgu_b[l],
            'shift_mu': shift_mu[l], 'decay_w0': decay_w0[l], 'decay_w2': decay_w2[l],
            'iclr_a0': iclr_a0[l], 'iclr_a2': iclr_a2[l], 'k_k': k_k[l], 'k_a': k_a[l],
            'r_k': r_k[l], 'gate_w2': gate_w2[l], 'lnx_g': lnx_g[l], 'lnx_b': lnx_b[l],
        }
        xp, s_ctx = _layer(xp, ctx_cond, zero_state, False, lp)
        ctx_states.append(s_ctx)
        xs, _ = _layer(xs, c, state_wkv[:, l], True, lp)
    y_prompt = _rmsnorm(xp, final_g)
    y_sample = _rmsnorm(xs, final_g)
    new_state_wkv = jnp.stack(ctx_states, axis=1)
    return (y_prompt, y_sample, new_state_wkv)
```

```python
import functools

import numpy as np
import jax
import jax.numpy as jnp
from jax import lax
from jax.experimental import pallas as pl
from jax.experimental.pallas import tpu as pltpu

F32 = jnp.float32
BF16 = jnp.bfloat16

D_MODEL = 2048
GRID_W = 64
SGU_CHUNK = 128
A_HEADS = 4
A_WIDTH = 512
B_WIDTH = 512
B_GROUP_DIM = 128
C_WIDTH = 1024
HEAD_DIM = 64
N_HEADS = 16
N_PAIRS = 8
DECAY_RANK = 64
ICLR_RANK = 64
GATE_RANK = 160
C_IN = 3488
C_PAD = 3584
Z_COLS = C_PAD + 2 * A_WIDTH + B_WIDTH
D_FF = 5632
N_MOD = 9
RMS_EPS = 1e-6
LN_EPS = 1e-5
LNX_EPS = 64e-5
SCAN_T = 64
MOD_ROWS = 8

VMEM_LIMIT = 56 * 1024 * 1024


def _cparams(sem):
    return pltpu.CompilerParams(dimension_semantics=sem, vmem_limit_bytes=VMEM_LIMIT)


def _dot(a, b):
    return jnp.dot(a, b, preferred_element_type=F32)


def _dot_hi(a, b, dims=(((1,), (0,)), ((), ()))):
    return lax.dot_general(a, b, dims, precision=lax.Precision.HIGHEST, preferred_element_type=F32)


def _split3(x):
    hi = x.astype(BF16)
    r1 = x - hi.astype(F32)
    mid = r1.astype(BF16)
    lo = (r1 - mid.astype(F32)).astype(BF16)
    return hi, mid, lo


def _dot_exact_rhs(x, e):
    hi, mid, lo = _split3(x)
    return _dot(hi, e) + _dot(mid, e) + _dot(lo, e)


def _dot3(a, b):
    ah = a.astype(BF16)
    al = (a - ah.astype(F32)).astype(BF16)
    bh = b.astype(BF16)
    bl = (b - bh.astype(F32)).astype(BF16)
    return _dot(ah, bh) + _dot(al, bh) + _dot(ah, bl)


def _sigmoid(x):
    return 1.0 / (1.0 + jnp.exp(-x))


def _silu(x):
    return x * _sigmoid(x)


def _gelu_tanh(x):
    return 0.5 * x * (1.0 + jnp.tanh(0.7978845608028654 * (x + 0.044715 * (x * x * x))))


def _mod_kernel(c_ref, w_ref, b_ref, o_ref):
    s = _silu(c_ref[...]).astype(BF16)
    o_ref[...] = _dot(s, w_ref[...].astype(BF16)) + b_ref[...]


def _modulation(cond, w_mod, b_mod):
    depth = w_mod.shape[0]
    n = w_mod.shape[2]
    tn = 1024
    return pl.pallas_call(
        _mod_kernel,
        grid=(depth, n // tn),
        in_specs=[
            pl.BlockSpec((MOD_ROWS, D_MODEL), lambda l, j: (0, 0)),
            pl.BlockSpec((None, D_MODEL, tn), lambda l, j: (l, 0, j)),
            pl.BlockSpec((None, 1, tn), lambda l, j: (l, 0, j)),
        ],
        out_specs=pl.BlockSpec((None, MOD_ROWS, tn), lambda l, j: (l, 0, j)),
        out_shape=jax.ShapeDtypeStruct((depth, MOD_ROWS, n), F32),
        compiler_params=_cparams(("arbitrary", "arbitrary")),
        name="modulation",
    )(cond, w_mod, b_mod.reshape(depth, 1, n))


def _mod_spec(layer, slot, row_of_tile, width=D_MODEL):
    return pl.BlockSpec((None, None, None, 1, width),
                        lambda i, *_: (layer, row_of_tile(i), slot, 0, 0))


def _norm_spec(layer, slot):
    return pl.BlockSpec((None, None, 1, D_MODEL), lambda i, *_: (layer, slot, 0, 0))


def _modulated_norm(x, g, scale, shift):
    ms = jnp.mean(x * x, axis=-1, keepdims=True)
    xn = x * lax.rsqrt(ms + RMS_EPS) * g
    return xn * (1.0 + scale) + shift


def _ffn_kernel(x_ref, sh_ref, sc_ref, gt_ref, g_ref, wg_ref, wu_ref, wo_ref, o_ref, h_ref, acc_ref):
    j = pl.program_id(1)

    @pl.when(j == 0)
    def _():
        h_ref[...] = _modulated_norm(x_ref[...], g_ref[...], sc_ref[...], sh_ref[...]).astype(BF16)
        acc_ref[...] = jnp.zeros_like(acc_ref)

    h = h_ref[...]
    gate = _dot(h, wg_ref[...])
    up = _dot(h, wu_ref[...])
    a = (_silu(gate) * up).astype(BF16)
    acc_ref[...] += _dot(a, wo_ref[...])

    @pl.when(j == pl.num_programs(1) - 1)
    def _():
        o_ref[...] = x_ref[...] + 0.5 * gt_ref[...] * acc_ref[...]


def _ffn(x, mod, norm_g, w_in, w_out, layer, sub, ffn_idx, row_of_tile, tm, tf=512):
    m = x.shape[0]
    nf = D_FF // tf
    return pl.pallas_call(
        _ffn_kernel,
        grid=(m // tm, nf),
        in_specs=[
            pl.BlockSpec((tm, D_MODEL), lambda i, j: (i, 0)),
            _mod_spec(layer, 3 * sub, row_of_tile),
            _mod_spec(layer, 3 * sub + 1, row_of_tile),
            _mod_spec(layer, 3 * sub + 2, row_of_tile),
            _norm_spec(layer, sub),
            pl.BlockSpec((None, None, D_MODEL, tf), lambda i, j: (layer, ffn_idx, 0, j)),
            pl.BlockSpec((None, None, D_MODEL, tf), lambda i, j: (layer, ffn_idx, 0, nf + j)),
            pl.BlockSpec((None, None, tf, D_MODEL), lambda i, j: (layer, ffn_idx, j, 0)),
        ],
        out_specs=pl.BlockSpec((tm, D_MODEL), lambda i, j: (i, 0)),
        out_shape=jax.ShapeDtypeStruct((m, D_MODEL), F32),
        scratch_shapes=[pltpu.VMEM((tm, D_MODEL), BF16), pltpu.VMEM((tm, D_MODEL), F32)],
        compiler_params=_cparams(("arbitrary", "arbitrary")),
        name="ffn",
    )(x, mod, mod, mod, norm_g, w_in, w_in, w_out)


def _in_kernel(x_ref, sh_ref, sc_ref, g_ref, w_ref, o_ref, h_ref):
    @pl.when(pl.program_id(1) == 0)
    def _():
        h_ref[...] = _modulated_norm(x_ref[...], g_ref[...], sc_ref[...], sh_ref[...]).astype(BF16)

    o_ref[...] = _dot(h_ref[...], w_ref[...])


def _in_proj(x, mod, norm_g, w_in, layer, row_of_tile, tm, tn=1024):
    m = x.shape[0]
    return pl.pallas_call(
        _in_kernel,
        grid=(m // tm, Z_COLS // tn),
        in_specs=[
            pl.BlockSpec((tm, D_MODEL), lambda i, j: (i, 0)),
            _mod_spec(layer, 3, row_of_tile),
            _mod_spec(layer, 4, row_of_tile),
            _norm_spec(layer, 1),
            pl.BlockSpec((None, D_MODEL, tn), lambda i, j: (layer, 0, j)),
        ],
        out_specs=pl.BlockSpec((tm, tn), lambda i, j: (i, j)),
        out_shape=jax.ShapeDtypeStruct((m, Z_COLS), F32),
        scratch_shapes=[pltpu.VMEM((tm, D_MODEL), BF16)],
        compiler_params=_cparams(("arbitrary", "arbitrary")),
        name="in_proj",
    )(x, mod, mod, norm_g, w_in)


def _sgu_kernel(zu_ref, zv_ref, lg_ref, lb_ref, w_ref, bias_ref, o_ref):
    u = _gelu_tanh(zu_ref[...])
    v = _gelu_tanh(zv_ref[...])
    mu = jnp.mean(v, axis=-1, keepdims=True)
    vc = v - mu
    var = jnp.mean(vc * vc, axis=-1, keepdims=True)
    vn = (vc * lax.rsqrt(var + LN_EPS) * lg_ref[...] + lb_ref[...]).astype(BF16)
    hd = A_WIDTH // A_HEADS
    for c in range(u.shape[0] // SGU_CHUNK):
        rows = slice(c * SGU_CHUNK, (c + 1) * SGU_CHUNK)
        for h in range(A_HEADS):
            cols = slice(h * hd, (h + 1) * hd)
            mixed = _dot(w_ref[h], vn[rows, cols]) + bias_ref[:, cols]
            o_ref[rows, cols] = (u[rows, cols] * mixed).astype(BF16)


def _sgu(z, ln_g, ln_b, w_s, bias, layer, tm):
    m = z.shape[0]
    cu = C_PAD // A_WIDTH
    return pl.pallas_call(
        _sgu_kernel,
        grid=(m // tm,),
        in_specs=[
            pl.BlockSpec((tm, A_WIDTH), lambda i: (i, cu)),
            pl.BlockSpec((tm, A_WIDTH), lambda i: (i, cu + 1)),
            pl.BlockSpec((None, 1, A_WIDTH), lambda i: (layer, 0, 0)),
            pl.BlockSpec((None, 1, A_WIDTH), lambda i: (layer, 0, 0)),
            pl.BlockSpec((None, A_HEADS, SGU_CHUNK, SGU_CHUNK), lambda i: (layer, 0, 0, 0)),
            pl.BlockSpec((None, SGU_CHUNK, A_WIDTH), lambda i: (layer, 0, 0)),
        ],
        out_specs=pl.BlockSpec((tm, A_WIDTH), lambda i: (i, 0)),
        out_shape=jax.ShapeDtypeStruct((m, A_WIDTH), BF16),
        compiler_params=_cparams(("arbitrary",)),
        name="sgu",
    )(z, z, ln_g, ln_b, w_s, bias)


def _dft_mats(n):
    idx = np.arange(n)
    ang = 2.0 * np.pi * ((idx[:, None] * idx[None, :]) % n) / n
    return np.cos(ang) / np.sqrt(n), np.sin(ang) / np.sqrt(n)


def _hi_lo(a):
    a32 = jnp.asarray(a, F32)
    hi = a32.astype(BF16)
    lo = (a32 - hi.astype(F32)).astype(BF16)
    return hi, lo


def _fnet_kernel(z_ref, fdh_ref, fdl_ref, fsh_ref, fsl_ref, flh_ref, fll_ref, o_ref, t_ref, *,
                 n_ctx_tiles, seq):
    tm = z_ref.shape[0]
    x = z_ref[...]
    xh = x.astype(BF16)
    xl = (x - xh.astype(F32)).astype(BF16)
    gd = B_GROUP_DIM
    for g in range(B_WIDTH // gd):
        cols = slice(g * gd, (g + 1) * gd)
        t = _dot(xh[:, cols], fdh_ref[...]) + _dot(xl[:, cols], fdh_ref[...]) + _dot(xh[:, cols], fdl_ref[...])
        t_ref[0:tm, cols] = t[:, 0:gd]
        t_ref[tm:2 * tm, cols] = t[:, gd:2 * gd]

    def position_dft(fh_ref, fl_ref, rows_in, rows_out):
        tc = t_ref[rows_in[0], :]
        ts = t_ref[rows_in[1], :]
        tt = jnp.concatenate([tc, ts], axis=0)
        th = tt.astype(BF16)
        tl = (tt - th.astype(F32)).astype(BF16)
        out = _dot(fh_ref[...], th) + _dot(fl_ref[...], th) + _dot(fh_ref[...], tl)
        o_ref[rows_out, :] = out.astype(BF16)

    is_ctx = pl.program_id(0) < n_ctx_tiles

    @pl.when(is_ctx)
    def _():
        for s in range(tm // seq):
            r0 = slice(s * seq, (s + 1) * seq)
            r1 = slice(tm + s * seq, tm + (s + 1) * seq)
            position_dft(fsh_ref, fsl_ref, (r0, r1), r0)

    @pl.when(jnp.logical_not(is_ctx))
    def _():
        position_dft(flh_ref, fll_ref, (slice(0, tm), slice(tm, 2 * tm)), slice(0, tm))


def _fnet(z, consts, n_ctx_tiles, seq, tm):
    m = z.shape[0]
    cb = (C_PAD + 2 * A_WIDTH) // B_WIDTH
    full = lambda a: pl.BlockSpec(a.shape, lambda i: (0,) * a.ndim)
    return pl.pallas_call(
        functools.partial(_fnet_kernel, n_ctx_tiles=n_ctx_tiles, seq=seq),
        grid=(m // tm,),
        in_specs=[pl.BlockSpec((tm, B_WIDTH), lambda i: (i, cb))] + [full(a) for a in consts],
        out_specs=pl.BlockSpec((tm, B_WIDTH), lambda i: (i, 0)),
        out_shape=jax.ShapeDtypeStruct((m, B_WIDTH), BF16),
        scratch_shapes=[pltpu.VMEM((2 * tm, B_WIDTH), F32)],
        compiler_params=_cparams(("arbitrary",)),
        name="fnet",
    )(z, *consts)


def _seg_sum(x, e_ref):
    w = e_ref.shape[0]
    parts = [_dot_exact_rhs(x[:, b * w:(b + 1) * w], e_ref[...]) for b in range(x.shape[1] // w)]
    return parts[0] if len(parts) == 1 else jnp.concatenate(parts, axis=1)


def _prep_kernel(zc_ref, zp_ref, zn_ref, mu_ref, w0_ref, w2_ref, a0_ref, a2_ref, kk_ref, ka_ref, rk_ref,
                 g2_ref, e_ref,
                 r_o, v_o, kk_o, ldf_o, ldb_o, kmf_o, kmb_o, kaf_o, kab_o, bv_o, g_o, *,
                 n_ctx_tiles, tiles_per_lat):
    i = pl.program_id(0)
    tm = zc_ref.shape[0]
    is_ctx = i < n_ctx_tiles
    q = (i - n_ctx_tiles) % tiles_per_lat
    x = zc_ref[...]
    row = lax.broadcasted_iota(jnp.int32, (tm, 1), 0)
    lane = lax.broadcasted_iota(jnp.int32, (1, C_PAD), 1)
    period = jnp.where(is_ctx, tm, GRID_W)
    pos = row & (period - 1)
    prev1 = jnp.where(pos == 0, 0.0, pltpu.roll(x, 1, 0))
    next1 = jnp.where(pos == period - 1, 0.0, pltpu.roll(x, tm - 1, 0))
    up_halo = jnp.where(q > 0, zp_ref[...], 0.0)
    dn_halo = jnp.where(q < tiles_per_lat - 1, zn_ref[...], 0.0)
    up = jnp.concatenate([up_halo, x[:tm - GRID_W]], axis=0)
    down = jnp.concatenate([x[GRID_W:], dn_halo], axis=0)
    half = C_IN // 2
    quarter = C_IN // 4
    zs_ctx = jnp.where(lane < half, prev1, next1)
    zs_lat = jnp.where(lane < quarter, prev1,
                       jnp.where(lane < 2 * quarter, next1, jnp.where(lane < 3 * quarter, up, down)))
    zs = jnp.where(is_ctx, zs_ctx, zs_lat)
    z = x + (zs - x) * mu_ref[...]

    cw = C_WIDTH
    r = z[:, 0:cw]
    k = z[:, cw:2 * cw]
    v = z[:, 2 * cw:3 * cw]
    wd = z[:, 3 * cw:3 * cw + 128]
    ad = z[:, 3 * cw + 128:3 * cw + 256]
    gd = z[:, 3 * cw + 256:C_PAD]

    uw = _dot3(jnp.tanh(wd), w2_ref[...])
    ua = _dot3(ad, a2_ref[...])
    kk_raw = k * kk_ref[...]
    ss = _seg_sum(kk_raw * kk_raw, e_ref)
    kk = kk_raw / jnp.maximum(jnp.sqrt(ss), 1e-12)
    r_o[...] = r
    v_o[...] = v
    kk_o[...] = kk
    km_sum = None
    for d, (ld_o, km_o, ka_o) in enumerate(((ldf_o, kmf_o, kaf_o), (ldb_o, kmb_o, kab_o))):
        u = uw[:, d * cw:(d + 1) * cw] + w0_ref[d:d + 1, :]
        ld_o[...] = -float(np.exp(-0.5)) * _sigmoid(u)
        a = _sigmoid(ua[:, d * cw:(d + 1) * cw] + a0_ref[d:d + 1, :])
        km = k * (1.0 + (a - 1.0) * ka_ref[...])
        km_o[...] = km
        ka_o[...] = kk * a
        km_sum = km if km_sum is None else km_sum + km
    bonus = _seg_sum(r * km_sum * rk_ref[...], e_ref)
    bv_o[...] = bonus * v
    g_o[...] = _dot(_sigmoid(gd).astype(BF16), g2_ref[...])


def _rwkv_prep(z, p, layer, n_ctx_tiles, tiles_per_lat, tm):
    m = z.shape[0]
    hb = tm // GRID_W
    n_halo = m // GRID_W
    lay = lambda *shape: pl.BlockSpec((None,) + shape, lambda i: (layer,) + (0,) * len(shape))
    out = jax.ShapeDtypeStruct((m, C_WIDTH), F32)
    return pl.pallas_call(
        functools.partial(_prep_kernel, n_ctx_tiles=n_ctx_tiles, tiles_per_lat=tiles_per_lat),
        grid=(m // tm,),
        in_specs=[
            pl.BlockSpec((tm, C_PAD), lambda i: (i, 0)),
            pl.BlockSpec((GRID_W, C_PAD), lambda i: (jnp.maximum(i * hb - 1, 0), 0)),
            pl.BlockSpec((GRID_W, C_PAD), lambda i: (jnp.minimum(i * hb + hb, n_halo - 1), 0)),
            lay(1, C_PAD),
            lay(2, C_WIDTH),
            lay(128, 2 * C_WIDTH),
            lay(2, C_WIDTH),
            lay(128, 2 * C_WIDTH),
            lay(1, C_WIDTH),
            lay(1, C_WIDTH),
            lay(1, C_WIDTH),
            lay(256, C_WIDTH),
            pl.BlockSpec((256, 256), lambda i: (0, 0)),
        ],
        out_specs=[pl.BlockSpec((tm, C_WIDTH), lambda i: (i, 0))] * 11,
        out_shape=[out] * 11,
        compiler_params=_cparams(("arbitrary",)),
        name="rwkv_prep",
    )(z, z, z, p["mu"], p["w0"], p["w2"], p["a0"], p["a2"], p["k_k"], p["k_a"], p["r_k"], p["g2"],
      p["e256"])


def _block_diag(y, bd_mask):
    return jnp.where(bd_mask, jnp.concatenate([y, y], axis=0), 0.0)


def _scan_unit(reverse, r, v, kk, ld, km, ka, h_bd):
    t = SCAN_T
    w = 2 * HEAD_DIM
    ti = lax.broadcasted_iota(jnp.int32, (t, w), 0)
    si = lax.broadcasted_iota(jnp.int32, (t, w), 1) & (HEAD_DIM - 1)
    if reverse:
        strict, incl = si > ti, si >= ti
    else:
        strict, incl = si < ti, si <= ti
    eye = si == ti
    t_row = lax.broadcasted_iota(jnp.int32, (t, t), 0)
    t_col = lax.broadcasted_iota(jnp.int32, (t, t), 1)
    tri = jnp.where((t_col >= t_row) if reverse else (t_col <= t_row), 1.0, 0.0).astype(BF16)
    br = lax.broadcasted_iota(jnp.int32, (w, w), 0) // HEAD_DIM
    bc = lax.broadcasted_iota(jnp.int32, (w, w), 1) // HEAD_DIM
    bd_mask = br == bc

    cum = _dot_exact_rhs_left(tri, ld)
    c_end = cum[0:1, :] if reverse else cum[t - 1:t, :]
    p_in = jnp.exp(cum)
    p_ex = jnp.exp(cum - ld)
    p_inv = jnp.exp(-cum)
    p_end = jnp.exp(c_end - cum)
    p_tot = jnp.exp(c_end)
    rd = r * p_in
    kkd = kk * p_ex
    ai = ka * p_inv
    ki = km * p_inv
    q = jnp.concatenate([kkd, rd], axis=0)
    rr = jnp.concatenate([_block_diag(ai, bd_mask), _block_diag(ki, bd_mask)], axis=0)
    sc = _dot_hi(q, rr, (((1,), (1,)), ((), ())))
    la = jnp.where(strict, sc[0:t, 0:w], 0.0)
    lk = jnp.where(strict, sc[0:t, w:2 * w], 0.0)
    ma = jnp.where(incl, sc[t:2 * t, 0:w], 0.0)
    mk = jnp.where(incl, sc[t:2 * t, w:2 * w], 0.0)

    n_pow = -la
    x_inv = jnp.where(eye, 1.0, 0.0) + n_pow
    steps = int(np.log2(t)) - 1
    for j in range(steps):
        n_pow = _dot_hi(n_pow, _block_diag(n_pow, bd_mask))
        x_inv = x_inv + _dot_hi(x_inv, _block_diag(n_pow, bd_mask))

    hb = _dot_hi(q, h_bd)
    lv = _dot_hi(jnp.concatenate([lk, mk], axis=0), _block_diag(v, bd_mask))
    u = -_dot_hi(x_inv, _block_diag(hb[0:t] + lv[0:t], bd_mask))
    y = hb[t:2 * t] + lv[t:2 * t] + _dot_hi(ma, _block_diag(u, bd_mask))
    lhs = jnp.concatenate([ka * p_end, km * p_end], axis=0)
    rhs = jnp.concatenate([u, v], axis=0)
    upd = _dot_hi(lhs.T, rhs)
    p_col = jnp.broadcast_to(p_tot, (w, w)).T
    h_new = p_col * h_bd + jnp.where(bd_mask, upd, 0.0)
    return y, h_new


def _dot_exact_rhs_left(e, x):
    hi, mid, lo = _split3(x)
    return _dot(e, hi) + _dot(e, mid) + _dot(e, lo)


def _scan_kernel(*refs, has_s0, emit_state):
    ins = refs[:12]
    pos = 12
    s0_ref = None
    if has_s0:
        s0_ref = refs[pos]
        pos += 1
    yf_ref, yb_ref = refs[pos], refs[pos + 1]
    pos += 2
    st_ref = None
    if emit_state:
        st_ref = refs[pos]
        pos += 1
    h_ref = refs[pos]
    c = pl.program_id(2)

    @pl.when(c == 0)
    def _():
        if has_s0:
            h_ref[...] = s0_ref[...]
        else:
            h_ref[...] = jnp.zeros_like(h_ref)

    for d, y_ref in enumerate((yf_ref, yb_ref)):
        r, v, kk, ld, km, ka = (ref[...] for ref in ins[6 * d:6 * d + 6])
        y, h_new = _scan_unit(d == 1, r, v, kk, ld, km, ka, h_ref[d])
        y_ref[...] = y
        h_ref[d] = h_new

    if emit_state:
        @pl.when(c == pl.num_programs(2) - 1)
        def _():
            st_ref[...] = h_ref[...]


def _rwkv_scan(arrs, s0_bd, n_seq, n_chunk, chunk0, emit_state):
    r, v, kk, ldf, kmf, kaf, ldb, kmb, kab = arrs
    m = n_seq * n_chunk * SCAN_T
    fwd = lambda b, p, c: (b * n_chunk + c, p)
    bwd = lambda b, p, c: (b * n_chunk + (n_chunk - 1 - c), p)
    fwd_in = lambda b, p, c: (chunk0 + b * n_chunk + c, p)
    bwd_in = lambda b, p, c: (chunk0 + b * n_chunk + (n_chunk - 1 - c), p)
    blk = lambda imap: pl.BlockSpec((SCAN_T, 2 * HEAD_DIM), imap)
    in_specs = [blk(fwd_in)] * 6 + [blk(bwd_in)] * 6
    args = [r, v, kk, ldf, kmf, kaf, r, v, kk, ldb, kmb, kab]
    st_spec = pl.BlockSpec((None, 2, None, 128, 128), lambda b, p, c: (b, 0, p, 0, 0))
    has_s0 = s0_bd is not None
    if has_s0:
        in_specs.append(st_spec)
        args.append(s0_bd)
    y_shape = jax.ShapeDtypeStruct((m, C_WIDTH), F32)
    out_specs = [blk(fwd), blk(bwd)]
    out_shape = [y_shape, y_shape]
    if emit_state:
        out_specs.append(st_spec)
        out_shape.append(jax.ShapeDtypeStruct((n_seq, 2, N_PAIRS, 128, 128), F32))
    return pl.pallas_call(
        functools.partial(_scan_kernel, has_s0=has_s0, emit_state=emit_state),
        grid=(n_seq, N_PAIRS, n_chunk),
        in_specs=in_specs,
        out_specs=out_specs,
        out_shape=out_shape,
        scratch_shapes=[pltpu.VMEM((2, 128, 128), F32)],
        compiler_params=_cparams(("arbitrary", "arbitrary", "arbitrary")),
        name="rwkv_scan",
    )(*args)


def _post_kernel(yf1_ref, yb1_ref, yf2_ref, yb2_ref, bv_ref, g_ref, lg_ref, lb_ref, e_ref, o_ref, *,
                 n_ctx_tiles):
    is_ctx = pl.program_id(0) < n_ctx_tiles
    y = jnp.where(is_ctx, yf1_ref[...] + yb1_ref[...], yf2_ref[...] + yb2_ref[...])
    inv = 1.0 / HEAD_DIM
    m = _seg_sum(y, e_ref) * inv
    yc = y - m
    var = _seg_sum(yc * yc, e_ref) * inv
    yn = yc * lax.rsqrt(var + LNX_EPS) * lg_ref[...] + lb_ref[...]
    o_ref[...] = ((yn + bv_ref[...]) * g_ref[...]).astype(BF16)


def _rwkv_post(y_ctx, y_lat, bv, g, lnx_g, lnx_b, e256, layer, n_ctx_tiles, tm):
    m = bv.shape[0]
    n_tiles = m // tm
    row = pl.BlockSpec((tm, C_WIDTH), lambda i: (i, 0))
    row_ctx = pl.BlockSpec((tm, C_WIDTH), lambda i: (jnp.minimum(i, n_ctx_tiles - 1), 0))
    row_lat = pl.BlockSpec((tm, C_WIDTH), lambda i: (jnp.maximum(i - n_ctx_tiles, 0), 0))
    vec = pl.BlockSpec((None, 1, C_WIDTH), lambda i: (layer, 0, 0))
    return pl.pallas_call(
        functools.partial(_post_kernel, n_ctx_tiles=n_ctx_tiles),
        grid=(n_tiles,),
        in_specs=[row_ctx, row_ctx, row_lat, row_lat, row, row, vec, vec,
                  pl.BlockSpec((256, 256), lambda i: (0, 0))],
        out_specs=row,
        out_shape=jax.ShapeDtypeStruct((m, C_WIDTH), BF16),
        compiler_params=_cparams(("arbitrary",)),
        name="rwkv_post",
    )(y_ctx[0], y_ctx[1], y_lat[0], y_lat[1], bv, g, lnx_g, lnx_b, e256)


def _out_kernel(x_ref, gt_ref, ya_ref, yb_ref, yc_ref, wa_ref, wb_ref, wc_ref, o_ref):
    y = _dot(ya_ref[...], wa_ref[...]) + _dot(yb_ref[...], wb_ref[...]) + _dot(yc_ref[...], wc_ref[...])
    o_ref[...] = x_ref[...] + gt_ref[...] * y


def _out_proj(x, mod, ya, yb, yc, w_out, layer, row_of_tile, tm):
    m = x.shape[0]
    return pl.pallas_call(
        _out_kernel,
        grid=(m // tm,),
        in_specs=[
            pl.BlockSpec((tm, D_MODEL), lambda i: (i, 0)),
            _mod_spec(layer, 5, row_of_tile),
            pl.BlockSpec((tm, A_WIDTH), lambda i: (i, 0)),
            pl.BlockSpec((tm, B_WIDTH), lambda i: (i, 0)),
            pl.BlockSpec((tm, C_WIDTH), lambda i: (i, 0)),
            pl.BlockSpec((None, A_WIDTH, D_MODEL), lambda i: (layer, 0, 0)),
            pl.BlockSpec((None, B_WIDTH, D_MODEL), lambda i: (layer, 1, 0)),
            pl.BlockSpec((None, C_WIDTH, D_MODEL), lambda i: (layer, 1, 0)),
        ],
        out_specs=pl.BlockSpec((tm, D_MODEL), lambda i: (i, 0)),
        out_shape=jax.ShapeDtypeStruct((m, D_MODEL), F32),
        compiler_params=_cparams(("arbitrary",)),
        name="out_proj",
    )(x, mod, ya, yb, yc, w_out, w_out, w_out)


def _final_norm_kernel(x_ref, g_ref, o_ref):
    x = x_ref[...]
    ms = jnp.mean(x * x, axis=-1, keepdims=True)
    o_ref[...] = x * lax.rsqrt(ms + RMS_EPS) * g_ref[...]


def _final_norm(x, g, tm):
    m = x.shape[0]
    return pl.pallas_call(
        _final_norm_kernel,
        grid=(m // tm,),
        in_specs=[pl.BlockSpec((tm, D_MODEL), lambda i: (i, 0)), pl.BlockSpec((1, D_MODEL), lambda i: (0, 0))],
        out_specs=pl.BlockSpec((tm, D_MODEL), lambda i: (i, 0)),
        out_shape=jax.ShapeDtypeStruct((m, D_MODEL), F32),
        compiler_params=_cparams(("arbitrary",)),
        name="final_norm",
    )(x, g)


def _pair_block_diag(s):
    h = jnp.swapaxes(s, -1, -2)
    h = h.reshape(h.shape[:-3] + (N_PAIRS, 2, HEAD_DIM, HEAD_DIM))
    eye = jnp.eye(2, dtype=s.dtype)
    out = jnp.einsum("...akv,ab->...akbv", h, eye)
    return out.reshape(h.shape[:-3] + (2 * HEAD_DIM, 2 * HEAD_DIM))


def _pair_unblock(h_bd):
    lead = h_bd.shape[:-3]
    h = h_bd.reshape(lead + (N_PAIRS, 2, HEAD_DIM, 2, HEAD_DIM))
    diag = jnp.stack([h[..., 0, :, 0, :], h[..., 1, :, 1, :]], axis=-3)
    diag = diag.reshape(lead + (N_HEADS, HEAD_DIM, HEAD_DIM))
    return jnp.swapaxes(diag, -1, -2)


def kernel(x_prompt, x_sample, state_wkv, c, c_ctx, norm_g, w_mod, b_mod, ffn_w_in, ffn_w_out, w_in, w_out,
           sgu_ln_g, sgu_ln_b, sgu_w, sgu_b, shift_mu, decay_w0, decay_w2, iclr_a0, iclr_a2, k_k, k_a, r_k,
           gate_w2, lnx_g, lnx_b, final_g):
    batch, seq, d = x_prompt.shape
    dec_batch, dec_seq, _ = x_sample.shape
    depth = w_mod.shape[0]
    assert d == D_MODEL and dec_batch + 1 <= MOD_ROWS
    m_ctx = batch * seq
    m = m_ctx + dec_batch * dec_seq
    tm = 512
    tile_seq = 1024
    tp = 256
    assert seq == tp and dec_seq == tile_seq and m_ctx % tile_seq == 0 and dec_seq % GRID_W == 0

    def row_of_tile_fn(rows):
        n_ctx = m_ctx // rows
        per_lat = dec_seq // rows
        return lambda i: jnp.where(i < n_ctx, 0, 1 + (i - n_ctx) // per_lat)

    ffn_w_in_b = ffn_w_in.astype(BF16)
    ffn_w_out_b = ffn_w_out.astype(BF16)
    w_out_b = w_out.astype(BF16)
    za = 2 * A_WIDTH
    w_in_p = jnp.concatenate(
        [w_in[:, :, za + B_WIDTH:], jnp.zeros((depth, d, C_PAD - C_IN), w_in.dtype),
         w_in[:, :, :za + B_WIDTH]], axis=-1).astype(BF16)
    zeros_cw = jnp.zeros((depth, DECAY_RANK, C_WIDTH), F32)

    def both_dirs(w):
        top = jnp.concatenate([w[:, 0], zeros_cw], axis=-1)
        bot = jnp.concatenate([zeros_cw, w[:, 1]], axis=-1)
        return jnp.concatenate([top, bot], axis=1)

    e_np = (np.arange(256)[:, None] // HEAD_DIM == np.arange(256)[None, :] // HEAD_DIM)
    prep_params = {
        "mu": jnp.pad(shift_mu, ((0, 0), (0, C_PAD - C_IN))).reshape(depth, 1, C_PAD),
        "w0": decay_w0, "w2": both_dirs(decay_w2), "a0": iclr_a0, "a2": both_dirs(iclr_a2),
        "k_k": k_k.reshape(depth, 1, C_WIDTH), "k_a": k_a.reshape(depth, 1, C_WIDTH),
        "r_k": r_k.reshape(depth, 1, C_WIDTH),
        "g2": jnp.pad(gate_w2, ((0, 0), (0, 256 - GATE_RANK), (0, 0))).astype(BF16),
        "e256": jnp.asarray(e_np, BF16),
    }
    cd, sd = _dft_mats(B_GROUP_DIM)
    cs, ss = _dft_mats(seq)
    cl, sl = _dft_mats(dec_seq)
    fnet_consts = (_hi_lo(np.concatenate([cd, sd], axis=1)) + _hi_lo(np.concatenate([cs, -ss], axis=1))
                   + _hi_lo(np.concatenate([cl, -sl], axis=1)))
    sgu_bias = jnp.repeat(jnp.swapaxes(sgu_b, 1, 2), A_WIDTH // A_HEADS, axis=2)
    sgu_w_b = sgu_w.astype(BF16)
    ln_g = sgu_ln_g.reshape(depth, 1, A_WIDTH)
    ln_b = sgu_ln_b.reshape(depth, 1, A_WIDTH)
    lnx_g3 = lnx_g.reshape(depth, 1, C_WIDTH)
    lnx_b3 = lnx_b.reshape(depth, 1, C_WIDTH)
    norm_g4 = norm_g.reshape(depth, 3, 1, d)

    cond = jnp.concatenate([c_ctx[None, :], c, jnp.zeros((MOD_ROWS - 1 - dec_batch, d), F32)], axis=0)
    mod = _modulation(cond, w_mod, b_mod).reshape(depth, MOD_ROWS, N_MOD, 1, d)

    x = jnp.concatenate([x_prompt.reshape(m_ctx, d), x_sample.reshape(dec_batch * dec_seq, d)], axis=0)
    rot = row_of_tile_fn(tm)
    s0_bd = _pair_block_diag(state_wkv)
    n_chunk_ctx = seq // SCAN_T
    n_chunk_lat = dec_seq // SCAN_T
    states = []
    for l in range(depth):
        x = _ffn(x, mod, norm_g4, ffn_w_in_b, ffn_w_out_b, l, 0, 0, rot, tm)
        z = _in_proj(x, mod, norm_g4, w_in_p, l, rot, tm)
        ya = _sgu(z, ln_g, ln_b, sgu_w_b, sgu_bias, l, tm)
        yb = _fnet(z, fnet_consts, m_ctx // tile_seq, seq, tile_seq)
        pr = _rwkv_prep(z, prep_params, l, m_ctx // tp, dec_seq // tp, tp)
        r, v, kk, ldf, ldb, kmf, kmb, kaf, kab, bv, g = pr
        scan_in = (r, v, kk, ldf, kmf, kaf, ldb, kmb, kab)
        yf_c, yb_c, st = _rwkv_scan(scan_in, None, batch, n_chunk_ctx, 0, True)
        yf_l, yb_l = _rwkv_scan(scan_in, s0_bd[:, l], dec_batch, n_chunk_lat, m_ctx // SCAN_T, False)
        states.append(_pair_unblock(st))
        yc = _rwkv_post((yf_c, yb_c), (yf_l, yb_l), bv, g, lnx_g3, lnx_b3, prep_params["e256"], l,
                        m_ctx // tm, tm)
        x = _out_proj(x, mod, ya, yb, yc, w_out_b, l, rot, tm)
        x = _ffn(x, mod, norm_g4, ffn_w_in_b, ffn_w_out_b, l, 2, 1, rot, tm)
    y = _final_norm(x, final_g.reshape(1, d), tm)
    y_prompt = y[:m_ctx].reshape(batch, seq, d)
    y_sample = y[m_ctx:].reshape(dec_batch, dec_seq, d)
    new_state = jnp.stack(states, axis=1)
    return (y_prompt, y_sample, new_state)
```

```python
import functools

import numpy as np
import jax
import jax.numpy as jnp
from jax import lax
from jax.experimental import pallas as pl
from jax.experimental.pallas import tpu as pltpu

F32 = jnp.float32
BF16 = jnp.bfloat16

D_MODEL = 2048
GRID_W = 64
SGU_CHUNK = 128
A_HEADS = 4
A_WIDTH = 512
B_WIDTH = 512
B_GROUP_DIM = 128
C_WIDTH = 1024
HEAD_DIM = 64
N_HEADS = 16
N_PAIRS = 8
DECAY_RANK = 64
ICLR_RANK = 64
GATE_RANK = 160
C_IN = 3488
C_PAD = 3584
Z_COLS = C_PAD + 2 * A_WIDTH + B_WIDTH
D_FF = 5632
N_MOD = 9
RMS_EPS = 1e-6
LN_EPS = 1e-5
LNX_EPS = 64e-5
SCAN_T = 64
MOD_ROWS = 8

VMEM_LIMIT = 56 * 1024 * 1024


def _cparams(sem):
    return pltpu.CompilerParams(dimension_semantics=sem, vmem_limit_bytes=VMEM_LIMIT)


def _dot(a, b):
    return jnp.dot(a, b, preferred_element_type=F32)


def _split3(x):
    hi = x.astype(BF16)
    r1 = x - hi.astype(F32)
    mid = r1.astype(BF16)
    lo = (r1 - mid.astype(F32)).astype(BF16)
    return hi, mid, lo


def _dot_exact_rhs(x, e):
    hi, mid, lo = _split3(x)
    return _dot(hi, e) + _dot(mid, e) + _dot(lo, e)


def _dot3(a, b):
    ah = a.astype(BF16)
    al = (a - ah.astype(F32)).astype(BF16)
    bh = b.astype(BF16)
    bl = (b - bh.astype(F32)).astype(BF16)
    return _dot(ah, bh) + _dot(al, bh) + _dot(ah, bl)


def _sigmoid(x):
    return 1.0 / (1.0 + jnp.exp(-x))


def _silu(x):
    return x * _sigmoid(x)


def _gelu_tanh(x):
    return 0.5 * x * (1.0 + jnp.tanh(0.7978845608028654 * (x + 0.044715 * (x * x * x))))


def _mod_kernel(c_ref, w_ref, b_ref, o_ref):
    s = _silu(c_ref[...]).astype(BF16)
    o_ref[...] = _dot(s, w_ref[...].astype(BF16)) + b_ref[...]


def _modulation(cond, w_mod, b_mod):
    depth = w_mod.shape[0]
    n = w_mod.shape[2]
    tn = 1024
    return pl.pallas_call(
        _mod_kernel,
        grid=(depth, n // tn),
        in_specs=[
            pl.BlockSpec((MOD_ROWS, D_MODEL), lambda l, j: (0, 0)),
            pl.BlockSpec((None, D_MODEL, tn), lambda l, j: (l, 0, j)),
            pl.BlockSpec((None, 1, tn), lambda l, j: (l, 0, j)),
        ],
        out_specs=pl.BlockSpec((None, MOD_ROWS, tn), lambda l, j: (l, 0, j)),
        out_shape=jax.ShapeDtypeStruct((depth, MOD_ROWS, n), F32),
        compiler_params=_cparams(("arbitrary", "arbitrary")),
        name="modulation",
    )(cond, w_mod, b_mod.reshape(depth, 1, n))


def _mod_spec(layer, slot, row_of_tile, width=D_MODEL):
    return pl.BlockSpec((None, None, None, 1, width),
                        lambda i, *_: (layer, row_of_tile(i), slot, 0, 0))


def _norm_spec(layer, slot):
    return pl.BlockSpec((None, None, 1, D_MODEL), lambda i, *_: (layer, slot, 0, 0))


def _modulated_norm(x, g, scale, shift):
    ms = jnp.mean(x * x, axis=-1, keepdims=True)
    xn = x * lax.rsqrt(ms + RMS_EPS) * g
    return xn * (1.0 + scale) + shift


def _ffn_kernel(x_ref, sh_ref, sc_ref, gt_ref, g_ref, wg_ref, wu_ref, wo_ref, o_ref, h_ref, acc_ref):
    j = pl.program_id(1)

    @pl.when(j == 0)
    def _():
        h_ref[...] = _modulated_norm(x_ref[...], g_ref[...], sc_ref[...], sh_ref[...]).astype(BF16)
        acc_ref[...] = jnp.zeros_like(acc_ref)

    h = h_ref[...]
    gate = _dot(h, wg_ref[...])
    up = _dot(h, wu_ref[...])
    a = (_silu(gate) * up).astype(BF16)
    acc_ref[...] += _dot(a, wo_ref[...])

    @pl.when(j == pl.num_programs(1) - 1)
    def _():
        o_ref[...] = x_ref[...] + 0.5 * gt_ref[...] * acc_ref[...]


def _ffn(x, mod, norm_g, w_in, w_out, layer, sub, ffn_idx, row_of_tile, tm, tf=512):
    m = x.shape[0]
    nf = D_FF // tf
    return pl.pallas_call(
        _ffn_kernel,
        grid=(m // tm, nf),
        in_specs=[
            pl.BlockSpec((tm, D_MODEL), lambda i, j: (i, 0)),
            _mod_spec(layer, 3 * sub, row_of_tile),
            _mod_spec(layer, 3 * sub + 1, row_of_tile),
            _mod_spec(layer, 3 * sub + 2, row_of_tile),
            _norm_spec(layer, sub),
            pl.BlockSpec((None, None, D_MODEL, tf), lambda i, j: (layer, ffn_idx, 0, j)),
            pl.BlockSpec((None, None, D_MODEL, tf), lambda i, j: (layer, ffn_idx, 0, nf + j)),
            pl.BlockSpec((None, None, tf, D_MODEL), lambda i, j: (layer, ffn_idx, j, 0)),
        ],
        out_specs=pl.BlockSpec((tm, D_MODEL), lambda i, j: (i, 0)),
        out_shape=jax.ShapeDtypeStruct((m, D_MODEL), F32),
        scratch_shapes=[pltpu.VMEM((tm, D_MODEL), BF16), pltpu.VMEM((tm, D_MODEL), F32)],
        compiler_params=_cparams(("arbitrary", "arbitrary")),
        name="ffn",
    )(x, mod, mod, mod, norm_g, w_in, w_in, w_out)


def _in_kernel(x_ref, sh_ref, sc_ref, g_ref, w_ref, o_ref, h_ref):
    @pl.when(pl.program_id(1) == 0)
    def _():
        h_ref[...] = _modulated_norm(x_ref[...], g_ref[...], sc_ref[...], sh_ref[...]).astype(BF16)

    o_ref[...] = _dot(h_ref[...], w_ref[...])


def _in_proj(x, mod, norm_g, w_in, layer, row_of_tile, tm, tn=1024):
    m = x.shape[0]
    return pl.pallas_call(
        _in_kernel,
        grid=(m // tm, Z_COLS // tn),
        in_specs=[
            pl.BlockSpec((tm, D_MODEL), lambda i, j: (i, 0)),
            _mod_spec(layer, 3, row_of_tile),
            _mod_spec(layer, 4, row_of_tile),
            _norm_spec(layer, 1),
            pl.BlockSpec((None, D_MODEL, tn), lambda i, j: (layer, 0, j)),
        ],
        out_specs=pl.BlockSpec((tm, tn), lambda i, j: (i, j)),
        out_shape=jax.ShapeDtypeStruct((m, Z_COLS), F32),
        scratch_shapes=[pltpu.VMEM((tm, D_MODEL), BF16)],
        compiler_params=_cparams(("arbitrary", "arbitrary")),
        name="in_proj",
    )(x, mod, mod, norm_g, w_in)


def _sgu_kernel(zu_ref, zv_ref, lg_ref, lb_ref, w_ref, bias_ref, o_ref):
    u = _gelu_tanh(zu_ref[...])
    v = _gelu_tanh(zv_ref[...])
    mu = jnp.mean(v, axis=-1, keepdims=True)
    vc = v - mu
    var = jnp.mean(vc * vc, axis=-1, keepdims=True)
    vn = (vc * lax.rsqrt(var + LN_EPS) * lg_ref[...] + lb_ref[...]).astype(BF16)
    hd = A_WIDTH // A_HEADS
    for c in range(u.shape[0] // SGU_CHUNK):
        rows = slice(c * SGU_CHUNK, (c + 1) * SGU_CHUNK)
        for h in range(A_HEADS):
            cols = slice(h * hd, (h + 1) * hd)
            mixed = _dot(w_ref[h], vn[rows, cols]) + bias_ref[:, cols]
            o_ref[rows, cols] = (u[rows, cols] * mixed).astype(BF16)


def _sgu(z, ln_g, ln_b, w_s, bias, layer, tm):
    m = z.shape[0]
    cu = C_PAD // A_WIDTH
    return pl.pallas_call(
        _sgu_kernel,
        grid=(m // tm,),
        in_specs=[
            pl.BlockSpec((tm, A_WIDTH), lambda i: (i, cu)),
            pl.BlockSpec((tm, A_WIDTH), lambda i: (i, cu + 1)),
            pl.BlockSpec((None, 1, A_WIDTH), lambda i: (layer, 0, 0)),
            pl.BlockSpec((None, 1, A_WIDTH), lambda i: (layer, 0, 0)),
            pl.BlockSpec((None, A_HEADS, SGU_CHUNK, SGU_CHUNK), lambda i: (layer, 0, 0, 0)),
            pl.BlockSpec((None, SGU_CHUNK, A_WIDTH), lambda i: (layer, 0, 0)),
        ],
        out_specs=pl.BlockSpec((tm, A_WIDTH), lambda i: (i, 0)),
        out_shape=jax.ShapeDtypeStruct((m, A_WIDTH), BF16),
        compiler_params=_cparams(("arbitrary",)),
        name="sgu",
    )(z, z, ln_g, ln_b, w_s, bias)


def _dft_mats(n):
    idx = np.arange(n)
    ang = 2.0 * np.pi * ((idx[:, None] * idx[None, :]) % n) / n
    return np.cos(ang) / np.sqrt(n), np.sin(ang) / np.sqrt(n)


def _hi_lo(a):
    a32 = jnp.asarray(a, F32)
    hi = a32.astype(BF16)
    lo = (a32 - hi.astype(F32)).astype(BF16)
    return hi, lo


def _fnet_kernel(z_ref, fdh_ref, fdl_ref, fsh_ref, fsl_ref, flh_ref, fll_ref, o_ref, t_ref, *,
                 n_ctx_tiles, seq):
    tm = z_ref.shape[0]
    x = z_ref[...]
    xh = x.astype(BF16)
    xl = (x - xh.astype(F32)).astype(BF16)
    gd = B_GROUP_DIM
    for g in range(B_WIDTH // gd):
        cols = slice(g * gd, (g + 1) * gd)
        t = _dot(xh[:, cols], fdh_ref[...]) + _dot(xl[:, cols], fdh_ref[...]) + _dot(xh[:, cols], fdl_ref[...])
        t_ref[0:tm, cols] = t[:, 0:gd]
        t_ref[tm:2 * tm, cols] = t[:, gd:2 * gd]

    def position_dft(fh_ref, fl_ref, rows_in, rows_out):
        tc = t_ref[rows_in[0], :]
        ts = t_ref[rows_in[1], :]
        tt = jnp.concatenate([tc, ts], axis=0)
        th = tt.astype(BF16)
        tl = (tt - th.astype(F32)).astype(BF16)
        out = _dot(fh_ref[...], th) + _dot(fl_ref[...], th) + _dot(fh_ref[...], tl)
        o_ref[rows_out, :] = out.astype(BF16)

    is_ctx = pl.program_id(0) < n_ctx_tiles

    @pl.when(is_ctx)
    def _():
        for s in range(tm // seq):
            r0 = slice(s * seq, (s + 1) * seq)
            r1 = slice(tm + s * seq, tm + (s + 1) * seq)
            position_dft(fsh_ref, fsl_ref, (r0, r1), r0)

    @pl.when(jnp.logical_not(is_ctx))
    def _():
        position_dft(flh_ref, fll_ref, (slice(0, tm), slice(tm, 2 * tm)), slice(0, tm))


def _fnet(z, consts, n_ctx_tiles, seq, tm):
    m = z.shape[0]
    cb = (C_PAD + 2 * A_WIDTH) // B_WIDTH
    full = lambda a: pl.BlockSpec(a.shape, lambda i: (0,) * a.ndim)
    return pl.pallas_call(
        functools.partial(_fnet_kernel, n_ctx_tiles=n_ctx_tiles, seq=seq),
        grid=(m // tm,),
        in_specs=[pl.BlockSpec((tm, B_WIDTH), lambda i: (i, cb))] + [full(a) for a in consts],
        out_specs=pl.BlockSpec((tm, B_WIDTH), lambda i: (i, 0)),
        out_shape=jax.ShapeDtypeStruct((m, B_WIDTH), BF16),
        scratch_shapes=[pltpu.VMEM((2 * tm, B_WIDTH), F32)],
        compiler_params=_cparams(("arbitrary",)),
        name="fnet",
    )(z, *consts)


def _seg_sum(x, e_ref):
    w = e_ref.shape[0]
    parts = [_dot_exact_rhs(x[:, b * w:(b + 1) * w], e_ref[...]) for b in range(x.shape[1] // w)]
    return parts[0] if len(parts) == 1 else jnp.concatenate(parts, axis=1)


def _prep_kernel(zc_ref, zp_ref, zn_ref, mu_ref, w0_ref, w2_ref, a0_ref, a2_ref, kk_ref, ka_ref, rk_ref,
                 g2_ref, e_ref,
                 r_o, v_o, kk_o, ldf_o, ldb_o, kmf_o, kmb_o, kaf_o, kab_o, bv_o, g_o, *,
                 n_ctx_tiles, tiles_per_lat):
    i = pl.program_id(0)
    tm = zc_ref.shape[0]
    is_ctx = i < n_ctx_tiles
    q = (i - n_ctx_tiles) % tiles_per_lat
    x = zc_ref[...]
    row = lax.broadcasted_iota(jnp.int32, (tm, 1), 0)
    lane = lax.broadcasted_iota(jnp.int32, (1, C_PAD), 1)
    period = jnp.where(is_ctx, tm, GRID_W)
    pos = row & (period - 1)
    prev1 = jnp.where(pos == 0, 0.0, pltpu.roll(x, 1, 0))
    next1 = jnp.where(pos == period - 1, 0.0, pltpu.roll(x, tm - 1, 0))
    up_halo = jnp.where(q > 0, zp_ref[...], 0.0)
    dn_halo = jnp.where(q < tiles_per_lat - 1, zn_ref[...], 0.0)
    up = jnp.concatenate([up_halo, x[:tm - GRID_W]], axis=0)
    down = jnp.concatenate([x[GRID_W:], dn_halo], axis=0)
    half = C_IN // 2
    quarter = C_IN // 4
    zs_ctx = jnp.where(lane < half, prev1, next1)
    zs_lat = jnp.where(lane < quarter, prev1,
                       jnp.where(lane < 2 * quarter, next1, jnp.where(lane < 3 * quarter, up, down)))
    zs = jnp.where(is_ctx, zs_ctx, zs_lat)
    z = x + (zs - x) * mu_ref[...]

    cw = C_WIDTH
    r = z[:, 0:cw]
    k = z[:, cw:2 * cw]
    v = z[:, 2 * cw:3 * cw]
    wd = z[:, 3 * cw:3 * cw + 128]
    ad = z[:, 3 * cw + 128:3 * cw + 256]
    gd = z[:, 3 * cw + 256:C_PAD]

    uw = _dot3(jnp.tanh(wd), w2_ref[...])
    ua = _dot3(ad, a2_ref[...])
    kk_raw = k * kk_ref[...]
    ss = _seg_sum(kk_raw * kk_raw, e_ref)
    kk = kk_raw / jnp.maximum(jnp.sqrt(ss), 1e-12)
    r_o[...] = r
    v_o[...] = v
    kk_o[...] = kk
    km_sum = None
    for d, (ld_o, km_o, ka_o) in enumerate(((ldf_o, kmf_o, kaf_o), (ldb_o, kmb_o, kab_o))):
        u = uw[:, d * cw:(d + 1) * cw] + w0_ref[d:d + 1, :]
        ld_o[...] = -float(np.exp(-0.5)) * _sigmoid(u)
        a = _sigmoid(ua[:, d * cw:(d + 1) * cw] + a0_ref[d:d + 1, :])
        km = k * (1.0 + (a - 1.0) * ka_ref[...])
        km_o[...] = km
        ka_o[...] = kk * a
        km_sum = km if km_sum is None else km_sum + km
    bonus = _seg_sum(r * km_sum * rk_ref[...], e_ref)
    bv_o[...] = bonus * v
    g_o[...] = _dot(_sigmoid(gd).astype(BF16), g2_ref[...])


def _rwkv_prep(z, p, layer, n_ctx_tiles, tiles_per_lat, tm):
    m = z.shape[0]
    hb = tm // GRID_W
    n_halo = m // GRID_W
    lay = lambda *shape: pl.BlockSpec((None,) + shape, lambda i: (layer,) + (0,) * len(shape))
    out = jax.ShapeDtypeStruct((m, C_WIDTH), F32)
    return pl.pallas_call(
        functools.partial(_prep_kernel, n_ctx_tiles=n_ctx_tiles, tiles_per_lat=tiles_per_lat),
        grid=(m // tm,),
        in_specs=[
            pl.BlockSpec((tm, C_PAD), lambda i: (i, 0)),
            pl.BlockSpec((GRID_W, C_PAD), lambda i: (jnp.maximum(i * hb - 1, 0), 0)),
            pl.BlockSpec((GRID_W, C_PAD), lambda i: (jnp.minimum(i * hb + hb, n_halo - 1), 0)),
            lay(1, C_PAD),
            lay(2, C_WIDTH),
            lay(128, 2 * C_WIDTH),
            lay(2, C_WIDTH),
            lay(128, 2 * C_WIDTH),
            lay(1, C_WIDTH),
            lay(1, C_WIDTH),
            lay(1, C_WIDTH),
            lay(256, C_WIDTH),
            pl.BlockSpec((256, 256), lambda i: (0, 0)),
        ],
        out_specs=[pl.BlockSpec((tm, C_WIDTH), lambda i: (i, 0))] * 11,
        out_shape=[out] * 11,
        compiler_params=_cparams(("arbitrary",)),
        name="rwkv_prep",
    )(z, z, z, p["mu"], p["w0"], p["w2"], p["a0"], p["a2"], p["k_k"], p["k_a"], p["r_k"], p["g2"],
      p["e256"])


def _bmm(a, b):
    return jnp.einsum("uik,ukj->uij", a.astype(BF16), b.astype(BF16), preferred_element_type=F32)


def _bmm_nt(a, b):
    return jnp.einsum("uik,ujk->uij", a.astype(BF16), b.astype(BF16), preferred_element_type=F32)


def _block_diag(y, bd_mask):
    return jnp.where(bd_mask, jnp.concatenate([y, y], axis=1), jnp.zeros((), y.dtype))


def _to_units(x):
    w = 2 * HEAD_DIM
    return jnp.stack([x[:, p * w:(p + 1) * w] for p in range(N_PAIRS)], axis=0)


def _scan_chunk(reverse, r, v, kk, ld, km, ka, hs):
    t = SCAN_T
    w = 2 * HEAD_DIM
    ti = lax.broadcasted_iota(jnp.int32, (t, w), 0)
    si = lax.broadcasted_iota(jnp.int32, (t, w), 1) & (HEAD_DIM - 1)
    strict, incl = (si > ti, si >= ti) if reverse else (si < ti, si <= ti)
    eye = jnp.where(si == ti, 1.0, 0.0)
    t_row = lax.broadcasted_iota(jnp.int32, (t, t), 0)
    t_col = lax.broadcasted_iota(jnp.int32, (t, t), 1)
    tri = jnp.where((t_col >= t_row) if reverse else (t_col <= t_row), 1.0, 0.0).astype(BF16)
    bd_mask = (lax.broadcasted_iota(jnp.int32, (w, w), 0) // HEAD_DIM
               == lax.broadcasted_iota(jnp.int32, (w, w), 1) // HEAD_DIM)

    cum = _dot_exact_lhs(tri, ld)
    c_end = cum[0:1, :] if reverse else cum[t - 1:t, :]
    p_end = jnp.exp(c_end - cum)
    p_inv = jnp.exp(-cum)
    q = _to_units(jnp.concatenate([kk * jnp.exp(cum - ld), r * jnp.exp(cum)], axis=0).astype(BF16))
    ai = _to_units((ka * p_inv).astype(BF16))
    ki = _to_units((km * p_inv).astype(BF16))
    vb = _to_units(v.astype(BF16))
    rr = jnp.concatenate([_block_diag(ai, bd_mask), _block_diag(ki, bd_mask)], axis=1)
    sc = _bmm_nt(q, rr)
    la = jnp.where(strict, sc[:, 0:t, 0:w], 0.0)
    lk = jnp.where(strict, sc[:, 0:t, w:2 * w], 0.0)
    ma = jnp.where(incl, sc[:, t:2 * t, 0:w], 0.0)
    mk = jnp.where(incl, sc[:, t:2 * t, w:2 * w], 0.0)

    n_pow = -la
    x_inv = eye + n_pow
    n_pow = _bmm(n_pow, _block_diag(n_pow.astype(BF16), bd_mask))
    for _ in range(int(np.log2(t)) - 2):
        both = _bmm(jnp.concatenate([n_pow, x_inv], axis=1), _block_diag(n_pow.astype(BF16), bd_mask))
        x_inv = x_inv + both[:, t:2 * t]
        n_pow = both[:, 0:t]
    x_inv = x_inv + _bmm(x_inv, _block_diag(n_pow.astype(BF16), bd_mask))

    hb = _bmm_nt(q, hs)
    lv = _bmm(jnp.concatenate([lk, mk], axis=1), _block_diag(vb, bd_mask))
    u = -_bmm(x_inv, _block_diag((hb[:, 0:t] + lv[:, 0:t]).astype(BF16), bd_mask))
    y = hb[:, t:2 * t] + lv[:, t:2 * t] + _bmm(ma, _block_diag(u.astype(BF16), bd_mask))
    lhs = _to_units(jnp.concatenate([ka * p_end, km * p_end], axis=0).astype(BF16))
    rhs_t = jnp.swapaxes(jnp.concatenate([u, _to_units(v)], axis=1), 1, 2)
    upd = _bmm(rhs_t, lhs)
    hs_new = _to_units(jnp.exp(c_end))[:, 0:1, :] * hs + jnp.where(bd_mask, upd, 0.0)
    return y, hs_new


def _dot_exact_lhs(e, x):
    hi, mid, lo = _split3(x)
    return _dot(e, hi) + _dot(e, mid) + _dot(e, lo)


def _scan_kernel(*refs, has_s0, emit_state):
    ins = refs[:12]
    pos = 12
    s0_ref = None
    if has_s0:
        s0_ref = refs[pos]
        pos += 1
    y_refs = refs[pos:pos + 2]
    pos += 2
    st_ref = None
    if emit_state:
        st_ref = refs[pos]
        pos += 1
    h_ref = refs[pos]
    c = pl.program_id(1)

    @pl.when(c == 0)
    def _():
        if has_s0:
            h_ref[...] = s0_ref[...]
        else:
            h_ref[...] = jnp.zeros_like(h_ref)

    w = 2 * HEAD_DIM
    for d, y_ref in enumerate(y_refs):
        r, v, kk, ld, km, ka = (ref[...] for ref in ins[6 * d:6 * d + 6])
        y, hs_new = _scan_chunk(d == 1, r, v, kk, ld, km, ka, h_ref[d])
        for p in range(N_PAIRS):
            y_ref[:, p * w:(p + 1) * w] = y[p]
        h_ref[d] = hs_new

    if emit_state:
        @pl.when(c == pl.num_programs(1) - 1)
        def _():
            st_ref[...] = h_ref[...]


def _rwkv_scan(arrs, s0_bd, n_seq, n_chunk, chunk0, emit_state):
    r, v, kk, ldf, kmf, kaf, ldb, kmb, kab = arrs
    m = n_seq * n_chunk * SCAN_T
    fwd = lambda b, c: (b * n_chunk + c, 0)
    bwd = lambda b, c: (b * n_chunk + (n_chunk - 1 - c), 0)
    fwd_in = lambda b, c: (chunk0 + b * n_chunk + c, 0)
    bwd_in = lambda b, c: (chunk0 + b * n_chunk + (n_chunk - 1 - c), 0)
    blk = lambda imap: pl.BlockSpec((SCAN_T, C_WIDTH), imap)
    in_specs = [blk(fwd_in)] * 6 + [blk(bwd_in)] * 6
    args = [r, v, kk, ldf, kmf, kaf, r, v, kk, ldb, kmb, kab]
    st_shape = (2, N_PAIRS, 2 * HEAD_DIM, 2 * HEAD_DIM)
    st_spec = pl.BlockSpec((None,) + st_shape, lambda b, c: (b, 0, 0, 0, 0))
    has_s0 = s0_bd is not None
    if has_s0:
        in_specs.append(st_spec)
        args.append(s0_bd)
    y_shape = jax.ShapeDtypeStruct((m, C_WIDTH), F32)
    out_specs = [blk(fwd), blk(bwd)]
    out_shape = [y_shape, y_shape]
    if emit_state:
        out_specs.append(st_spec)
        out_shape.append(jax.ShapeDtypeStruct((n_seq,) + st_shape, F32))
    return pl.pallas_call(
        functools.partial(_scan_kernel, has_s0=has_s0, emit_state=emit_state),
        grid=(n_seq, n_chunk),
        in_specs=in_specs,
        out_specs=out_specs,
        out_shape=out_shape,
        scratch_shapes=[pltpu.VMEM(st_shape, F32)],
        compiler_params=_cparams(("arbitrary", "arbitrary")),
        name="rwkv_scan",
    )(*args)


def _post_kernel(yf1_ref, yb1_ref, yf2_ref, yb2_ref, bv_ref, g_ref, lg_ref, lb_ref, e_ref, o_ref, *,
                 n_ctx_tiles):
    is_ctx = pl.program_id(0) < n_ctx_tiles
    y = jnp.where(is_ctx, yf1_ref[...] + yb1_ref[...], yf2_ref[...] + yb2_ref[...])
    inv = 1.0 / HEAD_DIM
    m = _seg_sum(y, e_ref) * inv
    yc = y - m
    var = _seg_sum(yc * yc, e_ref) * inv
    yn = yc * lax.rsqrt(var + LNX_EPS) * lg_ref[...] + lb_ref[...]
    o_ref[...] = ((yn + bv_ref[...]) * g_ref[...]).astype(BF16)


def _rwkv_post(y_ctx, y_lat, bv, g, lnx_g, lnx_b, e256, layer, n_ctx_tiles, tm):
    m = bv.shape[0]
    n_tiles = m // tm
    row = pl.BlockSpec((tm, C_WIDTH), lambda i: (i, 0))
    row_ctx = pl.BlockSpec((tm, C_WIDTH), lambda i: (jnp.minimum(i, n_ctx_tiles - 1), 0))
    row_lat = pl.BlockSpec((tm, C_WIDTH), lambda i: (jnp.maximum(i - n_ctx_tiles, 0), 0))
    vec = pl.BlockSpec((None, 1, C_WIDTH), lambda i: (layer, 0, 0))
    return pl.pallas_call(
        functools.partial(_post_kernel, n_ctx_tiles=n_ctx_tiles),
        grid=(n_tiles,),
        in_specs=[row_ctx, row_ctx, row_lat, row_lat, row, row, vec, vec,
                  pl.BlockSpec((256, 256), lambda i: (0, 0))],
        out_specs=row,
        out_shape=jax.ShapeDtypeStruct((m, C_WIDTH), BF16),
        compiler_params=_cparams(("arbitrary",)),
        name="rwkv_post",
    )(y_ctx[0], y_ctx[1], y_lat[0], y_lat[1], bv, g, lnx_g, lnx_b, e256)


def _out_kernel(x_ref, gt_ref, ya_ref, yb_ref, yc_ref, wa_ref, wb_ref, wc_ref, o_ref):
    y = _dot(ya_ref[...], wa_ref[...]) + _dot(yb_ref[...], wb_ref[...]) + _dot(yc_ref[...], wc_ref[...])
    o_ref[...] = x_ref[...] + gt_ref[...] * y


def _out_proj(x, mod, ya, yb, yc, w_out, layer, row_of_tile, tm):
    m = x.shape[0]
    return pl.pallas_call(
        _out_kernel,
        grid=(m // tm,),
        in_specs=[
            pl.BlockSpec((tm, D_MODEL), lambda i: (i, 0)),
            _mod_spec(layer, 5, row_of_tile),
            pl.BlockSpec((tm, A_WIDTH), lambda i: (i, 0)),
            pl.BlockSpec((tm, B_WIDTH), lambda i: (i, 0)),
            pl.BlockSpec((tm, C_WIDTH), lambda i: (i, 0)),
            pl.BlockSpec((None, A_WIDTH, D_MODEL), lambda i: (layer, 0, 0)),
            pl.BlockSpec((None, B_WIDTH, D_MODEL), lambda i: (layer, 1, 0)),
            pl.BlockSpec((None, C_WIDTH, D_MODEL), lambda i: (layer, 1, 0)),
        ],
        out_specs=pl.BlockSpec((tm, D_MODEL), lambda i: (i, 0)),
        out_shape=jax.ShapeDtypeStruct((m, D_MODEL), F32),
        compiler_params=_cparams(("arbitrary",)),
        name="out_proj",
    )(x, mod, ya, yb, yc, w_out, w_out, w_out)


def _final_norm_kernel(x_ref, g_ref, o_ref):
    x = x_ref[...]
    ms = jnp.mean(x * x, axis=-1, keepdims=True)
    o_ref[...] = x * lax.rsqrt(ms + RMS_EPS) * g_ref[...]


def _final_norm(x, g, tm):
    m = x.shape[0]
    return pl.pallas_call(
        _final_norm_kernel,
        grid=(m // tm,),
        in_specs=[pl.BlockSpec((tm, D_MODEL), lambda i: (i, 0)), pl.BlockSpec((1, D_MODEL), lambda i: (0, 0))],
        out_specs=pl.BlockSpec((tm, D_MODEL), lambda i: (i, 0)),
        out_shape=jax.ShapeDtypeStruct((m, D_MODEL), F32),
        compiler_params=_cparams(("arbitrary",)),
        name="final_norm",
    )(x, g)


def _pair_block_diag(s):
    h = s.reshape(s.shape[:-3] + (N_PAIRS, 2, HEAD_DIM, HEAD_DIM))
    eye = jnp.eye(2, dtype=s.dtype)
    out = jnp.einsum("...avk,ab->...avbk", h, eye)
    return out.reshape(h.shape[:-3] + (2 * HEAD_DIM, 2 * HEAD_DIM))


def _pair_unblock(h_bd):
    lead = h_bd.shape[:-3]
    h = h_bd.reshape(lead + (N_PAIRS, 2, HEAD_DIM, 2, HEAD_DIM))
    diag = jnp.stack([h[..., 0, :, 0, :], h[..., 1, :, 1, :]], axis=-3)
    return diag.reshape(lead + (N_HEADS, HEAD_DIM, HEAD_DIM))


def kernel(x_prompt, x_sample, state_wkv, c, c_ctx, norm_g, w_mod, b_mod, ffn_w_in, ffn_w_out, w_in, w_out,
           sgu_ln_g, sgu_ln_b, sgu_w, sgu_b, shift_mu, decay_w0, decay_w2, iclr_a0, iclr_a2, k_k, k_a, r_k,
           gate_w2, lnx_g, lnx_b, final_g):
    batch, seq, d = x_prompt.shape
    dec_batch, dec_seq, _ = x_sample.shape
    depth = w_mod.shape[0]
    assert d == D_MODEL and dec_batch + 1 <= MOD_ROWS
    m_ctx = batch * seq
    m = m_ctx + dec_batch * dec_seq
    tm = 512
    tile_seq = 1024
    tp = 256
    assert seq == tp and dec_seq == tile_seq and m_ctx % tile_seq == 0 and dec_seq % GRID_W == 0

    def row_of_tile_fn(rows):
        n_ctx = m_ctx // rows
        per_lat = dec_seq // rows
        return lambda i: jnp.where(i < n_ctx, 0, 1 + (i - n_ctx) // per_lat)

    ffn_w_in_b = ffn_w_in.astype(BF16)
    ffn_w_out_b = ffn_w_out.astype(BF16)
    w_out_b = w_out.astype(BF16)
    za = 2 * A_WIDTH
    w_in_p = jnp.concatenate(
        [w_in[:, :, za + B_WIDTH:], jnp.zeros((depth, d, C_PAD - C_IN), w_in.dtype),
         w_in[:, :, :za + B_WIDTH]], axis=-1).astype(BF16)
    zeros_cw = jnp.zeros((depth, DECAY_RANK, C_WIDTH), F32)

    def both_dirs(w):
        top = jnp.concatenate([w[:, 0], zeros_cw], axis=-1)
        bot = jnp.concatenate([zeros_cw, w[:, 1]], axis=-1)
        return jnp.concatenate([top, bot], axis=1)

    e_np = (np.arange(256)[:, None] // HEAD_DIM == np.arange(256)[None, :] // HEAD_DIM)
    prep_params = {
        "mu": jnp.pad(shift_mu, ((0, 0), (0, C_PAD - C_IN))).reshape(depth, 1, C_PAD),
        "w0": decay_w0, "w2": both_dirs(decay_w2), "a0": iclr_a0, "a2": both_dirs(iclr_a2),
        "k_k": k_k.reshape(depth, 1, C_WIDTH), "k_a": k_a.reshape(depth, 1, C_WIDTH),
        "r_k": r_k.reshape(depth, 1, C_WIDTH),
        "g2": jnp.pad(gate_w2, ((0, 0), (0, 256 - GATE_RANK), (0, 0))).astype(BF16),
        "e256": jnp.asarray(e_np, BF16),
    }
    cd, sd = _dft_mats(B_GROUP_DIM)
    cs, ss = _dft_mats(seq)
    cl, sl = _dft_mats(dec_seq)
    fnet_consts = (_hi_lo(np.concatenate([cd, sd], axis=1)) + _hi_lo(np.concatenate([cs, -ss], axis=1))
                   + _hi_lo(np.concatenate([cl, -sl], axis=1)))
    sgu_bias = jnp.repeat(jnp.swapaxes(sgu_b, 1, 2), A_WIDTH // A_HEADS, axis=2)
    sgu_w_b = sgu_w.astype(BF16)
    ln_g = sgu_ln_g.reshape(depth, 1, A_WIDTH)
    ln_b = sgu_ln_b.reshape(depth, 1, A_WIDTH)
    lnx_g3 = lnx_g.reshape(depth, 1, C_WIDTH)
    lnx_b3 = lnx_b.reshape(depth, 1, C_WIDTH)
    norm_g4 = norm_g.reshape(depth, 3, 1, d)

    cond = jnp.concatenate([c_ctx[None, :], c, jnp.zeros((MOD_ROWS - 1 - dec_batch, d), F32)], axis=0)
    mod = _modulation(cond, w_mod, b_mod).reshape(depth, MOD_ROWS, N_MOD, 1, d)

    x = jnp.concatenate([x_prompt.reshape(m_ctx, d), x_sample.reshape(dec_batch * dec_seq, d)], axis=0)
    rot = row_of_tile_fn(tm)
    s0_bd = _pair_block_diag(state_wkv)
    n_chunk_ctx = seq // SCAN_T
    n_chunk_lat = dec_seq // SCAN_T
    states = []
    for l in range(depth):
        x = _ffn(x, mod, norm_g4, ffn_w_in_b, ffn_w_out_b, l, 0, 0, rot, tm)
        z = _in_proj(x, mod, norm_g4, w_in_p, l, rot, tm)
        ya = _sgu(z, ln_g, ln_b, sgu_w_b, sgu_bias, l, tm)
        yb = _fnet(z, fnet_consts, m_ctx // tile_seq, seq, tile_seq)
        pr = _rwkv_prep(z, prep_params, l, m_ctx // tp, dec_seq // tp, tp)
        r, v, kk, ldf, ldb, kmf, kmb, kaf, kab, bv, g = pr
        scan_in = (r, v, kk, ldf, kmf, kaf, ldb, kmb, kab)
        yf_c, yb_c, st = _rwkv_scan(scan_in, None, batch, n_chunk_ctx, 0, True)
        yf_l, yb_l = _rwkv_scan(scan_in, s0_bd[:, l], dec_batch, n_chunk_lat, m_ctx // SCAN_T, False)
        states.append(_pair_unblock(st))
        yc = _rwkv_post((yf_c, yb_c), (yf_l, yb_l), bv, g, lnx_g3, lnx_b3, prep_params["e256"], l,
                        m_ctx // tm, tm)
        x = _out_proj(x, mod, ya, yb, yc, w_out_b, l, rot, tm)
        x = _ffn(x, mod, norm_g4, ffn_w_in_b, ffn_w_out_b, l, 2, 1, rot, tm)
    y = _final_norm(x, final_g.reshape(1, d), tm)
    y_prompt = y[:m_ctx].reshape(batch, seq, d)
    y_sample = y[m_ctx:].reshape(dec_batch, dec_seq, d)
    new_state = jnp.stack(states, axis=1)
    return (y_prompt, y_sample, new_state)
```

```python
import functools

import numpy as np
import jax
import jax.numpy as jnp
from jax import lax
from jax.experimental import pallas as pl
from jax.experimental.pallas import tpu as pltpu

F32 = jnp.float32
BF16 = jnp.bfloat16

D_MODEL = 2048
GRID_W = 64
SGU_CHUNK = 128
A_HEADS = 4
A_WIDTH = 512
B_WIDTH = 512
B_GROUP_DIM = 128
C_WIDTH = 1024
MIX_WIDTH = A_WIDTH + B_WIDTH + C_WIDTH
HEAD_DIM = 64
N_HEADS = 16
N_PAIRS = 8
DECAY_RANK = 64
GATE_RANK = 160
C_IN = 3488
C_PAD = 3584
IN_COLS = 2 * A_WIDTH + B_WIDTH + C_IN
Z_BLOCK = 512
Z_COLS = C_PAD + 2 * A_WIDTH + B_WIDTH
D_FF = 5632
N_MOD = 9
RMS_EPS = 1e-6
LN_EPS = 1e-5
LNX_EPS = 64e-5
SCAN_T = 64
MOD_ROWS = 8

VMEM_LIMIT = 56 * 1024 * 1024


def _cparams(sem):
    return pltpu.CompilerParams(dimension_semantics=sem, vmem_limit_bytes=VMEM_LIMIT)


def _dot(a, b):
    return jnp.dot(a, b, preferred_element_type=F32)


def _split3(x):
    hi = x.astype(BF16)
    r1 = x - hi.astype(F32)
    mid = r1.astype(BF16)
    lo = (r1 - mid.astype(F32)).astype(BF16)
    return hi, mid, lo


def _dot_exact_rhs(x, e):
    hi, mid, lo = _split3(x)
    return _dot(hi, e) + _dot(mid, e) + _dot(lo, e)


def _dot_exact_lhs(e, x):
    hi, mid, lo = _split3(x)
    return _dot(e, hi) + _dot(e, mid) + _dot(e, lo)


def _dot3(a, b):
    ah = a.astype(BF16)
    al = (a - ah.astype(F32)).astype(BF16)
    bh = b.astype(BF16)
    bl = (b - bh.astype(F32)).astype(BF16)
    return _dot(ah, bh) + _dot(al, bh) + _dot(ah, bl)


def _sigmoid(x):
    return 1.0 / (1.0 + jnp.exp(-x))


def _silu(x):
    return x * _sigmoid(x)


def _gelu_tanh(x):
    return 0.5 * x * (1.0 + jnp.tanh(0.7978845608028654 * (x + 0.044715 * (x * x * x))))


def _mod_kernel(c_ref, w_ref, b_ref, o_ref):
    s = _silu(c_ref[...]).astype(BF16)
    o_ref[...] = _dot(s, w_ref[...].astype(BF16)) + b_ref[...]


def _modulation(cond, w_mod, b_mod):
    depth = w_mod.shape[0]
    n = w_mod.shape[2]
    tn = 1024
    return pl.pallas_call(
        _mod_kernel,
        grid=(depth, n // tn),
        in_specs=[
            pl.BlockSpec((MOD_ROWS, D_MODEL), lambda l, j: (0, 0)),
            pl.BlockSpec((None, D_MODEL, tn), lambda l, j: (l, 0, j)),
            pl.BlockSpec((None, 1, tn), lambda l, j: (l, 0, j)),
        ],
        out_specs=pl.BlockSpec((None, MOD_ROWS, tn), lambda l, j: (l, 0, j)),
        out_shape=jax.ShapeDtypeStruct((depth, MOD_ROWS, n), F32),
        compiler_params=_cparams(("arbitrary", "arbitrary")),
        name="modulation",
    )(cond, w_mod, b_mod.reshape(depth, 1, n))


def _mod_spec(layer, slot, row_of_tile):
    return pl.BlockSpec((None, None, None, 1, D_MODEL),
                        lambda i, *_: (layer, row_of_tile(i), slot, 0, 0))


def _norm_spec(layer, slot):
    return pl.BlockSpec((None, None, 1, D_MODEL), lambda i, *_: (layer, slot, 0, 0))


def _norm_matmul_kernel(x_ref, sh_ref, sc_ref, g_ref, *rest, swiglu, valid_cols):
    o_ref, h_ref = rest[-2:]
    j = pl.program_id(1)

    @pl.when(j == 0)
    def _():
        x = x_ref[...]
        ms = jnp.mean(x * x, axis=-1, keepdims=True)
        xn = x * lax.rsqrt(ms + RMS_EPS) * g_ref[...]
        h_ref[...] = (xn * (1.0 + sc_ref[...]) + sh_ref[...]).astype(BF16)

    h = h_ref[...]
    if swiglu:
        wg_ref, wu_ref = rest[:2]
        gate = _dot(h, wg_ref[...].astype(BF16))
        up = _dot(h, wu_ref[...].astype(BF16))
        o_ref[...] = (_silu(gate) * up).astype(o_ref.dtype)
    else:
        out = _dot(h, rest[0][...].astype(BF16))
        tn = out.shape[1]
        if valid_cols % tn:
            col = j * tn + lax.broadcasted_iota(jnp.int32, (1, tn), 1)
            out = jnp.where(col < valid_cols, out, 0.0)
        o_ref[...] = out.astype(o_ref.dtype)


def _norm_matmul(x, mod, norm_g, layer, sub, row_of_tile, tm, w_specs, w_args, n_blocks, out_spec, out_shape,
                 swiglu, valid_cols, name):
    m = x.shape[0]
    return pl.pallas_call(
        functools.partial(_norm_matmul_kernel, swiglu=swiglu, valid_cols=valid_cols),
        grid=(m // tm, n_blocks),
        in_specs=[
            pl.BlockSpec((tm, D_MODEL), lambda i, j: (i, 0)),
            _mod_spec(layer, 3 * sub, row_of_tile),
            _mod_spec(layer, 3 * sub + 1, row_of_tile),
            _norm_spec(layer, sub),
        ] + w_specs,
        out_specs=out_spec,
        out_shape=out_shape,
        scratch_shapes=[pltpu.VMEM((tm, D_MODEL), BF16)],
        compiler_params=_cparams(("arbitrary", "arbitrary")),
        name=name,
    )(x, mod, mod, norm_g, *w_args)


def _ffn_hidden(x, mod, norm_g, ffn_w_in, layer, sub, ffn_idx, row_of_tile, tm, tf=512):
    nf = D_FF // tf
    w_spec = lambda off: pl.BlockSpec((None, None, D_MODEL, tf), lambda i, j: (layer, ffn_idx, 0, off + j))
    return _norm_matmul(
        x, mod, norm_g, layer, sub, row_of_tile, tm, [w_spec(0), w_spec(nf)], [ffn_w_in, ffn_w_in], nf,
        pl.BlockSpec((tm, tf), lambda i, j: (i, j)), jax.ShapeDtypeStruct((x.shape[0], D_FF), BF16),
        True, D_FF, "ffn_hidden")


def _in_proj(x, mod, norm_g, w_in, layer, row_of_tile, tm):
    nb = Z_COLS // Z_BLOCK
    shift = C_PAD // Z_BLOCK
    assert (2 * A_WIDTH + B_WIDTH) % Z_BLOCK == 0 and -(-IN_COLS // Z_BLOCK) == nb
    return _norm_matmul(
        x, mod, norm_g, layer, 1, row_of_tile, tm,
        [pl.BlockSpec((None, D_MODEL, Z_BLOCK), lambda i, j: (layer, 0, j))], [w_in], nb,
        pl.BlockSpec((tm, Z_BLOCK), lambda i, j: (i, (j + shift) % nb)),
        jax.ShapeDtypeStruct((x.shape[0], Z_COLS), F32), False, IN_COLS, "in_proj")


def _resid_matmul_kernel(x_ref, gt_ref, a_ref, w_ref, o_ref, *, coef, tn):
    k = pl.program_id(1)

    @pl.when(k == 0)
    def _():
        o_ref[...] = jnp.zeros_like(o_ref)

    a = a_ref[...]
    for n in range(o_ref.shape[1] // tn):
        cols = slice(n * tn, (n + 1) * tn)
        o_ref[:, cols] += _dot(a, w_ref[:, cols].astype(BF16))

    @pl.when(k == pl.num_programs(1) - 1)
    def _():
        o_ref[...] = x_ref[...] + (coef * gt_ref[...]) * o_ref[...]


def _resid_matmul(x, mod, a, w, w_spec, layer, gate_slot, coef, row_of_tile, tm, tk, name):
    m = x.shape[0]
    return pl.pallas_call(
        functools.partial(_resid_matmul_kernel, coef=coef, tn=512),
        grid=(m // tm, a.shape[1] // tk),
        in_specs=[
            pl.BlockSpec((tm, D_MODEL), lambda i, k: (i, 0)),
            _mod_spec(layer, gate_slot, row_of_tile),
            pl.BlockSpec((tm, tk), lambda i, k: (i, k)),
            w_spec,
        ],
        out_specs=pl.BlockSpec((tm, D_MODEL), lambda i, k: (i, 0)),
        out_shape=jax.ShapeDtypeStruct((m, D_MODEL), F32),
        compiler_params=_cparams(("arbitrary", "arbitrary")),
        name=name,
    )(x, mod, a, w)


def _sgu_kernel(zu_ref, zv_ref, lg_ref, lb_ref, w_ref, bias_ref, o_ref):
    u = _gelu_tanh(zu_ref[...])
    v = _gelu_tanh(zv_ref[...])
    mu = jnp.mean(v, axis=-1, keepdims=True)
    vc = v - mu
    var = jnp.mean(vc * vc, axis=-1, keepdims=True)
    vn = (vc * lax.rsqrt(var + LN_EPS) * lg_ref[...] + lb_ref[...]).astype(BF16)
    hd = A_WIDTH // A_HEADS
    for c in range(u.shape[0] // SGU_CHUNK):
        rows = slice(c * SGU_CHUNK, (c + 1) * SGU_CHUNK)
        for h in range(A_HEADS):
            cols = slice(h * hd, (h + 1) * hd)
            mixed = _dot(w_ref[h], vn[rows, cols]) + bias_ref[:, cols]
            o_ref[rows, cols] = (u[rows, cols] * mixed).astype(BF16)


def _sgu(z, ln_g, ln_b, w_s, bias, layer, tm):
    m = z.shape[0]
    cu = C_PAD // A_WIDTH
    return pl.pallas_call(
        _sgu_kernel,
        grid=(m // tm,),
        in_specs=[
            pl.BlockSpec((tm, A_WIDTH), lambda i: (i, cu)),
            pl.BlockSpec((tm, A_WIDTH), lambda i: (i, cu + 1)),
            pl.BlockSpec((None, 1, A_WIDTH), lambda i: (layer, 0, 0)),
            pl.BlockSpec((None, 1, A_WIDTH), lambda i: (layer, 0, 0)),
            pl.BlockSpec((None, A_HEADS, SGU_CHUNK, SGU_CHUNK), lambda i: (layer, 0, 0, 0)),
            pl.BlockSpec((None, SGU_CHUNK, A_WIDTH), lambda i: (layer, 0, 0)),
        ],
        out_specs=pl.BlockSpec((tm, A_WIDTH), lambda i: (i, 0)),
        out_shape=jax.ShapeDtypeStruct((m, MIX_WIDTH), BF16),
        compiler_params=_cparams(("arbitrary",)),
        name="sgu",
    )(z, z, ln_g, ln_b, w_s, bias)


def _dft_mats(n):
    idx = np.arange(n)
    ang = 2.0 * np.pi * ((idx[:, None] * idx[None, :]) % n) / n
    return np.cos(ang) / np.sqrt(n), np.sin(ang) / np.sqrt(n)


def _hi_lo(a):
    a32 = jnp.asarray(a, F32)
    hi = a32.astype(BF16)
    lo = (a32 - hi.astype(F32)).astype(BF16)
    return hi, lo


def _fnet_kernel(z_ref, fdh_ref, fdl_ref, fsh_ref, fsl_ref, flh_ref, fll_ref, mix_ref, o_ref, t_ref, *,
                 n_ctx_tiles, seq):
    del mix_ref
    tm = z_ref.shape[0]
    x = z_ref[...]
    xh = x.astype(BF16)
    xl = (x - xh.astype(F32)).astype(BF16)
    gd = B_GROUP_DIM
    for g in range(B_WIDTH // gd):
        cols = slice(g * gd, (g + 1) * gd)
        t = _dot(xh[:, cols], fdh_ref[...]) + _dot(xl[:, cols], fdh_ref[...]) + _dot(xh[:, cols], fdl_ref[...])
        t_ref[0:tm, cols] = t[:, 0:gd]
        t_ref[tm:2 * tm, cols] = t[:, gd:2 * gd]

    def position_dft(fh_ref, fl_ref, rows_in, rows_out):
        tc = t_ref[rows_in[0], :]
        ts = t_ref[rows_in[1], :]
        tt = jnp.concatenate([tc, ts], axis=0)
        th = tt.astype(BF16)
        tl = (tt - th.astype(F32)).astype(BF16)
        out = _dot(fh_ref[...], th) + _dot(fl_ref[...], th) + _dot(fh_ref[...], tl)
        o_ref[rows_out, :] = out.astype(BF16)

    is_ctx = pl.program_id(0) < n_ctx_tiles

    @pl.when(is_ctx)
    def _():
        for s in range(tm // seq):
            r0 = slice(s * seq, (s + 1) * seq)
            r1 = slice(tm + s * seq, tm + (s + 1) * seq)
            position_dft(fsh_ref, fsl_ref, (r0, r1), r0)

    @pl.when(jnp.logical_not(is_ctx))
    def _():
        position_dft(flh_ref, fll_ref, (slice(0, tm), slice(tm, 2 * tm)), slice(0, tm))


def _fnet(z, consts, mix, n_ctx_tiles, seq, tm):
    m = z.shape[0]
    cb = (C_PAD + 2 * A_WIDTH) // B_WIDTH
    full = lambda a: pl.BlockSpec(a.shape, lambda i: (0,) * a.ndim)
    return pl.pallas_call(
        functools.partial(_fnet_kernel, n_ctx_tiles=n_ctx_tiles, seq=seq),
        grid=(m // tm,),
        in_specs=([pl.BlockSpec((tm, B_WIDTH), lambda i: (i, cb))] + [full(a) for a in consts]
                  + [pl.BlockSpec(memory_space=pl.ANY)]),
        out_specs=pl.BlockSpec((tm, B_WIDTH), lambda i: (i, A_WIDTH // B_WIDTH)),
        out_shape=jax.ShapeDtypeStruct(mix.shape, mix.dtype),
        input_output_aliases={1 + len(consts): 0},
        scratch_shapes=[pltpu.VMEM((2 * tm, B_WIDTH), F32)],
        compiler_params=_cparams(("arbitrary",)),
        name="fnet",
    )(z, *consts, mix)


def _seg_sum(x, e_ref):
    w = e_ref.shape[0]
    parts = [_dot_exact_rhs(x[:, b * w:(b + 1) * w], e_ref[...]) for b in range(x.shape[1] // w)]
    return parts[0] if len(parts) == 1 else jnp.concatenate(parts, axis=1)


def _prep_kernel(zc_ref, zp_ref, zn_ref, mu_ref, w0_ref, w2_ref, a0_ref, a2_ref, kk_ref, ka_ref, rk_ref,
                 g2_ref, e_ref,
                 r_o, v_o, kk_o, ldf_o, ldb_o, kmf_o, kmb_o, kaf_o, kab_o, bv_o, g_o, *,
                 n_ctx_tiles, tiles_per_lat):
    i = pl.program_id(0)
    tm = zc_ref.shape[0]
    is_ctx = i < n_ctx_tiles
    q = (i - n_ctx_tiles) % tiles_per_lat
    x = zc_ref[...]
    row = lax.broadcasted_iota(jnp.int32, (tm, 1), 0)
    lane = lax.broadcasted_iota(jnp.int32, (1, C_PAD), 1)
    period = jnp.where(is_ctx, tm, GRID_W)
    pos = row & (period - 1)
    prev1 = jnp.where(pos == 0, 0.0, pltpu.roll(x, 1, 0))
    next1 = jnp.where(pos == period - 1, 0.0, pltpu.roll(x, tm - 1, 0))
    up_halo = jnp.where(q > 0, zp_ref[...], 0.0)
    dn_halo = jnp.where(q < tiles_per_lat - 1, zn_ref[...], 0.0)
    up = jnp.concatenate([up_halo, x[:tm - GRID_W]], axis=0)
    down = jnp.concatenate([x[GRID_W:], dn_halo], axis=0)
    half = C_IN // 2
    quarter = C_IN // 4
    zs_ctx = jnp.where(lane < half, prev1, next1)
    zs_lat = jnp.where(lane < quarter, prev1,
                       jnp.where(lane < 2 * quarter, next1, jnp.where(lane < 3 * quarter, up, down)))
    zs = jnp.where(is_ctx, zs_ctx, zs_lat)
    z = x + (zs - x) * mu_ref[...]

    cw = C_WIDTH
    r = z[:, 0:cw]
    k = z[:, cw:2 * cw]
    v = z[:, 2 * cw:3 * cw]
    wd = z[:, 3 * cw:3 * cw + 128]
    ad = z[:, 3 * cw + 128:3 * cw + 256]
    gd = z[:, 3 * cw + 256:C_PAD]

    uw = _dot3(jnp.tanh(wd), w2_ref[...])
    ua = _dot3(ad, a2_ref[...])
    kk_raw = k * kk_ref[...]
    ss = _seg_sum(kk_raw * kk_raw, e_ref)
    kk = kk_raw / jnp.maximum(jnp.sqrt(ss), 1e-12)
    r_o[...] = r
    v_o[...] = v
    kk_o[...] = kk
    km_sum = None
    for d, (ld_o, km_o, ka_o) in enumerate(((ldf_o, kmf_o, kaf_o), (ldb_o, kmb_o, kab_o))):
        u = uw[:, d * cw:(d + 1) * cw] + w0_ref[d:d + 1, :]
        ld_o[...] = -float(np.exp(-0.5)) * _sigmoid(u)
        a = _sigmoid(ua[:, d * cw:(d + 1) * cw] + a0_ref[d:d + 1, :])
        km = k * (1.0 + (a - 1.0) * ka_ref[...])
        km_o[...] = km
        ka_o[...] = kk * a
        km_sum = km if km_sum is None else km_sum + km
    bonus = _seg_sum(r * km_sum * rk_ref[...], e_ref)
    bv_o[...] = bonus * v
    g_o[...] = _dot(_sigmoid(gd).astype(BF16), g2_ref[...])


def _rwkv_prep(z, p, layer, n_ctx_tiles, tiles_per_lat, tm):
    m = z.shape[0]
    hb = tm // GRID_W
    n_halo = m // GRID_W
    lay = lambda *shape: pl.BlockSpec((None,) + shape, lambda i: (layer,) + (0,) * len(shape))
    out = jax.ShapeDtypeStruct((m, C_WIDTH), F32)
    return pl.pallas_call(
        functools.partial(_prep_kernel, n_ctx_tiles=n_ctx_tiles, tiles_per_lat=tiles_per_lat),
        grid=(m // tm,),
        in_specs=[
            pl.BlockSpec((tm, C_PAD), lambda i: (i, 0)),
            pl.BlockSpec((GRID_W, C_PAD), lambda i: (jnp.maximum(i * hb - 1, 0), 0)),
            pl.BlockSpec((GRID_W, C_PAD), lambda i: (jnp.minimum(i * hb + hb, n_halo - 1), 0)),
            lay(1, C_PAD),
            lay(2, C_WIDTH),
            lay(128, 2 * C_WIDTH),
            lay(2, C_WIDTH),
            lay(128, 2 * C_WIDTH),
            lay(1, C_WIDTH),
            lay(1, C_WIDTH),
            lay(1, C_WIDTH),
            lay(256, C_WIDTH),
            pl.BlockSpec((256, 256), lambda i: (0, 0)),
        ],
        out_specs=[pl.BlockSpec((tm, C_WIDTH), lambda i: (i, 0))] * 11,
        out_shape=[out] * 11,
        compiler_params=_cparams(("arbitrary",)),
        name="rwkv_prep",
    )(z, z, z, p["mu"], p["w0"], p["w2"], p["a0"], p["a2"], p["k_k"], p["k_a"], p["r_k"], p["g2"],
      p["e256"])


def _bmm(a, b):
    return jnp.einsum("uik,ukj->uij", a.astype(BF16), b.astype(BF16), preferred_element_type=F32)


def _bmm_nt(a, b):
    return jnp.einsum("uik,ujk->uij", a.astype(BF16), b.astype(BF16), preferred_element_type=F32)


def _block_diag(y, bd_mask):
    return jnp.where(bd_mask, jnp.concatenate([y, y], axis=1), jnp.zeros((), y.dtype))


def _to_units(x):
    w = 2 * HEAD_DIM
    return jnp.stack([x[:, p * w:(p + 1) * w] for p in range(N_PAIRS)], axis=0)


def _scan_chunk(reverse, r, v, kk, ld, km, ka, hs):
    t = SCAN_T
    w = 2 * HEAD_DIM
    ti = lax.broadcasted_iota(jnp.int32, (t, w), 0)
    si = lax.broadcasted_iota(jnp.int32, (t, w), 1) & (HEAD_DIM - 1)
    strict, incl = (si > ti, si >= ti) if reverse else (si < ti, si <= ti)
    eye = jnp.where(si == ti, 1.0, 0.0)
    t_row = lax.broadcasted_iota(jnp.int32, (t, t), 0)
    t_col = lax.broadcasted_iota(jnp.int32, (t, t), 1)
    tri = jnp.where((t_col >= t_row) if reverse else (t_col <= t_row), 1.0, 0.0).astype(BF16)
    bd_mask = (lax.broadcasted_iota(jnp.int32, (w, w), 0) // HEAD_DIM
               == lax.broadcasted_iota(jnp.int32, (w, w), 1) // HEAD_DIM)

    cum = _dot_exact_lhs(tri, ld)
    c_end = cum[0:1, :] if reverse else cum[t - 1:t, :]
    p_end = jnp.exp(c_end - cum)
    p_inv = jnp.exp(-cum)
    q = _to_units(jnp.concatenate([kk * jnp.exp(cum - ld), r * jnp.exp(cum)], axis=0).astype(BF16))
    ai = _to_units((ka * p_inv).astype(BF16))
    ki = _to_units((km * p_inv).astype(BF16))
    vb = _to_units(v.astype(BF16))
    rr = jnp.concatenate([_block_diag(ai, bd_mask), _block_diag(ki, bd_mask)], axis=1)
    sc = _bmm_nt(q, rr)
    la = jnp.where(strict, sc[:, 0:t, 0:w], 0.0)
    lk = jnp.where(strict, sc[:, 0:t, w:2 * w], 0.0)
    ma = jnp.where(incl, sc[:, t:2 * t, 0:w], 0.0)
    mk = jnp.where(incl, sc[:, t:2 * t, w:2 * w], 0.0)

    n_pow = -la
    x_inv = eye + n_pow
    n_pow = _bmm(n_pow, _block_diag(n_pow.astype(BF16), bd_mask))
    for _ in range(int(np.log2(t)) - 2):
        both = _bmm(jnp.concatenate([n_pow, x_inv], axis=1), _block_diag(n_pow.astype(BF16), bd_mask))
        x_inv = x_inv + both[:, t:2 * t]
        n_pow = both[:, 0:t]
    x_inv = x_inv + _bmm(x_inv, _block_diag(n_pow.astype(BF16), bd_mask))

    hb = _bmm_nt(q, hs)
    lv = _bmm(jnp.concatenate([lk, mk], axis=1), _block_diag(vb, bd_mask))
    u = -_bmm(x_inv, _block_diag((hb[:, 0:t] + lv[:, 0:t]).astype(BF16), bd_mask))
    y = hb[:, t:2 * t] + lv[:, t:2 * t] + _bmm(ma, _block_diag(u.astype(BF16), bd_mask))
    lhs = _to_units(jnp.concatenate([ka * p_end, km * p_end], axis=0).astype(BF16))
    rhs_t = jnp.swapaxes(jnp.concatenate([u, _to_units(v)], axis=1), 1, 2)
    upd = _bmm(rhs_t, lhs)
    hs_new = _to_units(jnp.exp(c_end))[:, 0:1, :] * hs + jnp.where(bd_mask, upd, 0.0)
    return y, hs_new


def _scan_kernel(*refs, has_s0, emit_state, aliased_state):
    ins = refs[:12]
    pos = 12
    s0_ref = None
    if has_s0:
        s0_ref = refs[pos]
        pos += 1
    if aliased_state:
        pos += 1
    y_refs = refs[pos:pos + 2]
    pos += 2
    st_ref = None
    if emit_state:
        st_ref = refs[pos]
        pos += 1
    h_ref = refs[pos]
    c = pl.program_id(1)
    hd = HEAD_DIM

    @pl.when(c == 0)
    def _():
        if has_s0:
            zero = jnp.zeros((hd, hd), F32)
            for d in range(2):
                for p in range(N_PAIRS):
                    top = jnp.concatenate([s0_ref[d, 2 * p], zero], axis=1)
                    bot = jnp.concatenate([zero, s0_ref[d, 2 * p + 1]], axis=1)
                    h_ref[d, p] = jnp.concatenate([top, bot], axis=0)
        else:
            h_ref[...] = jnp.zeros_like(h_ref)

    w = 2 * hd
    for d, y_ref in enumerate(y_refs):
        r, v, kk, ld, km, ka = (ref[...] for ref in ins[6 * d:6 * d + 6])
        y, hs_new = _scan_chunk(d == 1, r, v, kk, ld, km, ka, h_ref[d])
        for p in range(N_PAIRS):
            y_ref[:, p * w:(p + 1) * w] = y[p]
        h_ref[d] = hs_new

    if emit_state:
        @pl.when(c == pl.num_programs(1) - 1)
        def _():
            for d in range(2):
                for p in range(N_PAIRS):
                    hs = h_ref[d, p]
                    st_ref[d, 2 * p] = hs[0:hd, 0:hd]
                    st_ref[d, 2 * p + 1] = hs[hd:w, hd:w]


def _rwkv_scan(arrs, n_seq, n_chunk, chunk0, layer, s0=None, state_shape=None, state_prev=None):
    r, v, kk, ldf, kmf, kaf, ldb, kmb, kab = arrs
    m = n_seq * n_chunk * SCAN_T
    fwd = lambda b, c: (b * n_chunk + c, 0)
    bwd = lambda b, c: (b * n_chunk + (n_chunk - 1 - c), 0)
    fwd_in = lambda b, c: (chunk0 + b * n_chunk + c, 0)
    bwd_in = lambda b, c: (chunk0 + b * n_chunk + (n_chunk - 1 - c), 0)
    blk = lambda imap: pl.BlockSpec((SCAN_T, C_WIDTH), imap)
    in_specs = [blk(fwd_in)] * 6 + [blk(bwd_in)] * 6
    args = [r, v, kk, ldf, kmf, kaf, r, v, kk, ldb, kmb, kab]
    st_spec = pl.BlockSpec((None, None, 2, N_HEADS, HEAD_DIM, HEAD_DIM), lambda b, c: (b, layer, 0, 0, 0, 0))
    if s0 is not None:
        in_specs.append(st_spec)
        args.append(s0)
    aliases = {}
    if state_prev is not None:
        aliases = {len(args): 2}
        in_specs.append(pl.BlockSpec(memory_space=pl.ANY))
        args.append(state_prev)
    y_shape = jax.ShapeDtypeStruct((m, C_WIDTH), F32)
    out_specs = [blk(fwd), blk(bwd)]
    out_shape = [y_shape, y_shape]
    if state_shape is not None:
        out_specs.append(st_spec)
        out_shape.append(jax.ShapeDtypeStruct(state_shape, F32))
    return pl.pallas_call(
        functools.partial(_scan_kernel, has_s0=s0 is not None, emit_state=state_shape is not None,
                          aliased_state=state_prev is not None),
        grid=(n_seq, n_chunk),
        in_specs=in_specs,
        out_specs=out_specs,
        out_shape=out_shape,
        input_output_aliases=aliases,
        scratch_shapes=[pltpu.VMEM((2, N_PAIRS, 2 * HEAD_DIM, 2 * HEAD_DIM), F32)],
        compiler_params=_cparams(("arbitrary", "arbitrary")),
        name="rwkv_scan",
    )(*args)


def _post_kernel(yf1_ref, yb1_ref, yf2_ref, yb2_ref, bv_ref, g_ref, lg_ref, lb_ref, e_ref, mix_ref, o_ref, *,
                 n_ctx_tiles):
    del mix_ref
    is_ctx = pl.program_id(0) < n_ctx_tiles
    y = jnp.where(is_ctx, yf1_ref[...] + yb1_ref[...], yf2_ref[...] + yb2_ref[...])
    inv = 1.0 / HEAD_DIM
    m = _seg_sum(y, e_ref) * inv
    yc = y - m
    var = _seg_sum(yc * yc, e_ref) * inv
    yn = yc * lax.rsqrt(var + LNX_EPS) * lg_ref[...] + lb_ref[...]
    o_ref[...] = ((yn + bv_ref[...]) * g_ref[...]).astype(BF16)


def _rwkv_post(y_ctx, y_lat, bv, g, lnx_g, lnx_b, e256, mix, layer, n_ctx_tiles, tm):
    m = bv.shape[0]
    n_tiles = m // tm
    row = pl.BlockSpec((tm, C_WIDTH), lambda i: (i, 0))
    row_ctx = pl.BlockSpec((tm, C_WIDTH), lambda i: (jnp.minimum(i, n_ctx_tiles - 1), 0))
    row_lat = pl.BlockSpec((tm, C_WIDTH), lambda i: (jnp.maximum(i - n_ctx_tiles, 0), 0))
    vec = pl.BlockSpec((None, 1, C_WIDTH), lambda i: (layer, 0, 0))
    return pl.pallas_call(
        functools.partial(_post_kernel, n_ctx_tiles=n_ctx_tiles),
        grid=(n_tiles,),
        in_specs=[row_ctx, row_ctx, row_lat, row_lat, row, row, vec, vec,
                  pl.BlockSpec((256, 256), lambda i: (0, 0)), pl.BlockSpec(memory_space=pl.ANY)],
        out_specs=pl.BlockSpec((tm, C_WIDTH), lambda i: (i, (A_WIDTH + B_WIDTH) // C_WIDTH)),
        out_shape=jax.ShapeDtypeStruct(mix.shape, mix.dtype),
        input_output_aliases={9: 0},
        compiler_params=_cparams(("arbitrary",)),
        name="rwkv_post",
    )(y_ctx[0], y_ctx[1], y_lat[0], y_lat[1], bv, g, lnx_g, lnx_b, e256, mix)


def _final_norm_kernel(x_ref, g_ref, op_ref, os_ref, *, n_ctx_tiles):
    x = x_ref[...]
    ms = jnp.mean(x * x, axis=-1, keepdims=True)
    y = x * lax.rsqrt(ms + RMS_EPS) * g_ref[...]
    i = pl.program_id(0)

    @pl.when(i < n_ctx_tiles)
    def _():
        op_ref[...] = y

    @pl.when(i >= n_ctx_tiles)
    def _():
        os_ref[...] = y


def _final_norm(x, g, m_ctx, tm):
    m = x.shape[0]
    n_ctx_tiles = m_ctx // tm
    return pl.pallas_call(
        functools.partial(_final_norm_kernel, n_ctx_tiles=n_ctx_tiles),
        grid=(m // tm,),
        in_specs=[pl.BlockSpec((tm, D_MODEL), lambda i: (i, 0)), pl.BlockSpec((1, D_MODEL), lambda i: (0, 0))],
        out_specs=[pl.BlockSpec((tm, D_MODEL), lambda i: (jnp.minimum(i, n_ctx_tiles - 1), 0)),
                   pl.BlockSpec((tm, D_MODEL), lambda i: (jnp.maximum(i - n_ctx_tiles, 0), 0))],
        out_shape=[jax.ShapeDtypeStruct((m_ctx, D_MODEL), F32), jax.ShapeDtypeStruct((m - m_ctx, D_MODEL), F32)],
        compiler_params=_cparams(("arbitrary",)),
        name="final_norm",
    )(x, g)


def kernel(x_prompt, x_sample, state_wkv, c, c_ctx, norm_g, w_mod, b_mod, ffn_w_in, ffn_w_out, w_in, w_out,
           sgu_ln_g, sgu_ln_b, sgu_w, sgu_b, shift_mu, decay_w0, decay_w2, iclr_a0, iclr_a2, k_k, k_a, r_k,
           gate_w2, lnx_g, lnx_b, final_g):
    batch, seq, d = x_prompt.shape
    dec_batch, dec_seq, _ = x_sample.shape
    depth = w_mod.shape[0]
    assert d == D_MODEL and dec_batch + 1 <= MOD_ROWS
    m_ctx = batch * seq
    tm = 1024
    ts = 512
    tp = 256
    assert seq == tp and dec_seq == tm and m_ctx % tm == 0 and dec_seq % GRID_W == 0

    def row_of_tile_fn(rows):
        n_ctx = m_ctx // rows
        per_lat = dec_seq // rows
        return lambda i: jnp.where(i < n_ctx, 0, 1 + (i - n_ctx) // per_lat)

    zeros_cw = jnp.zeros((depth, DECAY_RANK, C_WIDTH), F32)

    def both_dirs(w):
        top = jnp.concatenate([w[:, 0], zeros_cw], axis=-1)
        bot = jnp.concatenate([zeros_cw, w[:, 1]], axis=-1)
        return jnp.concatenate([top, bot], axis=1)

    e_np = (np.arange(256)[:, None] // HEAD_DIM == np.arange(256)[None, :] // HEAD_DIM)
    prep_params = {
        "mu": jnp.pad(shift_mu, ((0, 0), (0, C_PAD - C_IN))).reshape(depth, 1, C_PAD),
        "w0": decay_w0, "w2": both_dirs(decay_w2), "a0": iclr_a0, "a2": both_dirs(iclr_a2),
        "k_k": k_k.reshape(depth, 1, C_WIDTH), "k_a": k_a.reshape(depth, 1, C_WIDTH),
        "r_k": r_k.reshape(depth, 1, C_WIDTH),
        "g2": jnp.pad(gate_w2, ((0, 0), (0, 256 - GATE_RANK), (0, 0))).astype(BF16),
        "e256": jnp.asarray(e_np, BF16),
    }
    cd, sd = _dft_mats(B_GROUP_DIM)
    cs, ss = _dft_mats(seq)
    cl, sl = _dft_mats(dec_seq)
    fnet_consts = (_hi_lo(np.concatenate([cd, sd], axis=1)) + _hi_lo(np.concatenate([cs, -ss], axis=1))
                   + _hi_lo(np.concatenate([cl, -sl], axis=1)))
    sgu_bias = jnp.repeat(jnp.swapaxes(sgu_b, 1, 2), A_WIDTH // A_HEADS, axis=2)
    sgu_w_b = sgu_w.astype(BF16)
    ln_g = sgu_ln_g.reshape(depth, 1, A_WIDTH)
    ln_b = sgu_ln_b.reshape(depth, 1, A_WIDTH)
    lnx_g3 = lnx_g.reshape(depth, 1, C_WIDTH)
    lnx_b3 = lnx_b.reshape(depth, 1, C_WIDTH)
    norm_g4 = norm_g.reshape(depth, 3, 1, d)

    cond = jnp.concatenate([c_ctx[None, :], c, jnp.zeros((MOD_ROWS - 1 - dec_batch, d), F32)], axis=0)
    mod = _modulation(cond, w_mod, b_mod).reshape(depth, MOD_ROWS, N_MOD, 1, d)

    x = jnp.concatenate([x_prompt.reshape(m_ctx, d), x_sample.reshape(dec_batch * dec_seq, d)], axis=0)
    rot = row_of_tile_fn(tm)
    n_chunk_ctx = seq // SCAN_T
    n_chunk_lat = dec_seq // SCAN_T
    state_shape = (batch, depth, 2, N_HEADS, HEAD_DIM, HEAD_DIM)
    tk = 512
    ffn_out_spec = lambda l, f: pl.BlockSpec((None, None, tk, D_MODEL), lambda i, k: (l, f, k, 0))
    new_state = None
    for l in range(depth):
        a = _ffn_hidden(x, mod, norm_g4, ffn_w_in, l, 0, 0, rot, tm)
        x = _resid_matmul(x, mod, a, ffn_w_out, ffn_out_spec(l, 0), l, 2, 0.5, rot, tm, tk, "ffn_out")
        z = _in_proj(x, mod, norm_g4, w_in, l, rot, tm)
        mix = _sgu(z, ln_g, ln_b, sgu_w_b, sgu_bias, l, ts)
        mix = _fnet(z, fnet_consts, mix, m_ctx // tm, seq, tm)
        pr = _rwkv_prep(z, prep_params, l, m_ctx // tp, dec_seq // tp, tp)
        r, v, kk, ldf, ldb, kmf, kmb, kaf, kab, bv, g = pr
        scan_in = (r, v, kk, ldf, kmf, kaf, ldb, kmb, kab)
        yf_c, yb_c, new_state = _rwkv_scan(scan_in, batch, n_chunk_ctx, 0, l, state_shape=state_shape,
                                           state_prev=new_state)
        yf_l, yb_l = _rwkv_scan(scan_in, dec_batch, n_chunk_lat, m_ctx // SCAN_T, l, s0=state_wkv)
        mix = _rwkv_post((yf_c, yb_c), (yf_l, yb_l), bv, g, lnx_g3, lnx_b3, prep_params["e256"], mix, l,
                         m_ctx // ts, ts)
        x = _resid_matmul(x, mod, mix, w_out, pl.BlockSpec((None, tk, D_MODEL), lambda i, k: (l, k, 0)),
                          l, 5, 1.0, rot, tm, tk, "mix_out")
        a = _ffn_hidden(x, mod, norm_g4, ffn_w_in, l, 2, 1, rot, tm)
        x = _resid_matmul(x, mod, a, ffn_w_out, ffn_out_spec(l, 1), l, 8, 0.5, rot, tm, tk, "ffn_out")
    y_ctx, y_lat = _final_norm(x, final_g.reshape(1, d), m_ctx, ts)
    return (y_ctx.reshape(batch, seq, d), y_lat.reshape(dec_batch, dec_seq, d), new_state)
```

```python
import functools

import numpy as np
import jax
import jax.numpy as jnp
from jax import lax
from jax.experimental import pallas as pl
from jax.experimental.pallas import tpu as pltpu

F32 = jnp.float32
BF16 = jnp.bfloat16

D_MODEL = 2048
GRID_W = 64
SGU_CHUNK = 128
A_HEADS = 4
A_WIDTH = 512
B_WIDTH = 512
B_GROUP_DIM = 128
C_WIDTH = 1024
MIX_WIDTH = A_WIDTH + B_WIDTH + C_WIDTH
HEAD_DIM = 64
N_HEADS = 16
UNIT_HEADS = 2
UNIT_W = UNIT_HEADS * HEAD_DIM
N_UNITS = N_HEADS // UNIT_HEADS
DECAY_RANK = 64
GATE_RANK = 160
C_IN = 3488
C_PAD = 3584
IN_COLS = 2 * A_WIDTH + B_WIDTH + C_IN
Z_BLOCK = 512
Z_COLS = C_PAD + 2 * A_WIDTH + B_WIDTH
D_FF = 5632
N_MOD = 9
RMS_EPS = 1e-6
LN_EPS = 1e-5
LNX_EPS = 64e-5
SCAN_T = 64
MOD_ROWS = 8

VMEM_LIMIT = 56 * 1024 * 1024
FFN_VMEM_LIMIT = 60 * 1024 * 1024


def _cparams(sem):
    return pltpu.CompilerParams(dimension_semantics=sem, vmem_limit_bytes=VMEM_LIMIT)


def _dot(a, b):
    return jnp.dot(a, b, preferred_element_type=F32)


def _split3(x):
    hi = x.astype(BF16)
    r1 = x - hi.astype(F32)
    mid = r1.astype(BF16)
    lo = (r1 - mid.astype(F32)).astype(BF16)
    return hi, mid, lo


def _dot_exact_rhs(x, e):
    hi, mid, lo = _split3(x)
    return _dot(hi, e) + _dot(mid, e) + _dot(lo, e)


def _dot_exact_lhs(e, x):
    hi, mid, lo = _split3(x)
    return _dot(e, hi) + _dot(e, mid) + _dot(e, lo)


def _dot3(a, b):
    ah = a.astype(BF16)
    al = (a - ah.astype(F32)).astype(BF16)
    bh = b.astype(BF16)
    bl = (b - bh.astype(F32)).astype(BF16)
    return _dot(ah, bh) + _dot(al, bh) + _dot(ah, bl)


def _sigmoid(x):
    return 1.0 / (1.0 + jnp.exp(-x))


def _silu(x):
    return x * _sigmoid(x)


def _gelu_tanh(x):
    return 0.5 * x * (1.0 + jnp.tanh(0.7978845608028654 * (x + 0.044715 * (x * x * x))))


def _mod_kernel(c_ref, w_ref, b_ref, o_ref):
    s = _silu(c_ref[...]).astype(BF16)
    o_ref[...] = _dot(s, w_ref[...].astype(BF16)) + b_ref[...]


def _modulation(cond, w_mod, b_mod):
    depth = w_mod.shape[0]
    n = w_mod.shape[2]
    tn = 1024
    return pl.pallas_call(
        _mod_kernel,
        grid=(depth, n // tn),
        in_specs=[
            pl.BlockSpec((MOD_ROWS, D_MODEL), lambda l, j: (0, 0)),
            pl.BlockSpec((None, D_MODEL, tn), lambda l, j: (l, 0, j)),
            pl.BlockSpec((None, 1, tn), lambda l, j: (l, 0, j)),
        ],
        out_specs=pl.BlockSpec((None, MOD_ROWS, tn), lambda l, j: (l, 0, j)),
        out_shape=jax.ShapeDtypeStruct((depth, MOD_ROWS, n), F32),
        compiler_params=_cparams(("arbitrary", "arbitrary")),
        name="modulation",
    )(cond, w_mod, b_mod.reshape(depth, 1, n))


def _mod_spec(layer, slot, row_of_tile):
    return pl.BlockSpec((None, None, None, 1, D_MODEL),
                        lambda i, *_: (layer, row_of_tile(i), slot, 0, 0))


def _norm_spec(layer, slot):
    return pl.BlockSpec((None, None, 1, D_MODEL), lambda i, *_: (layer, slot, 0, 0))


def _modulated_norm(x_ref, g_ref, sc_ref, sh_ref):
    x = x_ref[...]
    rs = lax.rsqrt(jnp.mean(x * x, axis=-1, keepdims=True) + RMS_EPS)
    gain = g_ref[...] * (1.0 + sc_ref[...])
    return ((x * rs) * gain + sh_ref[...]).astype(BF16)


def _in_proj_kernel(x_ref, sh_ref, sc_ref, g_ref, wt_ref, o_ref, h_ref, *, valid_cols):
    j = pl.program_id(1)

    @pl.when(j == 0)
    def _():
        h_ref[...] = _modulated_norm(x_ref, g_ref, sc_ref, sh_ref)

    out = lax.dot_general(h_ref[...], wt_ref[...].astype(BF16), (((1,), (1,)), ((), ())),
                          preferred_element_type=F32)
    tn = out.shape[1]
    if valid_cols % tn:
        col = j * tn + lax.broadcasted_iota(jnp.int32, (1, tn), 1)
        out = jnp.where(col < valid_cols, out, 0.0)
    o_ref[...] = out


def _in_proj(x, mod, norm_g, w_in_t, layer, row_of_tile, tm):
    m = x.shape[0]
    nb = Z_COLS // Z_BLOCK
    shift = C_PAD // Z_BLOCK
    assert (2 * A_WIDTH + B_WIDTH) % Z_BLOCK == 0 and -(-IN_COLS // Z_BLOCK) == nb
    return pl.pallas_call(
        functools.partial(_in_proj_kernel, valid_cols=IN_COLS),
        grid=(m // tm, nb),
        in_specs=[
            pl.BlockSpec((tm, D_MODEL), lambda i, j: (i, 0)),
            _mod_spec(layer, 3, row_of_tile),
            _mod_spec(layer, 4, row_of_tile),
            _norm_spec(layer, 1),
            pl.BlockSpec((None, Z_BLOCK, D_MODEL), lambda i, j: (layer, j, 0)),
        ],
        out_specs=pl.BlockSpec((tm, Z_BLOCK), lambda i, j: (i, (j + shift) % nb)),
        out_shape=jax.ShapeDtypeStruct((m, Z_COLS), F32),
        scratch_shapes=[pltpu.VMEM((tm, D_MODEL), BF16)],
        compiler_params=_cparams(("arbitrary", "arbitrary")),
        name="in_proj",
    )(x, mod, mod, norm_g, w_in_t)


def _ffn_kernel(x_ref, sh_ref, sc_ref, gt_ref, g_ref, wg_ref, wu_ref, wo_ref, o_ref, h_ref, *, tn):
    j = pl.program_id(1)

    @pl.when(j == 0)
    def _():
        h_ref[...] = _modulated_norm(x_ref, g_ref, sc_ref, sh_ref)
        o_ref[...] = jnp.zeros_like(o_ref)

    h = h_ref[...]
    gate = _dot(h, wg_ref[...].astype(BF16))
    up = _dot(h, wu_ref[...].astype(BF16))
    a = (_silu(gate) * up).astype(BF16)
    for n in range(o_ref.shape[1] // tn):
        cols = slice(n * tn, (n + 1) * tn)
        o_ref[:, cols] += _dot(a, wo_ref[:, cols].astype(BF16))

    @pl.when(j == pl.num_programs(1) - 1)
    def _():
        o_ref[...] = x_ref[...] + (0.5 * gt_ref[...]) * o_ref[...]


def _ffn(x, mod, norm_g, w_in, w_out, layer, sub, ffn_idx, row_of_tile, tm, tf=256):
    m = x.shape[0]
    nf = D_FF // tf
    return pl.pallas_call(
        functools.partial(_ffn_kernel, tn=512),
        grid=(m // tm, nf),
        in_specs=[
            pl.BlockSpec((tm, D_MODEL), lambda i, j: (i, 0)),
            _mod_spec(layer, 3 * sub, row_of_tile),
            _mod_spec(layer, 3 * sub + 1, row_of_tile),
            _mod_spec(layer, 3 * sub + 2, row_of_tile),
            _norm_spec(layer, sub),
            pl.BlockSpec((None, None, D_MODEL, tf), lambda i, j: (layer, ffn_idx, 0, j)),
            pl.BlockSpec((None, None, D_MODEL, tf), lambda i, j: (layer, ffn_idx, 0, nf + j)),
            pl.BlockSpec((None, None, tf, D_MODEL), lambda i, j: (layer, ffn_idx, j, 0)),
        ],
        out_specs=pl.BlockSpec((tm, D_MODEL), lambda i, j: (i, 0)),
        out_shape=jax.ShapeDtypeStruct((m, D_MODEL), F32),
        scratch_shapes=[pltpu.VMEM((tm, D_MODEL), BF16)],
        compiler_params=pltpu.CompilerParams(dimension_semantics=("arbitrary", "arbitrary"),
                                             vmem_limit_bytes=FFN_VMEM_LIMIT),
        name="ffn",
    )(x, mod, mod, mod, norm_g, w_in, w_in, w_out)


def _resid_matmul_kernel(x_ref, gt_ref, a_ref, w_ref, o_ref, *, coef, tn):
    k = pl.program_id(1)

    @pl.when(k == 0)
    def _():
        o_ref[...] = jnp.zeros_like(o_ref)

    a = a_ref[...]
    for n in range(o_ref.shape[1] // tn):
        cols = slice(n * tn, (n + 1) * tn)
        o_ref[:, cols] += _dot(a, w_ref[:, cols].astype(BF16))

    @pl.when(k == pl.num_programs(1) - 1)
    def _():
        o_ref[...] = x_ref[...] + (coef * gt_ref[...]) * o_ref[...]


def _resid_matmul(x, mod, a, w, w_spec, layer, gate_slot, coef, row_of_tile, tm, tk, name):
    m = x.shape[0]
    return pl.pallas_call(
        functools.partial(_resid_matmul_kernel, coef=coef, tn=512),
        grid=(m // tm, a.shape[1] // tk),
        in_specs=[
            pl.BlockSpec((tm, D_MODEL), lambda i, k: (i, 0)),
            _mod_spec(layer, gate_slot, row_of_tile),
            pl.BlockSpec((tm, tk), lambda i, k: (i, k)),
            w_spec,
        ],
        out_specs=pl.BlockSpec((tm, D_MODEL), lambda i, k: (i, 0)),
        out_shape=jax.ShapeDtypeStruct((m, D_MODEL), F32),
        compiler_params=_cparams(("arbitrary", "arbitrary")),
        name=name,
    )(x, mod, a, w)


def _sgu_kernel(zu_ref, zv_ref, lg_ref, lb_ref, w_ref, bias_ref, o_ref):
    u = _gelu_tanh(zu_ref[...])
    v = _gelu_tanh(zv_ref[...])
    mu = jnp.mean(v, axis=-1, keepdims=True)
    vc = v - mu
    var = jnp.mean(vc * vc, axis=-1, keepdims=True)
    vn = (vc * lax.rsqrt(var + LN_EPS) * lg_ref[...] + lb_ref[...]).astype(BF16)
    hd = A_WIDTH // A_HEADS
    for c in range(u.shape[0] // SGU_CHUNK):
        rows = slice(c * SGU_CHUNK, (c + 1) * SGU_CHUNK)
        for h in range(A_HEADS):
            cols = slice(h * hd, (h + 1) * hd)
            mixed = _dot(w_ref[h], vn[rows, cols]) + bias_ref[:, cols]
            o_ref[rows, cols] = (u[rows, cols] * mixed).astype(BF16)


def _sgu(z, ln_g, ln_b, w_s, bias, layer, tm):
    m = z.shape[0]
    cu = C_PAD // A_WIDTH
    return pl.pallas_call(
        _sgu_kernel,
        grid=(m // tm,),
        in_specs=[
            pl.BlockSpec((tm, A_WIDTH), lambda i: (i, cu)),
            pl.BlockSpec((tm, A_WIDTH), lambda i: (i, cu + 1)),
            pl.BlockSpec((None, 1, A_WIDTH), lambda i: (layer, 0, 0)),
            pl.BlockSpec((None, 1, A_WIDTH), lambda i: (layer, 0, 0)),
            pl.BlockSpec((None, A_HEADS, SGU_CHUNK, SGU_CHUNK), lambda i: (layer, 0, 0, 0)),
            pl.BlockSpec((None, SGU_CHUNK, A_WIDTH), lambda i: (layer, 0, 0)),
        ],
        out_specs=pl.BlockSpec((tm, A_WIDTH), lambda i: (i, 0)),
        out_shape=jax.ShapeDtypeStruct((m, MIX_WIDTH), BF16),
        compiler_params=_cparams(("arbitrary",)),
        name="sgu",
    )(z, z, ln_g, ln_b, w_s, bias)


def _dft_mats(n):
    idx = np.arange(n)
    ang = 2.0 * np.pi * ((idx[:, None] * idx[None, :]) % n) / n
    return np.cos(ang) / np.sqrt(n), np.sin(ang) / np.sqrt(n)


def _hi_lo(a):
    a32 = jnp.asarray(a, F32)
    hi = a32.astype(BF16)
    lo = (a32 - hi.astype(F32)).astype(BF16)
    return hi, lo


def _fnet_kernel(z_ref, fdh_ref, fdl_ref, fsh_ref, fsl_ref, flh_ref, fll_ref, mix_ref, o_ref, t_ref, *,
                 n_ctx_tiles, seq):
    del mix_ref
    tm = z_ref.shape[0]
    x = z_ref[...]
    xh = x.astype(BF16)
    xl = (x - xh.astype(F32)).astype(BF16)
    gd = B_GROUP_DIM
    for g in range(B_WIDTH // gd):
        cols = slice(g * gd, (g + 1) * gd)
        t = _dot(xh[:, cols], fdh_ref[...]) + _dot(xl[:, cols], fdh_ref[...]) + _dot(xh[:, cols], fdl_ref[...])
        t_ref[0:tm, cols] = t[:, 0:gd]
        t_ref[tm:2 * tm, cols] = t[:, gd:2 * gd]

    def position_dft(fh_ref, fl_ref, rows_in, rows_out):
        tc = t_ref[rows_in[0], :]
        ts = t_ref[rows_in[1], :]
        tt = jnp.concatenate([tc, ts], axis=0)
        th = tt.astype(BF16)
        tl = (tt - th.astype(F32)).astype(BF16)
        out = _dot(fh_ref[...], th) + _dot(fl_ref[...], th) + _dot(fh_ref[...], tl)
        o_ref[rows_out, :] = out.astype(BF16)

    is_ctx = pl.program_id(0) < n_ctx_tiles

    @pl.when(is_ctx)
    def _():
        for s in range(tm // seq):
            r0 = slice(s * seq, (s + 1) * seq)
            r1 = slice(tm + s * seq, tm + (s + 1) * seq)
            position_dft(fsh_ref, fsl_ref, (r0, r1), r0)

    @pl.when(jnp.logical_not(is_ctx))
    def _():
        position_dft(flh_ref, fll_ref, (slice(0, tm), slice(tm, 2 * tm)), slice(0, tm))


def _fnet(z, consts, mix, n_ctx_tiles, seq, tm):
    m = z.shape[0]
    cb = (C_PAD + 2 * A_WIDTH) // B_WIDTH
    full = lambda a: pl.BlockSpec(a.shape, lambda i: (0,) * a.ndim)
    return pl.pallas_call(
        functools.partial(_fnet_kernel, n_ctx_tiles=n_ctx_tiles, seq=seq),
        grid=(m // tm,),
        in_specs=([pl.BlockSpec((tm, B_WIDTH), lambda i: (i, cb))] + [full(a) for a in consts]
                  + [pl.BlockSpec(memory_space=pl.ANY)]),
        out_specs=pl.BlockSpec((tm, B_WIDTH), lambda i: (i, A_WIDTH // B_WIDTH)),
        out_shape=jax.ShapeDtypeStruct(mix.shape, mix.dtype),
        input_output_aliases={1 + len(consts): 0},
        scratch_shapes=[pltpu.VMEM((2 * tm, B_WIDTH), F32)],
        compiler_params=_cparams(("arbitrary",)),
        name="fnet",
    )(z, *consts, mix)


def _seg_sum(x, e_ref):
    w = e_ref.shape[0]
    parts = [_dot_exact_rhs(x[:, b * w:(b + 1) * w], e_ref[...]) for b in range(x.shape[1] // w)]
    return parts[0] if len(parts) == 1 else jnp.concatenate(parts, axis=1)


def _prep_kernel(zc_ref, zp_ref, zn_ref, mu_ref, w0_ref, w2_ref, a0_ref, a2_ref, kk_ref, ka_ref, rk_ref,
                 g2_ref, e_ref,
                 r_o, v_o, kk_o, ldf_o, ldb_o, kmf_o, kmb_o, kaf_o, kab_o, bv_o, g_o, *,
                 n_ctx_tiles, tiles_per_lat):
    i = pl.program_id(0)
    tm = zc_ref.shape[0]
    is_ctx = i < n_ctx_tiles
    q = (i - n_ctx_tiles) % tiles_per_lat
    x = zc_ref[...]
    row = lax.broadcasted_iota(jnp.int32, (tm, 1), 0)
    lane = lax.broadcasted_iota(jnp.int32, (1, C_PAD), 1)
    period = jnp.where(is_ctx, tm, GRID_W)
    pos = row & (period - 1)
    prev1 = jnp.where(pos == 0, 0.0, pltpu.roll(x, 1, 0))
    next1 = jnp.where(pos == period - 1, 0.0, pltpu.roll(x, tm - 1, 0))
    up_halo = jnp.where(q > 0, zp_ref[...], 0.0)
    dn_halo = jnp.where(q < tiles_per_lat - 1, zn_ref[...], 0.0)
    up = jnp.concatenate([up_halo, x[:tm - GRID_W]], axis=0)
    down = jnp.concatenate([x[GRID_W:], dn_halo], axis=0)
    half = C_IN // 2
    quarter = C_IN // 4
    zs_ctx = jnp.where(lane < half, prev1, next1)
    zs_lat = jnp.where(lane < quarter, prev1,
                       jnp.where(lane < 2 * quarter, next1, jnp.where(lane < 3 * quarter, up, down)))
    zs = jnp.where(is_ctx, zs_ctx, zs_lat)
    z = x + (zs - x) * mu_ref[...]

    cw = C_WIDTH
    r = z[:, 0:cw]
    k = z[:, cw:2 * cw]
    v = z[:, 2 * cw:3 * cw]
    wd = z[:, 3 * cw:3 * cw + 128]
    ad = z[:, 3 * cw + 128:3 * cw + 256]
    gd = z[:, 3 * cw + 256:C_PAD]

    uw = _dot3(jnp.tanh(wd), w2_ref[...])
    ua = _dot3(ad, a2_ref[...])
    kk_raw = k * kk_ref[...]
    ss = _seg_sum(kk_raw * kk_raw, e_ref)
    kk = kk_raw / jnp.maximum(jnp.sqrt(ss), 1e-12)
    r_o[...] = r
    v_o[...] = v
    kk_o[...] = kk
    km_sum = None
    for d, (ld_o, km_o, ka_o) in enumerate(((ldf_o, kmf_o, kaf_o), (ldb_o, kmb_o, kab_o))):
        u = uw[:, d * cw:(d + 1) * cw] + w0_ref[d:d + 1, :]
        ld_o[...] = -float(np.exp(-0.5)) * _sigmoid(u)
        a = _sigmoid(ua[:, d * cw:(d + 1) * cw] + a0_ref[d:d + 1, :])
        km = k * (1.0 + (a - 1.0) * ka_ref[...])
        km_o[...] = km
        ka_o[...] = kk * a
        km_sum = km if km_sum is None else km_sum + km
    bonus = _seg_sum(r * km_sum * rk_ref[...], e_ref)
    bv_o[...] = bonus * v
    g_o[...] = _dot(_sigmoid(gd).astype(BF16), g2_ref[...])


def _rwkv_prep(z, p, layer, n_ctx_tiles, tiles_per_lat, tm):
    m = z.shape[0]
    hb = tm // GRID_W
    n_halo = m // GRID_W
    lay = lambda *shape: pl.BlockSpec((None,) + shape, lambda i: (layer,) + (0,) * len(shape))
    out = jax.ShapeDtypeStruct((m, C_WIDTH), F32)
    return pl.pallas_call(
        functools.partial(_prep_kernel, n_ctx_tiles=n_ctx_tiles, tiles_per_lat=tiles_per_lat),
        grid=(m // tm,),
        in_specs=[
            pl.BlockSpec((tm, C_PAD), lambda i: (i, 0)),
            pl.BlockSpec((GRID_W, C_PAD), lambda i: (jnp.maximum(i * hb - 1, 0), 0)),
            pl.BlockSpec((GRID_W, C_PAD), lambda i: (jnp.minimum(i * hb + hb, n_halo - 1), 0)),
            lay(1, C_PAD),
            lay(2, C_WIDTH),
            lay(128, 2 * C_WIDTH),
            lay(2, C_WIDTH),
            lay(128, 2 * C_WIDTH),
            lay(1, C_WIDTH),
            lay(1, C_WIDTH),
            lay(1, C_WIDTH),
            lay(256, C_WIDTH),
            pl.BlockSpec((256, 256), lambda i: (0, 0)),
        ],
        out_specs=[pl.BlockSpec((tm, C_WIDTH), lambda i: (i, 0))] * 11,
        out_shape=[out] * 11,
        compiler_params=_cparams(("arbitrary",)),
        name="rwkv_prep",
    )(z, z, z, p["mu"], p["w0"], p["w2"], p["a0"], p["a2"], p["k_k"], p["k_a"], p["r_k"], p["g2"],
      p["e256"])


def _bmm(a, b):
    return jnp.einsum("uik,ukj->uij", a.astype(BF16), b.astype(BF16), preferred_element_type=F32)


def _bmm_nt(a, b):
    return jnp.einsum("uik,ujk->uij", a.astype(BF16), b.astype(BF16), preferred_element_type=F32)


def _block_diag(y, bd_mask):
    return jnp.where(bd_mask, jnp.concatenate([y] * UNIT_HEADS, axis=1), jnp.zeros((), y.dtype))


def _to_units(x):
    return jnp.stack([x[:, p * UNIT_W:(p + 1) * UNIT_W] for p in range(N_UNITS)], axis=0)


def _scan_chunk(reverse, r, v, kk, ld, km, ka, hs):
    t = SCAN_T
    w = UNIT_W
    ti = lax.broadcasted_iota(jnp.int32, (t, w), 0)
    si = lax.broadcasted_iota(jnp.int32, (t, w), 1) & (HEAD_DIM - 1)
    strict, incl = (si > ti, si >= ti) if reverse else (si < ti, si <= ti)
    eye = jnp.where(si == ti, 1.0, 0.0)
    t_row = lax.broadcasted_iota(jnp.int32, (t, t), 0)
    t_col = lax.broadcasted_iota(jnp.int32, (t, t), 1)
    tri = jnp.where((t_col >= t_row) if reverse else (t_col <= t_row), 1.0, 0.0).astype(BF16)
    bd_mask = (lax.broadcasted_iota(jnp.int32, (w, w), 0) // HEAD_DIM
               == lax.broadcasted_iota(jnp.int32, (w, w), 1) // HEAD_DIM)

    cum = _dot_exact_lhs(tri, ld)
    c_end = cum[0:1, :] if reverse else cum[t - 1:t, :]
    p_end = jnp.exp(c_end - cum)
    p_inv = jnp.exp(-cum)
    q = _to_units(jnp.concatenate([kk * jnp.exp(cum - ld), r * jnp.exp(cum)], axis=0).astype(BF16))
    ai = _to_units((ka * p_inv).astype(BF16))
    ki = _to_units((km * p_inv).astype(BF16))
    vb = _to_units(v.astype(BF16))
    rr = jnp.concatenate([_block_diag(ai, bd_mask), _block_diag(ki, bd_mask)], axis=1)
    sc = _bmm_nt(q, rr)
    la = jnp.where(strict, sc[:, 0:t, 0:w], 0.0)
    lk = jnp.where(strict, sc[:, 0:t, w:2 * w], 0.0)
    ma = jnp.where(incl, sc[:, t:2 * t, 0:w], 0.0)
    mk = jnp.where(incl, sc[:, t:2 * t, w:2 * w], 0.0)

    n_pow = -la
    x_inv = eye + n_pow
    n_pow = _bmm(n_pow, _block_diag(n_pow.astype(BF16), bd_mask))
    for _ in range(int(np.log2(t)) - 2):
        both = _bmm(jnp.concatenate([n_pow, x_inv], axis=1), _block_diag(n_pow.astype(BF16), bd_mask))
        x_inv = x_inv + both[:, t:2 * t]
        n_pow = both[:, 0:t]
    x_inv = x_inv + _bmm(x_inv, _block_diag(n_pow.astype(BF16), bd_mask))

    hb = _bmm_nt(q, hs)
    lv = _bmm(jnp.concatenate([lk, mk], axis=1), _block_diag(vb, bd_mask))
    u = -_bmm(x_inv, _block_diag((hb[:, 0:t] + lv[:, 0:t]).astype(BF16), bd_mask))
    y = hb[:, t:2 * t] + lv[:, t:2 * t] + _bmm(ma, _block_diag(u.astype(BF16), bd_mask))
    lhs = _to_units(jnp.concatenate([ka * p_end, km * p_end], axis=0).astype(BF16))
    rhs_t = jnp.swapaxes(jnp.concatenate([u, _to_units(v)], axis=1), 1, 2)
    upd = _bmm(rhs_t, lhs)
    hs_new = _to_units(jnp.exp(c_end))[:, 0:1, :] * hs + jnp.where(bd_mask, upd, 0.0)
    return y, hs_new


def _scan_kernel(*refs, has_s0, emit_state, aliased_state):
    ins = refs[:12]
    pos = 12
    s0_ref = None
    if has_s0:
        s0_ref = refs[pos]
        pos += 1
    if aliased_state:
        pos += 1
    y_refs = refs[pos:pos + 2]
    pos += 2
    st_ref = None
    if emit_state:
        st_ref = refs[pos]
        pos += 1
    h_ref = refs[pos]
    c = pl.program_id(1)
    hd = HEAD_DIM

    n_group = h_ref.shape[0]

    @pl.when(c == 0)
    def _():
        if has_s0:
            zero = jnp.zeros((hd, hd), F32)
            for s in range(n_group):
                for d in range(2):
                    for p in range(N_UNITS):
                        rows = [jnp.concatenate([s0_ref[s, d, UNIT_HEADS * p + a] if a == b else zero
                                                 for b in range(UNIT_HEADS)], axis=1) for a in range(UNIT_HEADS)]
                        h_ref[s, d, p] = jnp.concatenate(rows, axis=0)
        else:
            h_ref[...] = jnp.zeros_like(h_ref)

    for s in range(n_group):
        for d, y_ref in enumerate(y_refs):
            r, v, kk, ld, km, ka = (ref[s] for ref in ins[6 * d:6 * d + 6])
            y, hs_new = _scan_chunk(d == 1, r, v, kk, ld, km, ka, h_ref[s, d])
            for p in range(N_UNITS):
                y_ref[s, :, p * UNIT_W:(p + 1) * UNIT_W] = y[p]
            h_ref[s, d] = hs_new

    if emit_state:
        @pl.when(c == pl.num_programs(1) - 1)
        def _():
            for s in range(n_group):
                for d in range(2):
                    for p in range(N_UNITS):
                        hs = h_ref[s, d, p]
                        for a in range(UNIT_HEADS):
                            st_ref[s, d, UNIT_HEADS * p + a] = hs[a * hd:(a + 1) * hd, a * hd:(a + 1) * hd]


def _rwkv_scan(arrs, n_seq, n_chunk, row0, layer, group, s0=None, state_shape=None, state_prev=None):
    seq_len = n_chunk * SCAN_T
    m_all = arrs[0].shape[0]
    assert row0 % (seq_len * group) == 0 and m_all % seq_len == 0 and n_seq % group == 0
    r, v, kk, ldf, kmf, kaf, ldb, kmb, kab = (a.reshape(m_all // seq_len, seq_len, C_WIDTH) for a in arrs)
    g0 = row0 // (seq_len * group)
    fwd = lambda b, c: (b, c, 0)
    bwd = lambda b, c: (b, n_chunk - 1 - c, 0)
    fwd_in = lambda b, c: (g0 + b, c, 0)
    bwd_in = lambda b, c: (g0 + b, n_chunk - 1 - c, 0)
    blk = lambda imap: pl.BlockSpec((group, SCAN_T, C_WIDTH), imap)
    in_specs = [blk(fwd_in)] * 6 + [blk(bwd_in)] * 6
    args = [r, v, kk, ldf, kmf, kaf, r, v, kk, ldb, kmb, kab]
    st_spec = pl.BlockSpec((group, None, 2, N_HEADS, HEAD_DIM, HEAD_DIM), lambda b, c: (b, layer, 0, 0, 0, 0))
    if s0 is not None:
        in_specs.append(st_spec)
        args.append(s0)
    aliases = {}
    if state_prev is not None:
        aliases = {len(args): 2}
        in_specs.append(pl.BlockSpec(memory_space=pl.ANY))
        args.append(state_prev)
    y_shape = jax.ShapeDtypeStruct((n_seq, seq_len, C_WIDTH), F32)
    out_specs = [blk(fwd), blk(bwd)]
    out_shape = [y_shape, y_shape]
    if state_shape is not None:
        out_specs.append(st_spec)
        out_shape.append(jax.ShapeDtypeStruct(state_shape, F32))
    outs = pl.pallas_call(
        functools.partial(_scan_kernel, has_s0=s0 is not None, emit_state=state_shape is not None,
                          aliased_state=state_prev is not None),
        grid=(n_seq // group, n_chunk),
        in_specs=in_specs,
        out_specs=out_specs,
        out_shape=out_shape,
        input_output_aliases=aliases,
        scratch_shapes=[pltpu.VMEM((group, 2, N_UNITS, UNIT_W, UNIT_W), F32)],
        compiler_params=_cparams(("arbitrary", "arbitrary")),
        name="rwkv_scan",
    )(*args)
    ys = [y.reshape(n_seq * seq_len, C_WIDTH) for y in outs[:2]]
    return ys + list(outs[2:])


def _post_kernel(yf1_ref, yb1_ref, yf2_ref, yb2_ref, bv_ref, g_ref, lg_ref, lb_ref, e_ref, mix_ref, o_ref, *,
                 n_ctx_tiles):
    del mix_ref
    is_ctx = pl.program_id(0) < n_ctx_tiles
    y = jnp.where(is_ctx, yf1_ref[...] + yb1_ref[...], yf2_ref[...] + yb2_ref[...])
    inv = 1.0 / HEAD_DIM
    m = _seg_sum(y, e_ref) * inv
    yc = y - m
    var = _seg_sum(yc * yc, e_ref) * inv
    yn = yc * lax.rsqrt(var + LNX_EPS) * lg_ref[...] + lb_ref[...]
    o_ref[...] = ((yn + bv_ref[...]) * g_ref[...]).astype(BF16)


def _rwkv_post(y_ctx, y_lat, bv, g, lnx_g, lnx_b, e256, mix, layer, n_ctx_tiles, tm):
    m = bv.shape[0]
    n_tiles = m // tm
    row = pl.BlockSpec((tm, C_WIDTH), lambda i: (i, 0))
    row_ctx = pl.BlockSpec((tm, C_WIDTH), lambda i: (jnp.minimum(i, n_ctx_tiles - 1), 0))
    row_lat = pl.BlockSpec((tm, C_WIDTH), lambda i: (jnp.maximum(i - n_ctx_tiles, 0), 0))
    vec = pl.BlockSpec((None, 1, C_WIDTH), lambda i: (layer, 0, 0))
    return pl.pallas_call(
        functools.partial(_post_kernel, n_ctx_tiles=n_ctx_tiles),
        grid=(n_tiles,),
        in_specs=[row_ctx, row_ctx, row_lat, row_lat, row, row, vec, vec,
                  pl.BlockSpec((256, 256), lambda i: (0, 0)), pl.BlockSpec(memory_space=pl.ANY)],
        out_specs=pl.BlockSpec((tm, C_WIDTH), lambda i: (i, (A_WIDTH + B_WIDTH) // C_WIDTH)),
        out_shape=jax.ShapeDtypeStruct(mix.shape, mix.dtype),
        input_output_aliases={9: 0},
        compiler_params=_cparams(("arbitrary",)),
        name="rwkv_post",
    )(y_ctx[0], y_ctx[1], y_lat[0], y_lat[1], bv, g, lnx_g, lnx_b, e256, mix)


def _final_norm_kernel(x_ref, g_ref, op_ref, os_ref, *, n_ctx_tiles):
    x = x_ref[...]
    ms = jnp.mean(x * x, axis=-1, keepdims=True)
    y = x * lax.rsqrt(ms + RMS_EPS) * g_ref[...]
    i = pl.program_id(0)

    @pl.when(i < n_ctx_tiles)
    def _():
        op_ref[...] = y

    @pl.when(i >= n_ctx_tiles)
    def _():
        os_ref[...] = y


def _final_norm(x, g, m_ctx, tm):
    m = x.shape[0]
    n_ctx_tiles = m_ctx // tm
    return pl.pallas_call(
        functools.partial(_final_norm_kernel, n_ctx_tiles=n_ctx_tiles),
        grid=(m // tm,),
        in_specs=[pl.BlockSpec((tm, D_MODEL), lambda i: (i, 0)), pl.BlockSpec((1, D_MODEL), lambda i: (0, 0))],
        out_specs=[pl.BlockSpec((tm, D_MODEL), lambda i: (jnp.minimum(i, n_ctx_tiles - 1), 0)),
                   pl.BlockSpec((tm, D_MODEL), lambda i: (jnp.maximum(i - n_ctx_tiles, 0), 0))],
        out_shape=[jax.ShapeDtypeStruct((m_ctx, D_MODEL), F32), jax.ShapeDtypeStruct((m - m_ctx, D_MODEL), F32)],
        compiler_params=_cparams(("arbitrary",)),
        name="final_norm",
    )(x, g)


def kernel(x_prompt, x_sample, state_wkv, c, c_ctx, norm_g, w_mod, b_mod, ffn_w_in, ffn_w_out, w_in, w_out,
           sgu_ln_g, sgu_ln_b, sgu_w, sgu_b, shift_mu, decay_w0, decay_w2, iclr_a0, iclr_a2, k_k, k_a, r_k,
           gate_w2, lnx_g, lnx_b, final_g):
    batch, seq, d = x_prompt.shape
    dec_batch, dec_seq, _ = x_sample.shape
    depth = w_mod.shape[0]
    assert d == D_MODEL and dec_batch + 1 <= MOD_ROWS
    m_ctx = batch * seq
    tm = 1024
    ts = 512
    tp = 256
    assert seq == tp and dec_seq == tm and m_ctx % tm == 0 and dec_seq % GRID_W == 0

    def row_of_tile_fn(rows):
        n_ctx = m_ctx // rows
        per_lat = dec_seq // rows
        return lambda i: jnp.where(i < n_ctx, 0, 1 + (i - n_ctx) // per_lat)

    zeros_cw = jnp.zeros((depth, DECAY_RANK, C_WIDTH), F32)

    def both_dirs(w):
        top = jnp.concatenate([w[:, 0], zeros_cw], axis=-1)
        bot = jnp.concatenate([zeros_cw, w[:, 1]], axis=-1)
        return jnp.concatenate([top, bot], axis=1)

    e_np = (np.arange(256)[:, None] // HEAD_DIM == np.arange(256)[None, :] // HEAD_DIM)
    prep_params = {
        "mu": jnp.pad(shift_mu, ((0, 0), (0, C_PAD - C_IN))).reshape(depth, 1, C_PAD),
        "w0": decay_w0, "w2": both_dirs(decay_w2), "a0": iclr_a0, "a2": both_dirs(iclr_a2),
        "k_k": k_k.reshape(depth, 1, C_WIDTH), "k_a": k_a.reshape(depth, 1, C_WIDTH),
        "r_k": r_k.reshape(depth, 1, C_WIDTH),
        "g2": jnp.pad(gate_w2, ((0, 0), (0, 256 - GATE_RANK), (0, 0))).astype(BF16),
        "e256": jnp.asarray(e_np, BF16),
    }
    cd, sd = _dft_mats(B_GROUP_DIM)
    cs, ss = _dft_mats(seq)
    cl, sl = _dft_mats(dec_seq)
    fnet_consts = (_hi_lo(np.concatenate([cd, sd], axis=1)) + _hi_lo(np.concatenate([cs, -ss], axis=1))
                   + _hi_lo(np.concatenate([cl, -sl], axis=1)))
    sgu_bias = jnp.repeat(jnp.swapaxes(sgu_b, 1, 2), A_WIDTH // A_HEADS, axis=2)
    sgu_w_b = sgu_w.astype(BF16)
    ln_g = sgu_ln_g.reshape(depth, 1, A_WIDTH)
    ln_b = sgu_ln_b.reshape(depth, 1, A_WIDTH)
    lnx_g3 = lnx_g.reshape(depth, 1, C_WIDTH)
    lnx_b3 = lnx_b.reshape(depth, 1, C_WIDTH)
    norm_g4 = norm_g.reshape(depth, 3, 1, d)

    cond = jnp.concatenate([c_ctx[None, :], c, jnp.zeros((MOD_ROWS - 1 - dec_batch, d), F32)], axis=0)
    mod = _modulation(cond, w_mod, b_mod).reshape(depth, MOD_ROWS, N_MOD, 1, d)

    x = jnp.concatenate([x_prompt.reshape(m_ctx, d), x_sample.reshape(dec_batch * dec_seq, d)], axis=0)
    rot = row_of_tile_fn(tm)
    n_chunk_ctx = seq // SCAN_T
    n_chunk_lat = dec_seq // SCAN_T
    state_shape = (batch, depth, 2, N_HEADS, HEAD_DIM, HEAD_DIM)
    tk = 512
    scan_group = lambda n_seq: 2 if n_seq % 2 == 0 else 1
    w_in_t = jnp.swapaxes(w_in, 1, 2)
    new_state = None
    for l in range(depth):
        x = _ffn(x, mod, norm_g4, ffn_w_in, ffn_w_out, l, 0, 0, rot, tm)
        z = _in_proj(x, mod, norm_g4, w_in_t, l, rot, tm)
        mix = _sgu(z, ln_g, ln_b, sgu_w_b, sgu_bias, l, ts)
        mix = _fnet(z, fnet_consts, mix, m_ctx // tm, seq, tm)
        pr = _rwkv_prep(z, prep_params, l, m_ctx // tp, dec_seq // tp, tp)
        r, v, kk, ldf, ldb, kmf, kmb, kaf, kab, bv, g = pr
        scan_in = (r, v, kk, ldf, kmf, kaf, ldb, kmb, kab)
        yf_c, yb_c, new_state = _rwkv_scan(scan_in, batch, n_chunk_ctx, 0, l, scan_group(batch),
                                           state_shape=state_shape, state_prev=new_state)
        yf_l, yb_l = _rwkv_scan(scan_in, dec_batch, n_chunk_lat, m_ctx, l, scan_group(dec_batch), s0=state_wkv)
        mix = _rwkv_post((yf_c, yb_c), (yf_l, yb_l), bv, g, lnx_g3, lnx_b3, prep_params["e256"], mix, l,
                         m_ctx // ts, ts)
        x = _resid_matmul(x, mod, mix, w_out, pl.BlockSpec((None, tk, D_MODEL), lambda i, k: (l, k, 0)),
                          l, 5, 1.0, rot, tm, tk, "mix_out")
        x = _ffn(x, mod, norm_g4, ffn_w_in, ffn_w_out, l, 2, 1, rot, tm)
    y_ctx, y_lat = _final_norm(x, final_g.reshape(1, d), m_ctx, ts)
    return (y_ctx.reshape(batch, seq, d), y_lat.reshape(dec_batch, dec_seq, d), new_state)
```

```python
import functools

import numpy as np
import jax
import jax.numpy as jnp
from jax import lax
from jax.experimental import pallas as pl
from jax.experimental.pallas import tpu as pltpu

F32 = jnp.float32
BF16 = jnp.bfloat16

D_MODEL = 2048
GRID_W = 64
SGU_CHUNK = 128
A_HEADS = 4
A_WIDTH = 512
B_WIDTH = 512
B_GROUP_DIM = 128
C_WIDTH = 1024
MIX_WIDTH = A_WIDTH + B_WIDTH + C_WIDTH
HEAD_DIM = 64
N_HEADS = 16
UNIT_HEADS = 2
UNIT_W = UNIT_HEADS * HEAD_DIM
N_UNITS = N_HEADS // UNIT_HEADS
DECAY_RANK = 64
GATE_RANK = 160
C_IN = 3488
C_PAD = 3584
IN_COLS = 2 * A_WIDTH + B_WIDTH + C_IN
Z_BLOCK = 512
Z_COLS = C_PAD + 2 * A_WIDTH + B_WIDTH
D_FF = 5632
N_MOD = 9
RMS_EPS = 1e-6
LN_EPS = 1e-5
LNX_EPS = 64e-5
SCAN_T = 64
MOD_ROWS = 8

VMEM_LIMIT = 56 * 1024 * 1024
FFN_VMEM_LIMIT = 60 * 1024 * 1024


def _cparams(sem):
    return pltpu.CompilerParams(dimension_semantics=sem, vmem_limit_bytes=VMEM_LIMIT)


def _dot(a, b):
    return jnp.dot(a, b, preferred_element_type=F32)


def _split3(x):
    hi = x.astype(BF16)
    r1 = x - hi.astype(F32)
    mid = r1.astype(BF16)
    lo = (r1 - mid.astype(F32)).astype(BF16)
    return hi, mid, lo


def _dot_ones_rhs(x, e):
    hi = x.astype(BF16)
    lo = (x - hi.astype(F32)).astype(BF16)
    return _dot(hi, e) + _dot(lo, e)


def _dot_exact_lhs(e, x):
    hi, mid, lo = _split3(x)
    return _dot(e, hi) + _dot(e, mid) + _dot(e, lo)


def _dot3(a, b):
    ah = a.astype(BF16)
    al = (a - ah.astype(F32)).astype(BF16)
    bh = b.astype(BF16)
    bl = (b - bh.astype(F32)).astype(BF16)
    return _dot(ah, bh) + _dot(al, bh) + _dot(ah, bl)


def _sigmoid(x):
    return 1.0 / (1.0 + jnp.exp(-x))


def _silu(x):
    return x * _sigmoid(x)


def _gelu_tanh(x):
    return 0.5 * x * (1.0 + jnp.tanh(0.7978845608028654 * (x + 0.044715 * (x * x * x))))


def _mod_kernel(c_ref, w_ref, b_ref, o_ref):
    s = _silu(c_ref[...]).astype(BF16)
    o_ref[...] = _dot(s, w_ref[...].astype(BF16)) + b_ref[...]


def _modulation(cond, w_mod, b_mod):
    depth = w_mod.shape[0]
    n = w_mod.shape[2]
    tn = 1024
    return pl.pallas_call(
        _mod_kernel,
        grid=(depth, n // tn),
        in_specs=[
            pl.BlockSpec((MOD_ROWS, D_MODEL), lambda l, j: (0, 0)),
            pl.BlockSpec((None, D_MODEL, tn), lambda l, j: (l, 0, j)),
            pl.BlockSpec((None, 1, tn), lambda l, j: (l, 0, j)),
        ],
        out_specs=pl.BlockSpec((None, MOD_ROWS, tn), lambda l, j: (l, 0, j)),
        out_shape=jax.ShapeDtypeStruct((depth, MOD_ROWS, n), F32),
        compiler_params=_cparams(("arbitrary", "arbitrary")),
        name="modulation",
    )(cond, w_mod, b_mod.reshape(depth, 1, n))


def _mod_spec(layer, slot, row_of_tile):
    return pl.BlockSpec((None, None, None, 1, D_MODEL),
                        lambda i, *_: (layer, row_of_tile(i), slot, 0, 0))


def _norm_spec(layer, slot):
    return pl.BlockSpec((None, None, 1, D_MODEL), lambda i, *_: (layer, slot, 0, 0))


def _modulated_norm(x_ref, g_ref, sc_ref, sh_ref):
    x = x_ref[...]
    rs = lax.rsqrt(jnp.mean(x * x, axis=-1, keepdims=True) + RMS_EPS)
    gain = g_ref[...] * (1.0 + sc_ref[...])
    return ((x * rs) * gain + sh_ref[...]).astype(BF16)


def _in_proj_kernel(x_ref, sh_ref, sc_ref, g_ref, wt_ref, o_ref, h_ref, *, valid_cols):
    j = pl.program_id(1)

    @pl.when(j == 0)
    def _():
        h_ref[...] = _modulated_norm(x_ref, g_ref, sc_ref, sh_ref)

    out = lax.dot_general(h_ref[...], wt_ref[...].astype(BF16), (((1,), (1,)), ((), ())),
                          preferred_element_type=F32)
    tn = out.shape[1]
    if valid_cols % tn:
        col = j * tn + lax.broadcasted_iota(jnp.int32, (1, tn), 1)
        out = jnp.where(col < valid_cols, out, 0.0)
    o_ref[...] = out


def _in_proj(x, mod, norm_g, w_in_t, layer, row_of_tile, tm):
    m = x.shape[0]
    nb = Z_COLS // Z_BLOCK
    shift = C_PAD // Z_BLOCK
    assert (2 * A_WIDTH + B_WIDTH) % Z_BLOCK == 0 and -(-IN_COLS // Z_BLOCK) == nb
    return pl.pallas_call(
        functools.partial(_in_proj_kernel, valid_cols=IN_COLS),
        grid=(m // tm, nb),
        in_specs=[
            pl.BlockSpec((tm, D_MODEL), lambda i, j: (i, 0)),
            _mod_spec(layer, 3, row_of_tile),
            _mod_spec(layer, 4, row_of_tile),
            _norm_spec(layer, 1),
            pl.BlockSpec((None, Z_BLOCK, D_MODEL), lambda i, j: (layer, j, 0)),
        ],
        out_specs=pl.BlockSpec((tm, Z_BLOCK), lambda i, j: (i, (j + shift) % nb)),
        out_shape=jax.ShapeDtypeStruct((m, Z_COLS), F32),
        scratch_shapes=[pltpu.VMEM((tm, D_MODEL), BF16)],
        compiler_params=_cparams(("arbitrary", "arbitrary")),
        name="in_proj",
    )(x, mod, mod, norm_g, w_in_t)


def _ffn_kernel(x_ref, sh_ref, sc_ref, gt_ref, g_ref, wg_ref, wu_ref, wo_ref, o_ref, h_ref, *, tn):
    j = pl.program_id(1)

    @pl.when(j == 0)
    def _():
        h_ref[...] = _modulated_norm(x_ref, g_ref, sc_ref, sh_ref)
        o_ref[...] = jnp.zeros_like(o_ref)

    h = h_ref[...]
    gate = _dot(h, wg_ref[...].astype(BF16))
    up = _dot(h, wu_ref[...].astype(BF16))
    a = (_silu(gate) * up).astype(BF16)
    for n in range(o_ref.shape[1] // tn):
        cols = slice(n * tn, (n + 1) * tn)
        o_ref[:, cols] += _dot(a, wo_ref[:, cols].astype(BF16))

    @pl.when(j == pl.num_programs(1) - 1)
    def _():
        o_ref[...] = x_ref[...] + (0.5 * gt_ref[...]) * o_ref[...]


def _ffn(x, mod, norm_g, w_in, w_out, layer, sub, ffn_idx, row_of_tile, tm, tf=256):
    m = x.shape[0]
    nf = D_FF // tf
    return pl.pallas_call(
        functools.partial(_ffn_kernel, tn=512),
        grid=(m // tm, nf),
        in_specs=[
            pl.BlockSpec((tm, D_MODEL), lambda i, j: (i, 0)),
            _mod_spec(layer, 3 * sub, row_of_tile),
            _mod_spec(layer, 3 * sub + 1, row_of_tile),
            _mod_spec(layer, 3 * sub + 2, row_of_tile),
            _norm_spec(layer, sub),
            pl.BlockSpec((None, None, D_MODEL, tf), lambda i, j: (layer, ffn_idx, 0, j)),
            pl.BlockSpec((None, None, D_MODEL, tf), lambda i, j: (layer, ffn_idx, 0, nf + j)),
            pl.BlockSpec((None, None, tf, D_MODEL), lambda i, j: (layer, ffn_idx, j, 0)),
        ],
        out_specs=pl.BlockSpec((tm, D_MODEL), lambda i, j: (i, 0)),
        out_shape=jax.ShapeDtypeStruct((m, D_MODEL), F32),
        scratch_shapes=[pltpu.VMEM((tm, D_MODEL), BF16)],
        compiler_params=pltpu.CompilerParams(dimension_semantics=("arbitrary", "arbitrary"),
                                             vmem_limit_bytes=FFN_VMEM_LIMIT),
        name="ffn",
    )(x, mod, mod, mod, norm_g, w_in, w_in, w_out)


def _resid_matmul_kernel(x_ref, gt_ref, a_ref, w_ref, o_ref, *, coef, tn):
    k = pl.program_id(1)

    @pl.when(k == 0)
    def _():
        o_ref[...] = jnp.zeros_like(o_ref)

    a = a_ref[...]
    for n in range(o_ref.shape[1] // tn):
        cols = slice(n * tn, (n + 1) * tn)
        o_ref[:, cols] += _dot(a, w_ref[:, cols].astype(BF16))

    @pl.when(k == pl.num_programs(1) - 1)
    def _():
        o_ref[...] = x_ref[...] + (coef * gt_ref[...]) * o_ref[...]


def _resid_matmul(x, mod, a, w, w_spec, layer, gate_slot, coef, row_of_tile, tm, tk, name):
    m = x.shape[0]
    return pl.pallas_call(
        functools.partial(_resid_matmul_kernel, coef=coef, tn=512),
        grid=(m // tm, a.shape[1] // tk),
        in_specs=[
            pl.BlockSpec((tm, D_MODEL), lambda i, k: (i, 0)),
            _mod_spec(layer, gate_slot, row_of_tile),
            pl.BlockSpec((tm, tk), lambda i, k: (i, k)),
            w_spec,
        ],
        out_specs=pl.BlockSpec((tm, D_MODEL), lambda i, k: (i, 0)),
        out_shape=jax.ShapeDtypeStruct((m, D_MODEL), F32),
        compiler_params=_cparams(("arbitrary", "arbitrary")),
        name=name,
    )(x, mod, a, w)


def _sgu_kernel(zu_ref, zv_ref, lg_ref, lb_ref, w_ref, bias_ref, o_ref):
    u = _gelu_tanh(zu_ref[...])
    v = _gelu_tanh(zv_ref[...])
    mu = jnp.mean(v, axis=-1, keepdims=True)
    vc = v - mu
    var = jnp.mean(vc * vc, axis=-1, keepdims=True)
    vn = (vc * lax.rsqrt(var + LN_EPS) * lg_ref[...] + lb_ref[...]).astype(BF16)
    hd = A_WIDTH // A_HEADS
    for c in range(u.shape[0] // SGU_CHUNK):
        rows = slice(c * SGU_CHUNK, (c + 1) * SGU_CHUNK)
        for h in range(A_HEADS):
            cols = slice(h * hd, (h + 1) * hd)
            mixed = _dot(w_ref[h], vn[rows, cols]) + bias_ref[:, cols]
            o_ref[rows, cols] = (u[rows, cols] * mixed).astype(BF16)


def _sgu(z, ln_g, ln_b, w_s, bias, layer, tm):
    m = z.shape[0]
    cu = C_PAD // A_WIDTH
    return pl.pallas_call(
        _sgu_kernel,
        grid=(m // tm,),
        in_specs=[
            pl.BlockSpec((tm, A_WIDTH), lambda i: (i, cu)),
            pl.BlockSpec((tm, A_WIDTH), lambda i: (i, cu + 1)),
            pl.BlockSpec((None, 1, A_WIDTH), lambda i: (layer, 0, 0)),
            pl.BlockSpec((None, 1, A_WIDTH), lambda i: (layer, 0, 0)),
            pl.BlockSpec((None, A_HEADS, SGU_CHUNK, SGU_CHUNK), lambda i: (layer, 0, 0, 0)),
            pl.BlockSpec((None, SGU_CHUNK, A_WIDTH), lambda i: (layer, 0, 0)),
        ],
        out_specs=pl.BlockSpec((tm, A_WIDTH), lambda i: (i, 0)),
        out_shape=jax.ShapeDtypeStruct((m, MIX_WIDTH), BF16),
        compiler_params=_cparams(("arbitrary",)),
        name="sgu",
    )(z, z, ln_g, ln_b, w_s, bias)


def _dft_mats(n):
    idx = np.arange(n)
    ang = 2.0 * np.pi * ((idx[:, None] * idx[None, :]) % n) / n
    return np.cos(ang) / np.sqrt(n), np.sin(ang) / np.sqrt(n)


def _hi_lo(a):
    a32 = jnp.asarray(a, F32)
    hi = a32.astype(BF16)
    lo = (a32 - hi.astype(F32)).astype(BF16)
    return hi, lo


def _fnet_kernel(z_ref, fdh_ref, fdl_ref, fsh_ref, fsl_ref, flh_ref, fll_ref, mix_ref, o_ref, t_ref, *,
                 n_ctx_tiles, seq):
    del mix_ref
    tm = z_ref.shape[0]
    x = z_ref[...]
    xh = x.astype(BF16)
    xl = (x - xh.astype(F32)).astype(BF16)
    gd = B_GROUP_DIM
    for g in range(B_WIDTH // gd):
        cols = slice(g * gd, (g + 1) * gd)
        t = _dot(xh[:, cols], fdh_ref[...]) + _dot(xl[:, cols], fdh_ref[...]) + _dot(xh[:, cols], fdl_ref[...])
        t_ref[0:tm, cols] = t[:, 0:gd]
        t_ref[tm:2 * tm, cols] = t[:, gd:2 * gd]

    def position_dft(fh_ref, fl_ref, rows_in, rows_out):
        tc = t_ref[rows_in[0], :]
        ts = t_ref[rows_in[1], :]
        tt = jnp.concatenate([tc, ts], axis=0)
        th = tt.astype(BF16)
        tl = (tt - th.astype(F32)).astype(BF16)
        out = _dot(fh_ref[...], th) + _dot(fl_ref[...], th) + _dot(fh_ref[...], tl)
        o_ref[rows_out, :] = out.astype(BF16)

    is_ctx = pl.program_id(0) < n_ctx_tiles

    @pl.when(is_ctx)
    def _():
        for s in range(tm // seq):
            r0 = slice(s * seq, (s + 1) * seq)
            r1 = slice(tm + s * seq, tm + (s + 1) * seq)
            position_dft(fsh_ref, fsl_ref, (r0, r1), r0)

    @pl.when(jnp.logical_not(is_ctx))
    def _():
        position_dft(flh_ref, fll_ref, (slice(0, tm), slice(tm, 2 * tm)), slice(0, tm))


def _fnet(z, consts, mix, n_ctx_tiles, seq, tm):
    m = z.shape[0]
    cb = (C_PAD + 2 * A_WIDTH) // B_WIDTH
    full = lambda a: pl.BlockSpec(a.shape, lambda i: (0,) * a.ndim)
    return pl.pallas_call(
        functools.partial(_fnet_kernel, n_ctx_tiles=n_ctx_tiles, seq=seq),
        grid=(m // tm,),
        in_specs=([pl.BlockSpec((tm, B_WIDTH), lambda i: (i, cb))] + [full(a) for a in consts]
                  + [pl.BlockSpec(memory_space=pl.ANY)]),
        out_specs=pl.BlockSpec((tm, B_WIDTH), lambda i: (i, A_WIDTH // B_WIDTH)),
        out_shape=jax.ShapeDtypeStruct(mix.shape, mix.dtype),
        input_output_aliases={1 + len(consts): 0},
        scratch_shapes=[pltpu.VMEM((2 * tm, B_WIDTH), F32)],
        compiler_params=_cparams(("arbitrary",)),
        name="fnet",
    )(z, *consts, mix)


def _seg_sum(x, e_ref):
    w = e_ref.shape[0]
    parts = [_dot_ones_rhs(x[:, b * w:(b + 1) * w], e_ref[...]) for b in range(x.shape[1] // w)]
    return parts[0] if len(parts) == 1 else jnp.concatenate(parts, axis=1)


def _prep_kernel(zc_ref, zp_ref, zn_ref, mu_ref, w0_ref, w2_ref, a0_ref, a2_ref, kk_ref, ka_ref, rk_ref,
                 g2_ref, e_ref,
                 rvk_o, kf_o, kb_o, ld_o, bv_o, g_o, *,
                 n_ctx_tiles, tiles_per_lat):
    i = pl.program_id(0)
    tm = zc_ref.shape[0]
    is_ctx = i < n_ctx_tiles
    q = (i - n_ctx_tiles) % tiles_per_lat
    x = zc_ref[...]
    row = lax.broadcasted_iota(jnp.int32, (tm, 1), 0)
    lane = lax.broadcasted_iota(jnp.int32, (1, C_PAD), 1)
    period = jnp.where(is_ctx, tm, GRID_W)
    pos = row & (period - 1)
    prev1 = jnp.where(pos == 0, 0.0, pltpu.roll(x, 1, 0))
    next1 = jnp.where(pos == period - 1, 0.0, pltpu.roll(x, tm - 1, 0))
    up_halo = jnp.where(q > 0, zp_ref[...], 0.0)
    dn_halo = jnp.where(q < tiles_per_lat - 1, zn_ref[...], 0.0)
    up = jnp.concatenate([up_halo, x[:tm - GRID_W]], axis=0)
    down = jnp.concatenate([x[GRID_W:], dn_halo], axis=0)
    half = C_IN // 2
    quarter = C_IN // 4
    zs_ctx = jnp.where(lane < half, prev1, next1)
    zs_lat = jnp.where(lane < quarter, prev1,
                       jnp.where(lane < 2 * quarter, next1, jnp.where(lane < 3 * quarter, up, down)))
    zs = jnp.where(is_ctx, zs_ctx, zs_lat)
    z = x + (zs - x) * mu_ref[...]

    cw = C_WIDTH
    r = z[:, 0:cw]
    k = z[:, cw:2 * cw]
    v = z[:, 2 * cw:3 * cw]
    wd = z[:, 3 * cw:3 * cw + 128]
    ad = z[:, 3 * cw + 128:3 * cw + 256]
    gd = z[:, 3 * cw + 256:C_PAD]

    uw = _dot3(jnp.tanh(wd), w2_ref[...])
    ua = _dot3(ad, a2_ref[...])
    kk_raw = k * kk_ref[...]
    ss = _seg_sum(kk_raw * kk_raw, e_ref)
    kk = kk_raw / jnp.maximum(jnp.sqrt(ss), 1e-12)
    rvk_o[:, 0:cw] = r.astype(BF16)
    rvk_o[:, cw:2 * cw] = v.astype(BF16)
    rvk_o[:, 2 * cw:3 * cw] = kk.astype(BF16)
    km_sum = None
    for d, kd_o in enumerate((kf_o, kb_o)):
        u = uw[:, d * cw:(d + 1) * cw] + w0_ref[d:d + 1, :]
        ld_o[:, d * cw:(d + 1) * cw] = -float(np.exp(-0.5)) * _sigmoid(u)
        a = _sigmoid(ua[:, d * cw:(d + 1) * cw] + a0_ref[d:d + 1, :])
        km = k * (1.0 + (a - 1.0) * ka_ref[...])
        kd_o[:, 0:cw] = km.astype(BF16)
        kd_o[:, cw:2 * cw] = (kk * a).astype(BF16)
        km_sum = km if km_sum is None else km_sum + km
    bonus = _seg_sum(r * km_sum * rk_ref[...], e_ref)
    bv_o[...] = bonus * v
    g_o[...] = _dot(_sigmoid(gd).astype(BF16), g2_ref[...]).astype(BF16)


def _rwkv_prep(z, p, layer, n_ctx_tiles, tiles_per_lat, tm):
    m = z.shape[0]
    hb = tm // GRID_W
    n_halo = m // GRID_W
    lay = lambda *shape: pl.BlockSpec((None,) + shape, lambda i: (layer,) + (0,) * len(shape))
    outs = [(3, BF16), (2, BF16), (2, BF16), (2, F32), (1, F32), (1, BF16)]
    return pl.pallas_call(
        functools.partial(_prep_kernel, n_ctx_tiles=n_ctx_tiles, tiles_per_lat=tiles_per_lat),
        grid=(m // tm,),
        in_specs=[
            pl.BlockSpec((tm, C_PAD), lambda i: (i, 0)),
            pl.BlockSpec((GRID_W, C_PAD), lambda i: (jnp.maximum(i * hb - 1, 0), 0)),
            pl.BlockSpec((GRID_W, C_PAD), lambda i: (jnp.minimum(i * hb + hb, n_halo - 1), 0)),
            lay(1, C_PAD),
            lay(2, C_WIDTH),
            lay(128, 2 * C_WIDTH),
            lay(2, C_WIDTH),
            lay(128, 2 * C_WIDTH),
            lay(1, C_WIDTH),
            lay(1, C_WIDTH),
            lay(1, C_WIDTH),
            lay(256, C_WIDTH),
            pl.BlockSpec((256, 256), lambda i: (0, 0)),
        ],
        out_specs=[pl.BlockSpec((tm, n * C_WIDTH), lambda i: (i, 0)) for n, _ in outs],
        out_shape=[jax.ShapeDtypeStruct((m, n * C_WIDTH), dt) for n, dt in outs],
        compiler_params=_cparams(("arbitrary",)),
        name="rwkv_prep",
    )(z, z, z, p["mu"], p["w0"], p["w2"], p["a0"], p["a2"], p["k_k"], p["k_a"], p["r_k"], p["g2"],
      p["e256"])


def _bmm(a, b):
    return jnp.einsum("uik,ukj->uij", a.astype(BF16), b.astype(BF16), preferred_element_type=F32)


def _bmm_nt(a, b):
    return jnp.einsum("uik,ujk->uij", a.astype(BF16), b.astype(BF16), preferred_element_type=F32)


def _block_diag(y, bd_mask):
    return jnp.where(bd_mask, jnp.concatenate([y] * UNIT_HEADS, axis=1), jnp.zeros((), y.dtype))


def _to_units(x):
    return jnp.stack([x[:, p * UNIT_W:(p + 1) * UNIT_W] for p in range(N_UNITS)], axis=0)


def _scan_chunk(reverse, r, v, kk, ld, km, ka, hs):
    t = SCAN_T
    w = UNIT_W
    ti = lax.broadcasted_iota(jnp.int32, (t, w), 0)
    si = lax.broadcasted_iota(jnp.int32, (t, w), 1) & (HEAD_DIM - 1)
    strict, incl = (si > ti, si >= ti) if reverse else (si < ti, si <= ti)
    eye = jnp.where(si == ti, 1.0, 0.0)
    t_row = lax.broadcasted_iota(jnp.int32, (t, t), 0)
    t_col = lax.broadcasted_iota(jnp.int32, (t, t), 1)
    tri = jnp.where((t_col >= t_row) if reverse else (t_col <= t_row), 1.0, 0.0).astype(BF16)
    bd_mask = (lax.broadcasted_iota(jnp.int32, (w, w), 0) // HEAD_DIM
               == lax.broadcasted_iota(jnp.int32, (w, w), 1) // HEAD_DIM)

    cum = _dot_exact_lhs(tri, ld)
    c_end = cum[0:1, :] if reverse else cum[t - 1:t, :]
    p_end = jnp.exp(c_end - cum)
    p_inv = jnp.exp(-cum)
    q = _to_units(jnp.concatenate([kk * jnp.exp(cum - ld), r * jnp.exp(cum)], axis=0).astype(BF16))
    ai = _to_units((ka * p_inv).astype(BF16))
    ki = _to_units((km * p_inv).astype(BF16))
    vb = _to_units(v.astype(BF16))
    rr = jnp.concatenate([_block_diag(ai, bd_mask), _block_diag(ki, bd_mask)], axis=1)
    sc = _bmm_nt(q, rr)
    la = jnp.where(strict, sc[:, 0:t, 0:w], 0.0)
    lk = jnp.where(strict, sc[:, 0:t, w:2 * w], 0.0)
    ma = jnp.where(incl, sc[:, t:2 * t, 0:w], 0.0)
    mk = jnp.where(incl, sc[:, t:2 * t, w:2 * w], 0.0)

    n_pow = -la
    x_inv = eye + n_pow
    n_pow = _bmm(n_pow, _block_diag(n_pow.astype(BF16), bd_mask))
    for _ in range(int(np.log2(t)) - 2):
        both = _bmm(jnp.concatenate([n_pow, x_inv], axis=1), _block_diag(n_pow.astype(BF16), bd_mask))
        x_inv = x_inv + both[:, t:2 * t]
        n_pow = both[:, 0:t]
    x_inv = x_inv + _bmm(x_inv, _block_diag(n_pow.astype(BF16), bd_mask))

    hb = _bmm_nt(q, hs)
    lv = _bmm(jnp.concatenate([lk, mk], axis=1), _block_diag(vb, bd_mask))
    u = -_bmm(x_inv, _block_diag((hb[:, 0:t] + lv[:, 0:t]).astype(BF16), bd_mask))
    y = hb[:, t:2 * t] + lv[:, t:2 * t] + _bmm(ma, _block_diag(u.astype(BF16), bd_mask))
    lhs = _to_units(jnp.concatenate([ka * p_end, km * p_end], axis=0).astype(BF16))
    rhs_t = jnp.swapaxes(jnp.concatenate([u, _to_units(v)], axis=1), 1, 2)
    upd = _bmm(rhs_t, lhs)
    hs_new = _to_units(jnp.exp(c_end))[:, 0:1, :] * hs + jnp.where(bd_mask, upd, 0.0)
    return y, hs_new


def _scan_kernel(*refs, has_s0, emit_state, aliased_state):
    ins = refs[:6]
    pos = 6
    s0_ref = None
    if has_s0:
        s0_ref = refs[pos]
        pos += 1
    if aliased_state:
        pos += 1
    y_refs = refs[pos:pos + 2]
    pos += 2
    st_ref = None
    if emit_state:
        st_ref = refs[pos]
        pos += 1
    h_ref = refs[pos]
    c = pl.program_id(1)
    hd = HEAD_DIM

    n_group = h_ref.shape[0]

    @pl.when(c == 0)
    def _():
        if has_s0:
            zero = jnp.zeros((hd, hd), F32)
            for s in range(n_group):
                for d in range(2):
                    for p in range(N_UNITS):
                        rows = [jnp.concatenate([s0_ref[s, d, UNIT_HEADS * p + a] if a == b else zero
                                                 for b in range(UNIT_HEADS)], axis=1) for a in range(UNIT_HEADS)]
                        h_ref[s, d, p] = jnp.concatenate(rows, axis=0)
        else:
            h_ref[...] = jnp.zeros_like(h_ref)

    for s in range(n_group):
        for d, y_ref in enumerate(y_refs):
            rvk_ref, kd_ref, ld_ref = ins[3 * d:3 * d + 3]
            r, v, kk = (rvk_ref[s, :, n * C_WIDTH:(n + 1) * C_WIDTH].astype(F32) for n in range(3))
            km, ka = (kd_ref[s, :, n * C_WIDTH:(n + 1) * C_WIDTH].astype(F32) for n in range(2))
            y, hs_new = _scan_chunk(d == 1, r, v, kk, ld_ref[s], km, ka, h_ref[s, d])
            for p in range(N_UNITS):
                y_ref[s, :, p * UNIT_W:(p + 1) * UNIT_W] = y[p]
            h_ref[s, d] = hs_new

    if emit_state:
        @pl.when(c == pl.num_programs(1) - 1)
        def _():
            for s in range(n_group):
                for d in range(2):
                    for p in range(N_UNITS):
                        hs = h_ref[s, d, p]
                        for a in range(UNIT_HEADS):
                            st_ref[s, d, UNIT_HEADS * p + a] = hs[a * hd:(a + 1) * hd, a * hd:(a + 1) * hd]


def _rwkv_scan(arrs, n_seq, n_chunk, row0, layer, group, s0=None, state_shape=None, state_prev=None):
    seq_len = n_chunk * SCAN_T
    m_all = arrs[0].shape[0]
    assert row0 % (seq_len * group) == 0 and m_all % seq_len == 0 and n_seq % group == 0
    rvk, kf, kb, ld = (a.reshape(m_all // seq_len, seq_len, a.shape[1]) for a in arrs)
    g0 = row0 // (seq_len * group)
    fwd = lambda b, c: (b, c, 0)
    bwd = lambda b, c: (b, n_chunk - 1 - c, 0)
    blk = lambda imap, n=1: pl.BlockSpec((group, SCAN_T, n * C_WIDTH), imap)
    in_specs = [blk(lambda b, c: (g0 + b, c, 0), 3), blk(lambda b, c: (g0 + b, c, 0), 2),
                blk(lambda b, c: (g0 + b, c, 0)),
                blk(lambda b, c: (g0 + b, n_chunk - 1 - c, 0), 3), blk(lambda b, c: (g0 + b, n_chunk - 1 - c, 0), 2),
                blk(lambda b, c: (g0 + b, n_chunk - 1 - c, 1))]
    args = [rvk, kf, ld, rvk, kb, ld]
    st_spec = pl.BlockSpec((group, None, 2, N_HEADS, HEAD_DIM, HEAD_DIM), lambda b, c: (b, layer, 0, 0, 0, 0))
    if s0 is not None:
        in_specs.append(st_spec)
        args.append(s0)
    aliases = {}
    if state_prev is not None:
        aliases = {len(args): 2}
        in_specs.append(pl.BlockSpec(memory_space=pl.ANY))
        args.append(state_prev)
    y_shape = jax.ShapeDtypeStruct((n_seq, seq_len, C_WIDTH), F32)
    out_specs = [blk(fwd), blk(bwd)]
    out_shape = [y_shape, y_shape]
    if state_shape is not None:
        out_specs.append(st_spec)
        out_shape.append(jax.ShapeDtypeStruct(state_shape, F32))
    outs = pl.pallas_call(
        functools.partial(_scan_kernel, has_s0=s0 is not None, emit_state=state_shape is not None,
                          aliased_state=state_prev is not None),
        grid=(n_seq // group, n_chunk),
        in_specs=in_specs,
        out_specs=out_specs,
        out_shape=out_shape,
        input_output_aliases=aliases,
        scratch_shapes=[pltpu.VMEM((group, 2, N_UNITS, UNIT_W, UNIT_W), F32)],
        compiler_params=_cparams(("arbitrary", "arbitrary")),
        name="rwkv_scan",
    )(*args)
    ys = [y.reshape(n_seq * seq_len, C_WIDTH) for y in outs[:2]]
    return ys + list(outs[2:])


def _post_kernel(yf1_ref, yb1_ref, yf2_ref, yb2_ref, bv_ref, g_ref, lg_ref, lb_ref, e_ref, mix_ref, o_ref, *,
                 n_ctx_tiles):
    del mix_ref
    is_ctx = pl.program_id(0) < n_ctx_tiles
    y = jnp.where(is_ctx, yf1_ref[...] + yb1_ref[...], yf2_ref[...] + yb2_ref[...])
    inv = 1.0 / HEAD_DIM
    m = _seg_sum(y, e_ref) * inv
    yc = y - m
    var = _seg_sum(yc * yc, e_ref) * inv
    yn = yc * lax.rsqrt(var + LNX_EPS) * lg_ref[...] + lb_ref[...]
    o_ref[...] = ((yn + bv_ref[...]) * g_ref[...]).astype(BF16)


def _rwkv_post(y_ctx, y_lat, bv, g, lnx_g, lnx_b, e256, mix, layer, n_ctx_tiles, tm):
    m = bv.shape[0]
    n_tiles = m // tm
    row = pl.BlockSpec((tm, C_WIDTH), lambda i: (i, 0))
    row_ctx = pl.BlockSpec((tm, C_WIDTH), lambda i: (jnp.minimum(i, n_ctx_tiles - 1), 0))
    row_lat = pl.BlockSpec((tm, C_WIDTH), lambda i: (jnp.maximum(i - n_ctx_tiles, 0), 0))
    vec = pl.BlockSpec((None, 1, C_WIDTH), lambda i: (layer, 0, 0))
    return pl.pallas_call(
        functools.partial(_post_kernel, n_ctx_tiles=n_ctx_tiles),
        grid=(n_tiles,),
        in_specs=[row_ctx, row_ctx, row_lat, row_lat, row, row, vec, vec,
                  pl.BlockSpec((256, 256), lambda i: (0, 0)), pl.BlockSpec(memory_space=pl.ANY)],
        out_specs=pl.BlockSpec((tm, C_WIDTH), lambda i: (i, (A_WIDTH + B_WIDTH) // C_WIDTH)),
        out_shape=jax.ShapeDtypeStruct(mix.shape, mix.dtype),
        input_output_aliases={9: 0},
        compiler_params=_cparams(("arbitrary",)),
        name="rwkv_post",
    )(y_ctx[0], y_ctx[1], y_lat[0], y_lat[1], bv, g, lnx_g, lnx_b, e256, mix)


def _final_norm_kernel(x_ref, g_ref, op_ref, os_ref, *, n_ctx_tiles):
    x = x_ref[...]
    ms = jnp.mean(x * x, axis=-1, keepdims=True)
    y = x * lax.rsqrt(ms + RMS_EPS) * g_ref[...]
    i = pl.program_id(0)

    @pl.when(i < n_ctx_tiles)
    def _():
        op_ref[...] = y

    @pl.when(i >= n_ctx_tiles)
    def _():
        os_ref[...] = y


def _final_norm(x, g, m_ctx, tm):
    m = x.shape[0]
    n_ctx_tiles = m_ctx // tm
    return pl.pallas_call(
        functools.partial(_final_norm_kernel, n_ctx_tiles=n_ctx_tiles),
        grid=(m // tm,),
        in_specs=[pl.BlockSpec((tm, D_MODEL), lambda i: (i, 0)), pl.BlockSpec((1, D_MODEL), lambda i: (0, 0))],
        out_specs=[pl.BlockSpec((tm, D_MODEL), lambda i: (jnp.minimum(i, n_ctx_tiles - 1), 0)),
                   pl.BlockSpec((tm, D_MODEL), lambda i: (jnp.maximum(i - n_ctx_tiles, 0), 0))],
        out_shape=[jax.ShapeDtypeStruct((m_ctx, D_MODEL), F32), jax.ShapeDtypeStruct((m - m_ctx, D_MODEL), F32)],
        compiler_params=_cparams(("arbitrary",)),
        name="final_norm",
    )(x, g)


def kernel(x_prompt, x_sample, state_wkv, c, c_ctx, norm_g, w_mod, b_mod, ffn_w_in, ffn_w_out, w_in, w_out,
           sgu_ln_g, sgu_ln_b, sgu_w, sgu_b, shift_mu, decay_w0, decay_w2, iclr_a0, iclr_a2, k_k, k_a, r_k,
           gate_w2, lnx_g, lnx_b, final_g):
    batch, seq, d = x_prompt.shape
    dec_batch, dec_seq, _ = x_sample.shape
    depth = w_mod.shape[0]
    assert d == D_MODEL and dec_batch + 1 <= MOD_ROWS
    m_ctx = batch * seq
    tm = 1024
    ts = 512
    tp = 256
    assert seq == tp and dec_seq == tm and m_ctx % tm == 0 and dec_seq % GRID_W == 0

    def row_of_tile_fn(rows):
        n_ctx = m_ctx // rows
        per_lat = dec_seq // rows
        return lambda i: jnp.where(i < n_ctx, 0, 1 + (i - n_ctx) // per_lat)

    zeros_cw = jnp.zeros((depth, DECAY_RANK, C_WIDTH), F32)

    def both_dirs(w):
        top = jnp.concatenate([w[:, 0], zeros_cw], axis=-1)
        bot = jnp.concatenate([zeros_cw, w[:, 1]], axis=-1)
        return jnp.concatenate([top, bot], axis=1)

    e_np = (np.arange(256)[:, None] // HEAD_DIM == np.arange(256)[None, :] // HEAD_DIM)
    prep_params = {
        "mu": jnp.pad(shift_mu, ((0, 0), (0, C_PAD - C_IN))).reshape(depth, 1, C_PAD),
        "w0": decay_w0, "w2": both_dirs(decay_w2), "a0": iclr_a0, "a2": both_dirs(iclr_a2),
        "k_k": k_k.reshape(depth, 1, C_WIDTH), "k_a": k_a.reshape(depth, 1, C_WIDTH),
        "r_k": r_k.reshape(depth, 1, C_WIDTH),
        "g2": jnp.pad(gate_w2, ((0, 0), (0, 256 - GATE_RANK), (0, 0))).astype(BF16),
        "e256": jnp.asarray(e_np, BF16),
    }
    cd, sd = _dft_mats(B_GROUP_DIM)
    cs, ss = _dft_mats(seq)
    cl, sl = _dft_mats(dec_seq)
    fnet_consts = (_hi_lo(np.concatenate([cd, sd], axis=1)) + _hi_lo(np.concatenate([cs, -ss], axis=1))
                   + _hi_lo(np.concatenate([cl, -sl], axis=1)))
    sgu_bias = jnp.repeat(jnp.swapaxes(sgu_b, 1, 2), A_WIDTH // A_HEADS, axis=2)
    sgu_w_b = sgu_w.astype(BF16)
    ln_g = sgu_ln_g.reshape(depth, 1, A_WIDTH)
    ln_b = sgu_ln_b.reshape(depth, 1, A_WIDTH)
    lnx_g3 = lnx_g.reshape(depth, 1, C_WIDTH)
    lnx_b3 = lnx_b.reshape(depth, 1, C_WIDTH)
    norm_g4 = norm_g.reshape(depth, 3, 1, d)

    cond = jnp.concatenate([c_ctx[None, :], c, jnp.zeros((MOD_ROWS - 1 - dec_batch, d), F32)], axis=0)
    mod = _modulation(cond, w_mod, b_mod).reshape(depth, MOD_ROWS, N_MOD, 1, d)

    x = jnp.concatenate([x_prompt.reshape(m_ctx, d), x_sample.reshape(dec_batch * dec_seq, d)], axis=0)
    rot = row_of_tile_fn(tm)
    n_chunk_ctx = seq // SCAN_T
    n_chunk_lat = dec_seq // SCAN_T
    state_shape = (batch, depth, 2, N_HEADS, HEAD_DIM, HEAD_DIM)
    tk = 512
    scan_group = lambda n_seq: 2 if n_seq % 2 == 0 else 1
    w_in_t = jnp.swapaxes(w_in, 1, 2).astype(BF16)
    w_out_b = w_out.astype(BF16)
    new_state = None
    for l in range(depth):
        x = _ffn(x, mod, norm_g4, ffn_w_in, ffn_w_out, l, 0, 0, rot, tm)
        z = _in_proj(x, mod, norm_g4, w_in_t, l, rot, tm)
        mix = _sgu(z, ln_g, ln_b, sgu_w_b, sgu_bias, l, ts)
        mix = _fnet(z, fnet_consts, mix, m_ctx // tm, seq, tm)
        *scan_in, bv, g = _rwkv_prep(z, prep_params, l, m_ctx // tp, dec_seq // tp, tp)
        yf_c, yb_c, new_state = _rwkv_scan(scan_in, batch, n_chunk_ctx, 0, l, scan_group(batch),
                                           state_shape=state_shape, state_prev=new_state)
        yf_l, yb_l = _rwkv_scan(scan_in, dec_batch, n_chunk_lat, m_ctx, l, scan_group(dec_batch), s0=state_wkv)
        mix = _rwkv_post((yf_c, yb_c), (yf_l, yb_l), bv, g, lnx_g3, lnx_b3, prep_params["e256"], mix, l,
                         m_ctx // ts, ts)
        x = _resid_matmul(x, mod, mix, w_out_b, pl.BlockSpec((None, tk, D_MODEL), lambda i, k: (l, k, 0)),
                          l, 5, 1.0, rot, tm, tk, "mix_out")
        x = _ffn(x, mod, norm_g4, ffn_w_in, ffn_w_out, l, 2, 1, rot, tm)
    y_ctx, y_lat = _final_norm(x, final_g.reshape(1, d), m_ctx, ts)
    return (y_ctx.reshape(batch, seq, d), y_lat.reshape(dec_batch, dec_seq, d), new_state)
```

```python
import functools

import numpy as np
import jax
import jax.numpy as jnp
from jax import lax
from jax.experimental import pallas as pl
from jax.experimental.pallas import tpu as pltpu

F32 = jnp.float32
BF16 = jnp.bfloat16

D_MODEL = 2048
GRID_W = 64
SGU_CHUNK = 128
A_HEADS = 4
A_WIDTH = 512
B_WIDTH = 512
B_GROUP_DIM = 128
C_WIDTH = 1024
MIX_WIDTH = A_WIDTH + B_WIDTH + C_WIDTH
HEAD_DIM = 64
N_HEADS = 16
UNIT_HEADS = 2
UNIT_W = UNIT_HEADS * HEAD_DIM
N_UNITS = N_HEADS // UNIT_HEADS
DECAY_RANK = 64
GATE_RANK = 160
C_IN = 3488
C_PAD = 3584
IN_COLS = 2 * A_WIDTH + B_WIDTH + C_IN
Z_BLOCK = 512
Z_COLS = C_PAD + 2 * A_WIDTH + B_WIDTH
D_FF = 5632
N_MOD = 9
RMS_EPS = 1e-6
LN_EPS = 1e-5
LNX_EPS = 64e-5
SCAN_T = 64
MOD_ROWS = 8

VMEM_LIMIT = 56 * 1024 * 1024
FFN_VMEM_LIMIT = 60 * 1024 * 1024


def _cparams(sem):
    return pltpu.CompilerParams(dimension_semantics=sem, vmem_limit_bytes=VMEM_LIMIT)


def _dot(a, b):
    return jnp.dot(a, b, preferred_element_type=F32)


def _split3(x):
    hi = x.astype(BF16)
    r1 = x - hi.astype(F32)
    mid = r1.astype(BF16)
    lo = (r1 - mid.astype(F32)).astype(BF16)
    return hi, mid, lo


def _dot_ones_rhs(x, e):
    hi = x.astype(BF16)
    lo = (x - hi.astype(F32)).astype(BF16)
    return _dot(hi, e) + _dot(lo, e)


def _dot_exact_lhs(e, x):
    hi, mid, lo = _split3(x)
    return _dot(e, hi) + _dot(e, mid) + _dot(e, lo)


def _dot3(a, b):
    ah = a.astype(BF16)
    al = (a - ah.astype(F32)).astype(BF16)
    bh = b.astype(BF16)
    bl = (b - bh.astype(F32)).astype(BF16)
    return _dot(ah, bh) + _dot(al, bh) + _dot(ah, bl)


def _sigmoid(x):
    return 1.0 / (1.0 + jnp.exp(-x))


def _silu(x):
    return x * _sigmoid(x)


def _gelu_tanh(x):
    return 0.5 * x * (1.0 + jnp.tanh(0.7978845608028654 * (x + 0.044715 * (x * x * x))))


def _mod_kernel(c_ref, w_ref, b_ref, o_ref):
    s = _silu(c_ref[...]).astype(BF16)
    o_ref[...] = _dot(s, w_ref[...].astype(BF16)) + b_ref[...]


def _modulation(cond, w_mod, b_mod):
    depth = w_mod.shape[0]
    n = w_mod.shape[2]
    tn = 1024
    return pl.pallas_call(
        _mod_kernel,
        grid=(depth, n // tn),
        in_specs=[
            pl.BlockSpec((MOD_ROWS, D_MODEL), lambda l, j: (0, 0)),
            pl.BlockSpec((None, D_MODEL, tn), lambda l, j: (l, 0, j)),
            pl.BlockSpec((None, 1, tn), lambda l, j: (l, 0, j)),
        ],
        out_specs=pl.BlockSpec((None, MOD_ROWS, tn), lambda l, j: (l, 0, j)),
        out_shape=jax.ShapeDtypeStruct((depth, MOD_ROWS, n), F32),
        compiler_params=_cparams(("arbitrary", "arbitrary")),
        name="modulation",
    )(cond, w_mod, b_mod.reshape(depth, 1, n))


def _mod_spec(layer, slot, row_of_tile):
    return pl.BlockSpec((None, None, None, 1, D_MODEL),
                        lambda i, *_: (layer, row_of_tile(i), slot, 0, 0))


def _norm_spec(layer, slot):
    return pl.BlockSpec((None, None, 1, D_MODEL), lambda i, *_: (layer, slot, 0, 0))


def _modulated_norm(x_ref, g_ref, sc_ref, sh_ref):
    x = x_ref[...]
    rs = lax.rsqrt(jnp.mean(x * x, axis=-1, keepdims=True) + RMS_EPS)
    gain = g_ref[...] * (1.0 + sc_ref[...])
    return ((x * rs) * gain + sh_ref[...]).astype(BF16)


def _in_proj_kernel(x_ref, sh_ref, sc_ref, g_ref, wt_ref, o_ref, h_ref, *, valid_cols):
    j = pl.program_id(1)

    @pl.when(j == 0)
    def _():
        h_ref[...] = _modulated_norm(x_ref, g_ref, sc_ref, sh_ref)

    out = lax.dot_general(h_ref[...], wt_ref[...].astype(BF16), (((1,), (1,)), ((), ())),
                          preferred_element_type=F32)
    tn = out.shape[1]
    if valid_cols % tn:
        col = j * tn + lax.broadcasted_iota(jnp.int32, (1, tn), 1)
        out = jnp.where(col < valid_cols, out, 0.0)
    o_ref[...] = out


def _in_proj(x, mod, norm_g, w_in_t, layer, row_of_tile, tm):
    m = x.shape[0]
    nb = Z_COLS // Z_BLOCK
    shift = C_PAD // Z_BLOCK
    assert (2 * A_WIDTH + B_WIDTH) % Z_BLOCK == 0 and -(-IN_COLS // Z_BLOCK) == nb
    return pl.pallas_call(
        functools.partial(_in_proj_kernel, valid_cols=IN_COLS),
        grid=(m // tm, nb),
        in_specs=[
            pl.BlockSpec((tm, D_MODEL), lambda i, j: (i, 0)),
            _mod_spec(layer, 3, row_of_tile),
            _mod_spec(layer, 4, row_of_tile),
            _norm_spec(layer, 1),
            pl.BlockSpec((None, Z_BLOCK, D_MODEL), lambda i, j: (layer, j, 0)),
        ],
        out_specs=pl.BlockSpec((tm, Z_BLOCK), lambda i, j: (i, (j + shift) % nb)),
        out_shape=jax.ShapeDtypeStruct((m, Z_COLS), F32),
        scratch_shapes=[pltpu.VMEM((tm, D_MODEL), BF16)],
        compiler_params=_cparams(("arbitrary", "arbitrary")),
        name="in_proj",
    )(x, mod, mod, norm_g, w_in_t)


def _ffn_kernel(x_ref, sh_ref, sc_ref, gt_ref, g_ref, wg_ref, wu_ref, wo_ref, o_ref, h_ref, *, tn):
    j = pl.program_id(1)

    @pl.when(j == 0)
    def _():
        h_ref[...] = _modulated_norm(x_ref, g_ref, sc_ref, sh_ref)
        o_ref[...] = jnp.zeros_like(o_ref)

    h = h_ref[...]
    gate = _dot(h, wg_ref[...].astype(BF16))
    up = _dot(h, wu_ref[...].astype(BF16))
    a = (_silu(gate) * up).astype(BF16)
    for n in range(o_ref.shape[1] // tn):
        cols = slice(n * tn, (n + 1) * tn)
        o_ref[:, cols] += _dot(a, wo_ref[:, cols].astype(BF16))

    @pl.when(j == pl.num_programs(1) - 1)
    def _():
        o_ref[...] = x_ref[...] + (0.5 * gt_ref[...]) * o_ref[...]


def _ffn(x, mod, norm_g, w_in, w_out, layer, sub, ffn_idx, row_of_tile, tm, tf=256):
    m = x.shape[0]
    nf = D_FF // tf
    return pl.pallas_call(
        functools.partial(_ffn_kernel, tn=512),
        grid=(m // tm, nf),
        in_specs=[
            pl.BlockSpec((tm, D_MODEL), lambda i, j: (i, 0)),
            _mod_spec(layer, 3 * sub, row_of_tile),
            _mod_spec(layer, 3 * sub + 1, row_of_tile),
            _mod_spec(layer, 3 * sub + 2, row_of_tile),
            _norm_spec(layer, sub),
            pl.BlockSpec((None, None, D_MODEL, tf), lambda i, j: (layer, ffn_idx, 0, j)),
            pl.BlockSpec((None, None, D_MODEL, tf), lambda i, j: (layer, ffn_idx, 0, nf + j)),
            pl.BlockSpec((None, None, tf, D_MODEL), lambda i, j: (layer, ffn_idx, j, 0)),
        ],
        out_specs=pl.BlockSpec((tm, D_MODEL), lambda i, j: (i, 0)),
        out_shape=jax.ShapeDtypeStruct((m, D_MODEL), F32),
        scratch_shapes=[pltpu.VMEM((tm, D_MODEL), BF16)],
        compiler_params=pltpu.CompilerParams(dimension_semantics=("arbitrary", "arbitrary"),
                                             vmem_limit_bytes=FFN_VMEM_LIMIT),
        name="ffn",
    )(x, mod, mod, mod, norm_g, w_in, w_in, w_out)


def _resid_matmul_kernel(x_ref, gt_ref, a_ref, w_ref, o_ref, *, coef, tn):
    k = pl.program_id(1)

    @pl.when(k == 0)
    def _():
        o_ref[...] = jnp.zeros_like(o_ref)

    a = a_ref[...]
    for n in range(o_ref.shape[1] // tn):
        cols = slice(n * tn, (n + 1) * tn)
        o_ref[:, cols] += _dot(a, w_ref[:, cols].astype(BF16))

    @pl.when(k == pl.num_programs(1) - 1)
    def _():
        o_ref[...] = x_ref[...] + (coef * gt_ref[...]) * o_ref[...]


def _resid_matmul(x, mod, a, w, w_spec, layer, gate_slot, coef, row_of_tile, tm, tk, name):
    m = x.shape[0]
    return pl.pallas_call(
        functools.partial(_resid_matmul_kernel, coef=coef, tn=512),
        grid=(m // tm, a.shape[1] // tk),
        in_specs=[
            pl.BlockSpec((tm, D_MODEL), lambda i, k: (i, 0)),
            _mod_spec(layer, gate_slot, row_of_tile),
            pl.BlockSpec((tm, tk), lambda i, k: (i, k)),
            w_spec,
        ],
        out_specs=pl.BlockSpec((tm, D_MODEL), lambda i, k: (i, 0)),
        out_shape=jax.ShapeDtypeStruct((m, D_MODEL), F32),
        compiler_params=_cparams(("arbitrary", "arbitrary")),
        name=name,
    )(x, mod, a, w)


def _sgu_kernel(zu_ref, zv_ref, lg_ref, lb_ref, w_ref, bias_ref, o_ref):
    u = _gelu_tanh(zu_ref[...])
    v = _gelu_tanh(zv_ref[...])
    mu = jnp.mean(v, axis=-1, keepdims=True)
    vc = v - mu
    var = jnp.mean(vc * vc, axis=-1, keepdims=True)
    vn = (vc * lax.rsqrt(var + LN_EPS) * lg_ref[...] + lb_ref[...]).astype(BF16)
    hd = A_WIDTH // A_HEADS
    for c in range(u.shape[0] // SGU_CHUNK):
        rows = slice(c * SGU_CHUNK, (c + 1) * SGU_CHUNK)
        for h in range(A_HEADS):
            cols = slice(h * hd, (h + 1) * hd)
            mixed = _dot(w_ref[h], vn[rows, cols]) + bias_ref[:, cols]
            o_ref[rows, cols] = (u[rows, cols] * mixed).astype(BF16)


def _sgu(z, ln_g, ln_b, w_s, bias, layer, tm):
    m = z.shape[0]
    cu = C_PAD // A_WIDTH
    return pl.pallas_call(
        _sgu_kernel,
        grid=(m // tm,),
        in_specs=[
            pl.BlockSpec((tm, A_WIDTH), lambda i: (i, cu)),
            pl.BlockSpec((tm, A_WIDTH), lambda i: (i, cu + 1)),
            pl.BlockSpec((None, 1, A_WIDTH), lambda i: (layer, 0, 0)),
            pl.BlockSpec((None, 1, A_WIDTH), lambda i: (layer, 0, 0)),
            pl.BlockSpec((None, A_HEADS, SGU_CHUNK, SGU_CHUNK), lambda i: (layer, 0, 0, 0)),
            pl.BlockSpec((None, SGU_CHUNK, A_WIDTH), lambda i: (layer, 0, 0)),
        ],
        out_specs=pl.BlockSpec((tm, A_WIDTH), lambda i: (i, 0)),
        out_shape=jax.ShapeDtypeStruct((m, MIX_WIDTH), BF16),
        compiler_params=_cparams(("arbitrary",)),
        name="sgu",
    )(z, z, ln_g, ln_b, w_s, bias)


def _dft_mats(n):
    idx = np.arange(n)
    ang = 2.0 * np.pi * ((idx[:, None] * idx[None, :]) % n) / n
    return np.cos(ang) / np.sqrt(n), np.sin(ang) / np.sqrt(n)


def _hi_lo(a):
    a32 = jnp.asarray(a, F32)
    hi = a32.astype(BF16)
    lo = (a32 - hi.astype(F32)).astype(BF16)
    return hi, lo


def _fnet_kernel(z_ref, fdh_ref, fdl_ref, fsh_ref, fsl_ref, flh_ref, fll_ref, mix_ref, o_ref, t_ref, *,
                 n_ctx_tiles, seq):
    del mix_ref
    tm = z_ref.shape[0]
    x = z_ref[...]
    xh = x.astype(BF16)
    xl = (x - xh.astype(F32)).astype(BF16)
    gd = B_GROUP_DIM
    for g in range(B_WIDTH // gd):
        cols = slice(g * gd, (g + 1) * gd)
        t = _dot(xh[:, cols], fdh_ref[...]) + _dot(xl[:, cols], fdh_ref[...]) + _dot(xh[:, cols], fdl_ref[...])
        t_ref[0:tm, cols] = t[:, 0:gd]
        t_ref[tm:2 * tm, cols] = t[:, gd:2 * gd]

    def position_dft(fh_ref, fl_ref, rows_in, rows_out):
        tc = t_ref[rows_in[0], :]
        ts = t_ref[rows_in[1], :]
        tt = jnp.concatenate([tc, ts], axis=0)
        th = tt.astype(BF16)
        tl = (tt - th.astype(F32)).astype(BF16)
        out = _dot(fh_ref[...], th) + _dot(fl_ref[...], th) + _dot(fh_ref[...], tl)
        o_ref[rows_out, :] = out.astype(BF16)

    is_ctx = pl.program_id(0) < n_ctx_tiles

    @pl.when(is_ctx)
    def _():
        for s in range(tm // seq):
            r0 = slice(s * seq, (s + 1) * seq)
            r1 = slice(tm + s * seq, tm + (s + 1) * seq)
            position_dft(fsh_ref, fsl_ref, (r0, r1), r0)

    @pl.when(jnp.logical_not(is_ctx))
    def _():
        position_dft(flh_ref, fll_ref, (slice(0, tm), slice(tm, 2 * tm)), slice(0, tm))


def _fnet(z, consts, mix, n_ctx_tiles, seq, tm):
    m = z.shape[0]
    cb = (C_PAD + 2 * A_WIDTH) // B_WIDTH
    full = lambda a: pl.BlockSpec(a.shape, lambda i: (0,) * a.ndim)
    return pl.pallas_call(
        functools.partial(_fnet_kernel, n_ctx_tiles=n_ctx_tiles, seq=seq),
        grid=(m // tm,),
        in_specs=([pl.BlockSpec((tm, B_WIDTH), lambda i: (i, cb))] + [full(a) for a in consts]
                  + [pl.BlockSpec(memory_space=pl.ANY)]),
        out_specs=pl.BlockSpec((tm, B_WIDTH), lambda i: (i, A_WIDTH // B_WIDTH)),
        out_shape=jax.ShapeDtypeStruct(mix.shape, mix.dtype),
        input_output_aliases={1 + len(consts): 0},
        scratch_shapes=[pltpu.VMEM((2 * tm, B_WIDTH), F32)],
        compiler_params=_cparams(("arbitrary",)),
        name="fnet",
    )(z, *consts, mix)


def _seg_sum(x, e_ref):
    w = e_ref.shape[0]
    parts = [_dot_ones_rhs(x[:, b * w:(b + 1) * w], e_ref[...]) for b in range(x.shape[1] // w)]
    return parts[0] if len(parts) == 1 else jnp.concatenate(parts, axis=1)


def _prep_kernel(zc_ref, zp_ref, zn_ref, mu_ref, w0_ref, w2_ref, a0_ref, a2_ref, kk_ref, ka_ref, rk_ref,
                 g2_ref, e_ref,
                 rvk_o, kf_o, kb_o, ld_o, bv_o, g_o, *,
                 n_ctx_tiles, tiles_per_lat):
    i = pl.program_id(0)
    tm = zc_ref.shape[0]
    is_ctx = i < n_ctx_tiles
    q = (i - n_ctx_tiles) % tiles_per_lat
    x = zc_ref[...]
    row = lax.broadcasted_iota(jnp.int32, (tm, 1), 0)
    lane = lax.broadcasted_iota(jnp.int32, (1, C_PAD), 1)
    period = jnp.where(is_ctx, tm, GRID_W)
    pos = row & (period - 1)
    prev1 = jnp.where(pos == 0, 0.0, pltpu.roll(x, 1, 0))
    next1 = jnp.where(pos == period - 1, 0.0, pltpu.roll(x, tm - 1, 0))
    up_halo = jnp.where(q > 0, zp_ref[...], 0.0)
    dn_halo = jnp.where(q < tiles_per_lat - 1, zn_ref[...], 0.0)
    up = jnp.concatenate([up_halo, x[:tm - GRID_W]], axis=0)
    down = jnp.concatenate([x[GRID_W:], dn_halo], axis=0)
    half = C_IN // 2
    quarter = C_IN // 4
    zs_ctx = jnp.where(lane < half, prev1, next1)
    zs_lat = jnp.where(lane < quarter, prev1,
                       jnp.where(lane < 2 * quarter, next1, jnp.where(lane < 3 * quarter, up, down)))
    zs = jnp.where(is_ctx, zs_ctx, zs_lat)
    z = x + (zs - x) * mu_ref[...]

    cw = C_WIDTH
    r = z[:, 0:cw]
    k = z[:, cw:2 * cw]
    v = z[:, 2 * cw:3 * cw]
    wd = z[:, 3 * cw:3 * cw + 128]
    ad = z[:, 3 * cw + 128:3 * cw + 256]
    gd = z[:, 3 * cw + 256:C_PAD]

    uw = _dot3(jnp.tanh(wd), w2_ref[...])
    ua = _dot3(ad, a2_ref[...])
    kk_raw = k * kk_ref[...]
    ss = _seg_sum(kk_raw * kk_raw, e_ref)
    kk = kk_raw / jnp.maximum(jnp.sqrt(ss), 1e-12)
    rvk_o[:, 0:cw] = r.astype(BF16)
    rvk_o[:, cw:2 * cw] = v.astype(BF16)
    rvk_o[:, 2 * cw:3 * cw] = kk.astype(BF16)
    km_sum = None
    for d, kd_o in enumerate((kf_o, kb_o)):
        u = uw[:, d * cw:(d + 1) * cw] + w0_ref[d:d + 1, :]
        ld_o[:, d * cw:(d + 1) * cw] = -float(np.exp(-0.5)) * _sigmoid(u)
        a = _sigmoid(ua[:, d * cw:(d + 1) * cw] + a0_ref[d:d + 1, :])
        km = k * (1.0 + (a - 1.0) * ka_ref[...])
        kd_o[:, 0:cw] = km.astype(BF16)
        kd_o[:, cw:2 * cw] = (kk * a).astype(BF16)
        km_sum = km if km_sum is None else km_sum + km
    bonus = _seg_sum(r * km_sum * rk_ref[...], e_ref)
    bv_o[...] = bonus * v
    g_o[...] = _dot(_sigmoid(gd).astype(BF16), g2_ref[...]).astype(BF16)


def _rwkv_prep(z, p, layer, n_ctx_tiles, tiles_per_lat, tm):
    m = z.shape[0]
    hb = tm // GRID_W
    n_halo = m // GRID_W
    lay = lambda *shape: pl.BlockSpec((None,) + shape, lambda i: (layer,) + (0,) * len(shape))
    outs = [(3, BF16), (2, BF16), (2, BF16), (2, F32), (1, F32), (1, BF16)]
    return pl.pallas_call(
        functools.partial(_prep_kernel, n_ctx_tiles=n_ctx_tiles, tiles_per_lat=tiles_per_lat),
        grid=(m // tm,),
        in_specs=[
            pl.BlockSpec((tm, C_PAD), lambda i: (i, 0)),
            pl.BlockSpec((GRID_W, C_PAD), lambda i: (jnp.maximum(i * hb - 1, 0), 0)),
            pl.BlockSpec((GRID_W, C_PAD), lambda i: (jnp.minimum(i * hb + hb, n_halo - 1), 0)),
            lay(1, C_PAD),
            lay(2, C_WIDTH),
            lay(128, 2 * C_WIDTH),
            lay(2, C_WIDTH),
            lay(128, 2 * C_WIDTH),
            lay(1, C_WIDTH),
            lay(1, C_WIDTH),
            lay(1, C_WIDTH),
            lay(256, C_WIDTH),
            pl.BlockSpec((256, 256), lambda i: (0, 0)),
        ],
        out_specs=[pl.BlockSpec((tm, n * C_WIDTH), lambda i: (i, 0)) for n, _ in outs],
        out_shape=[jax.ShapeDtypeStruct((m, n * C_WIDTH), dt) for n, dt in outs],
        compiler_params=_cparams(("arbitrary",)),
        name="rwkv_prep",
    )(z, z, z, p["mu"], p["w0"], p["w2"], p["a0"], p["a2"], p["k_k"], p["k_a"], p["r_k"], p["g2"],
      p["e256"])


def _bmm(a, b):
    return jnp.einsum("uik,ukj->uij", a.astype(BF16), b.astype(BF16), preferred_element_type=F32)


def _bmm_nt(a, b):
    return jnp.einsum("uik,ujk->uij", a.astype(BF16), b.astype(BF16), preferred_element_type=F32)


def _block_diag(y, bd_mask):
    return jnp.where(bd_mask, jnp.concatenate([y] * UNIT_HEADS, axis=1), jnp.zeros((), y.dtype))


def _to_units(x):
    return jnp.stack([x[:, p * UNIT_W:(p + 1) * UNIT_W] for p in range(N_UNITS)], axis=0)


def _scan_chunks(streams, hs):
    t = SCAN_T
    w = UNIT_W
    ti = lax.broadcasted_iota(jnp.int32, (t, w), 0)
    si = lax.broadcasted_iota(jnp.int32, (t, w), 1) & (HEAD_DIM - 1)
    eye = jnp.where(si == ti, 1.0, 0.0)
    t_row = lax.broadcasted_iota(jnp.int32, (t, t), 0)
    t_col = lax.broadcasted_iota(jnp.int32, (t, t), 1)
    bd_mask = (lax.broadcasted_iota(jnp.int32, (w, w), 0) // HEAD_DIM
               == lax.broadcasted_iota(jnp.int32, (w, w), 1) // HEAD_DIM)

    q, ai, ki, vb, vf, lhs, p_tot = [], [], [], [], [], [], []
    for reverse, r, v, kk, ld, km, ka in streams:
        tri = jnp.where((t_col >= t_row) if reverse else (t_col <= t_row), 1.0, 0.0).astype(BF16)
        cum = _dot_exact_lhs(tri, ld)
        c_end = cum[0:1, :] if reverse else cum[t - 1:t, :]
        p_end = jnp.exp(c_end - cum)
        p_inv = jnp.exp(-cum)
        q.append(_to_units(jnp.concatenate([kk * jnp.exp(cum - ld), r * jnp.exp(cum)], axis=0).astype(BF16)))
        ai.append(_to_units((ka * p_inv).astype(BF16)))
        ki.append(_to_units((km * p_inv).astype(BF16)))
        vb.append(_to_units(v.astype(BF16)))
        vf.append(_to_units(v))
        lhs.append(_to_units(jnp.concatenate([ka * p_end, km * p_end], axis=0).astype(BF16)))
        p_tot.append(_to_units(jnp.exp(c_end)))
    q, ai, ki, vb, vf, lhs, p_tot = (jnp.concatenate(a, axis=0) for a in (q, ai, ki, vb, vf, lhs, p_tot))

    def masked(x, strictly):
        parts = []
        for n, stream in enumerate(streams):
            if stream[0]:
                keep = si > ti if strictly else si >= ti
            else:
                keep = si < ti if strictly else si <= ti
            parts.append(jnp.where(keep, x[n * N_UNITS:(n + 1) * N_UNITS], 0.0))
        return jnp.concatenate(parts, axis=0)

    rr = jnp.concatenate([_block_diag(ai, bd_mask), _block_diag(ki, bd_mask)], axis=1)
    sc = _bmm_nt(q, rr)
    la = masked(sc[:, 0:t, 0:w], True)
    lk = masked(sc[:, 0:t, w:2 * w], True)
    ma = masked(sc[:, t:2 * t, 0:w], False)
    mk = masked(sc[:, t:2 * t, w:2 * w], False)

    n_pow = -la
    x_inv = eye + n_pow
    n_pow = _bmm(n_pow, _block_diag(n_pow.astype(BF16), bd_mask))
    for _ in range(int(np.log2(t)) - 2):
        both = _bmm(jnp.concatenate([n_pow, x_inv], axis=1), _block_diag(n_pow.astype(BF16), bd_mask))
        x_inv = x_inv + both[:, t:2 * t]
        n_pow = both[:, 0:t]
    x_inv = x_inv + _bmm(x_inv, _block_diag(n_pow.astype(BF16), bd_mask))

    hb = _bmm_nt(q, hs)
    lv = _bmm(jnp.concatenate([lk, mk], axis=1), _block_diag(vb, bd_mask))
    u = -_bmm(x_inv, _block_diag((hb[:, 0:t] + lv[:, 0:t]).astype(BF16), bd_mask))
    y = hb[:, t:2 * t] + lv[:, t:2 * t] + _bmm(ma, _block_diag(u.astype(BF16), bd_mask))
    rhs_t = jnp.swapaxes(jnp.concatenate([u, vf], axis=1), 1, 2)
    upd = _bmm(rhs_t, lhs)
    hs_new = p_tot * hs + jnp.where(bd_mask, upd, 0.0)
    return y, hs_new


def _scan_kernel(*refs, has_s0, emit_state, aliased_state):
    ins = refs[:6]
    pos = 6
    s0_ref = None
    if has_s0:
        s0_ref = refs[pos]
        pos += 1
    if aliased_state:
        pos += 1
    y_refs = refs[pos:pos + 2]
    pos += 2
    st_ref = None
    if emit_state:
        st_ref = refs[pos]
        pos += 1
    h_ref = refs[pos]
    c = pl.program_id(1)
    hd = HEAD_DIM

    n_group = h_ref.shape[0]

    @pl.when(c == 0)
    def _():
        if has_s0:
            zero = jnp.zeros((hd, hd), F32)
            for s in range(n_group):
                for d in range(2):
                    for p in range(N_UNITS):
                        rows = [jnp.concatenate([s0_ref[s, d, UNIT_HEADS * p + a] if a == b else zero
                                                 for b in range(UNIT_HEADS)], axis=1) for a in range(UNIT_HEADS)]
                        h_ref[s, d, p] = jnp.concatenate(rows, axis=0)
        else:
            h_ref[...] = jnp.zeros_like(h_ref)

    streams = []
    for s in range(n_group):
        for d in range(2):
            rvk_ref, kd_ref, ld_ref = ins[3 * d:3 * d + 3]
            r, v, kk = (rvk_ref[s, :, n * C_WIDTH:(n + 1) * C_WIDTH].astype(F32) for n in range(3))
            km, ka = (kd_ref[s, :, n * C_WIDTH:(n + 1) * C_WIDTH].astype(F32) for n in range(2))
            streams.append((d == 1, r, v, kk, ld_ref[s], km, ka))
    n_all = len(streams) * N_UNITS
    y, hs_new = _scan_chunks(streams, h_ref[...].reshape(n_all, UNIT_W, UNIT_W))
    h_ref[...] = hs_new.reshape(h_ref.shape)
    for s in range(n_group):
        for d, y_ref in enumerate(y_refs):
            for p in range(N_UNITS):
                y_ref[s, :, p * UNIT_W:(p + 1) * UNIT_W] = y[(2 * s + d) * N_UNITS + p]

    if emit_state:
        @pl.when(c == pl.num_programs(1) - 1)
        def _():
            for s in range(n_group):
                for d in range(2):
                    for p in range(N_UNITS):
                        hs = h_ref[s, d, p]
                        for a in range(UNIT_HEADS):
                            st_ref[s, d, UNIT_HEADS * p + a] = hs[a * hd:(a + 1) * hd, a * hd:(a + 1) * hd]


def _rwkv_scan(arrs, n_seq, n_chunk, row0, layer, group, s0=None, state_shape=None, state_prev=None):
    seq_len = n_chunk * SCAN_T
    m_all = arrs[0].shape[0]
    assert row0 % (seq_len * group) == 0 and m_all % seq_len == 0 and n_seq % group == 0
    rvk, kf, kb, ld = (a.reshape(m_all // seq_len, seq_len, a.shape[1]) for a in arrs)
    g0 = row0 // (seq_len * group)
    fwd = lambda b, c: (b, c, 0)
    bwd = lambda b, c: (b, n_chunk - 1 - c, 0)
    blk = lambda imap, n=1: pl.BlockSpec((group, SCAN_T, n * C_WIDTH), imap)
    in_specs = [blk(lambda b, c: (g0 + b, c, 0), 3), blk(lambda b, c: (g0 + b, c, 0), 2),
                blk(lambda b, c: (g0 + b, c, 0)),
                blk(lambda b, c: (g0 + b, n_chunk - 1 - c, 0), 3), blk(lambda b, c: (g0 + b, n_chunk - 1 - c, 0), 2),
                blk(lambda b, c: (g0 + b, n_chunk - 1 - c, 1))]
    args = [rvk, kf, ld, rvk, kb, ld]
    st_spec = pl.BlockSpec((group, None, 2, N_HEADS, HEAD_DIM, HEAD_DIM), lambda b, c: (b, layer, 0, 0, 0, 0))
    if s0 is not None:
        in_specs.append(st_spec)
        args.append(s0)
    aliases = {}
    if state_prev is not None:
        aliases = {len(args): 2}
        in_specs.append(pl.BlockSpec(memory_space=pl.ANY))
        args.append(state_prev)
    y_shape = jax.ShapeDtypeStruct((n_seq, seq_len, C_WIDTH), F32)
    out_specs = [blk(fwd), blk(bwd)]
    out_shape = [y_shape, y_shape]
    if state_shape is not None:
        out_specs.append(st_spec)
        out_shape.append(jax.ShapeDtypeStruct(state_shape, F32))
    outs = pl.pallas_call(
        functools.partial(_scan_kernel, has_s0=s0 is not None, emit_state=state_shape is not None,
                          aliased_state=state_prev is not None),
        grid=(n_seq // group, n_chunk),
        in_specs=in_specs,
        out_specs=out_specs,
        out_shape=out_shape,
        input_output_aliases=aliases,
        scratch_shapes=[pltpu.VMEM((group, 2, N_UNITS, UNIT_W, UNIT_W), F32)],
        compiler_params=_cparams(("arbitrary", "arbitrary")),
        name="rwkv_scan",
    )(*args)
    ys = [y.reshape(n_seq * seq_len, C_WIDTH) for y in outs[:2]]
    return ys + list(outs[2:])


def _post_kernel(yf1_ref, yb1_ref, yf2_ref, yb2_ref, bv_ref, g_ref, lg_ref, lb_ref, e_ref, mix_ref, o_ref, *,
                 n_ctx_tiles):
    del mix_ref
    is_ctx = pl.program_id(0) < n_ctx_tiles
    y = jnp.where(is_ctx, yf1_ref[...] + yb1_ref[...], yf2_ref[...] + yb2_ref[...])
    inv = 1.0 / HEAD_DIM
    m = _seg_sum(y, e_ref) * inv
    yc = y - m
    var = _seg_sum(yc * yc, e_ref) * inv
    yn = yc * lax.rsqrt(var + LNX_EPS) * lg_ref[...] + lb_ref[...]
    o_ref[...] = ((yn + bv_ref[...]) * g_ref[...]).astype(BF16)


def _rwkv_post(y_ctx, y_lat, bv, g, lnx_g, lnx_b, e256, mix, layer, n_ctx_tiles, tm):
    m = bv.shape[0]
    n_tiles = m // tm
    row = pl.BlockSpec((tm, C_WIDTH), lambda i: (i, 0))
    row_ctx = pl.BlockSpec((tm, C_WIDTH), lambda i: (jnp.minimum(i, n_ctx_tiles - 1), 0))
    row_lat = pl.BlockSpec((tm, C_WIDTH), lambda i: (jnp.maximum(i - n_ctx_tiles, 0), 0))
    vec = pl.BlockSpec((None, 1, C_WIDTH), lambda i: (layer, 0, 0))
    return pl.pallas_call(
        functools.partial(_post_kernel, n_ctx_tiles=n_ctx_tiles),
        grid=(n_tiles,),
        in_specs=[row_ctx, row_ctx, row_lat, row_lat, row, row, vec, vec,
                  pl.BlockSpec((256, 256), lambda i: (0, 0)), pl.BlockSpec(memory_space=pl.ANY)],
        out_specs=pl.BlockSpec((tm, C_WIDTH), lambda i: (i, (A_WIDTH + B_WIDTH) // C_WIDTH)),
        out_shape=jax.ShapeDtypeStruct(mix.shape, mix.dtype),
        input_output_aliases={9: 0},
        compiler_params=_cparams(("arbitrary",)),
        name="rwkv_post",
    )(y_ctx[0], y_ctx[1], y_lat[0], y_lat[1], bv, g, lnx_g, lnx_b, e256, mix)


def _final_norm_kernel(x_ref, g_ref, op_ref, os_ref, *, n_ctx_tiles):
    x = x_ref[...]
    ms = jnp.mean(x * x, axis=-1, keepdims=True)
    y = x * lax.rsqrt(ms + RMS_EPS) * g_ref[...]
    i = pl.program_id(0)

    @pl.when(i < n_ctx_tiles)
    def _():
        op_ref[...] = y

    @pl.when(i >= n_ctx_tiles)
    def _():
        os_ref[...] = y


def _final_norm(x, g, m_ctx, tm):
    m = x.shape[0]
    n_ctx_tiles = m_ctx // tm
    return pl.pallas_call(
        functools.partial(_final_norm_kernel, n_ctx_tiles=n_ctx_tiles),
        grid=(m // tm,),
        in_specs=[pl.BlockSpec((tm, D_MODEL), lambda i: (i, 0)), pl.BlockSpec((1, D_MODEL), lambda i: (0, 0))],
        out_specs=[pl.BlockSpec((tm, D_MODEL), lambda i: (jnp.minimum(i, n_ctx_tiles - 1), 0)),
                   pl.BlockSpec((tm, D_MODEL), lambda i: (jnp.maximum(i - n_ctx_tiles, 0), 0))],
        out_shape=[jax.ShapeDtypeStruct((m_ctx, D_MODEL), F32), jax.ShapeDtypeStruct((m - m_ctx, D_MODEL), F32)],
        compiler_params=_cparams(("arbitrary",)),
        name="final_norm",
    )(x, g)


def kernel(x_prompt, x_sample, state_wkv, c, c_ctx, norm_g, w_mod, b_mod, ffn_w_in, ffn_w_out, w_in, w_out,
           sgu_ln_g, sgu_ln_b, sgu_w, sgu_b, shift_mu, decay_w0, decay_w2, iclr_a0, iclr_a2, k_k, k_a, r_k,
           gate_w2, lnx_g, lnx_b, final_g):
    batch, seq, d = x_prompt.shape
    dec_batch, dec_seq, _ = x_sample.shape
    depth = w_mod.shape[0]
    assert d == D_MODEL and dec_batch + 1 <= MOD_ROWS
    m_ctx = batch * seq
    tm = 1024
    ts = 512
    tp = 256
    assert seq == tp and dec_seq == tm and m_ctx % tm == 0 and dec_seq % GRID_W == 0

    def row_of_tile_fn(rows):
        n_ctx = m_ctx // rows
        per_lat = dec_seq // rows
        return lambda i: jnp.where(i < n_ctx, 0, 1 + (i - n_ctx) // per_lat)

    zeros_cw = jnp.zeros((depth, DECAY_RANK, C_WIDTH), F32)

    def both_dirs(w):
        top = jnp.concatenate([w[:, 0], zeros_cw], axis=-1)
        bot = jnp.concatenate([zeros_cw, w[:, 1]], axis=-1)
        return jnp.concatenate([top, bot], axis=1)

    e_np = (np.arange(256)[:, None] // HEAD_DIM == np.arange(256)[None, :] // HEAD_DIM)
    prep_params = {
        "mu": jnp.pad(shift_mu, ((0, 0), (0, C_PAD - C_IN))).reshape(depth, 1, C_PAD),
        "w0": decay_w0, "w2": both_dirs(decay_w2), "a0": iclr_a0, "a2": both_dirs(iclr_a2),
        "k_k": k_k.reshape(depth, 1, C_WIDTH), "k_a": k_a.reshape(depth, 1, C_WIDTH),
        "r_k": r_k.reshape(depth, 1, C_WIDTH),
        "g2": jnp.pad(gate_w2, ((0, 0), (0, 256 - GATE_RANK), (0, 0))).astype(BF16),
        "e256": jnp.asarray(e_np, BF16),
    }
    cd, sd = _dft_mats(B_GROUP_DIM)
    cs, ss = _dft_mats(seq)
    cl, sl = _dft_mats(dec_seq)
    fnet_consts = (_hi_lo(np.concatenate([cd, sd], axis=1)) + _hi_lo(np.concatenate([cs, -ss], axis=1))
                   + _hi_lo(np.concatenate([cl, -sl], axis=1)))
    sgu_bias = jnp.repeat(jnp.swapaxes(sgu_b, 1, 2), A_WIDTH // A_HEADS, axis=2)
    sgu_w_b = sgu_w.astype(BF16)
    ln_g = sgu_ln_g.reshape(depth, 1, A_WIDTH)
    ln_b = sgu_ln_b.reshape(depth, 1, A_WIDTH)
    lnx_g3 = lnx_g.reshape(depth, 1, C_WIDTH)
    lnx_b3 = lnx_b.reshape(depth, 1, C_WIDTH)
    norm_g4 = norm_g.reshape(depth, 3, 1, d)

    cond = jnp.concatenate([c_ctx[None, :], c, jnp.zeros((MOD_ROWS - 1 - dec_batch, d), F32)], axis=0)
    mod = _modulation(cond, w_mod, b_mod).reshape(depth, MOD_ROWS, N_MOD, 1, d)

    x = jnp.concatenate([x_prompt.reshape(m_ctx, d), x_sample.reshape(dec_batch * dec_seq, d)], axis=0)
    rot = row_of_tile_fn(tm)
    n_chunk_ctx = seq // SCAN_T
    n_chunk_lat = dec_seq // SCAN_T
    state_shape = (batch, depth, 2, N_HEADS, HEAD_DIM, HEAD_DIM)
    tk = 512
    scan_group = lambda n_seq: 4 if n_seq % 4 == 0 else 1
    w_in_t = jnp.swapaxes(w_in, 1, 2).astype(BF16)
    w_out_b = w_out.astype(BF16)
    new_state = None
    for l in range(depth):
        x = _ffn(x, mod, norm_g4, ffn_w_in, ffn_w_out, l, 0, 0, rot, tm)
        z = _in_proj(x, mod, norm_g4, w_in_t, l, rot, tm)
        mix = _sgu(z, ln_g, ln_b, sgu_w_b, sgu_bias, l, ts)
        mix = _fnet(z, fnet_consts, mix, m_ctx // tm, seq, tm)
        *scan_in, bv, g = _rwkv_prep(z, prep_params, l, m_ctx // tp, dec_seq // tp, tp)
        yf_c, yb_c, new_state = _rwkv_scan(scan_in, batch, n_chunk_ctx, 0, l, scan_group(batch),
                                           state_shape=state_shape, state_prev=new_state)
        yf_l, yb_l = _rwkv_scan(scan_in, dec_batch, n_chunk_lat, m_ctx, l, scan_group(dec_batch), s0=state_wkv)
        mix = _rwkv_post((yf_c, yb_c), (yf_l, yb_l), bv, g, lnx_g3, lnx_b3, prep_params["e256"], mix, l,
                         m_ctx // ts, ts)
        x = _resid_matmul(x, mod, mix, w_out_b, pl.BlockSpec((None, tk, D_MODEL), lambda i, k: (l, k, 0)),
                          l, 5, 1.0, rot, tm, tk, "mix_out")
        x = _ffn(x, mod, norm_g4, ffn_w_in, ffn_w_out, l, 2, 1, rot, tm)
    y_ctx, y_lat = _final_norm(x, final_g.reshape(1, d), m_ctx, ts)
    return (y_ctx.reshape(batch, seq, d), y_lat.reshape(dec_batch, dec_seq, d), new_state)
```

```python
import functools

import numpy as np
import jax
import jax.numpy as jnp
from jax import lax
from jax.experimental import pallas as pl
from jax.experimental.pallas import tpu as pltpu

F32 = jnp.float32
BF16 = jnp.bfloat16

D_MODEL = 2048
GRID_W = 64
SGU_CHUNK = 128
A_HEADS = 4
A_WIDTH = 512
B_WIDTH = 512
B_GROUP_DIM = 128
C_WIDTH = 1024
MIX_WIDTH = A_WIDTH + B_WIDTH + C_WIDTH
HEAD_DIM = 64
N_HEADS = 16
UNIT_HEADS = 2
UNIT_W = UNIT_HEADS * HEAD_DIM
N_UNITS = N_HEADS // UNIT_HEADS
DECAY_RANK = 64
GATE_RANK = 160
C_IN = 3488
C_PAD = 3584
IN_COLS = 2 * A_WIDTH + B_WIDTH + C_IN
Z_BLOCK = 512
Z_COLS = C_PAD + 2 * A_WIDTH + B_WIDTH
D_FF = 5632
N_MOD = 9
RMS_EPS = 1e-6
LN_EPS = 1e-5
LNX_EPS = 64e-5
SCAN_T = 64
MOD_ROWS = 8

VMEM_LIMIT = 56 * 1024 * 1024
FFN_VMEM_LIMIT = 60 * 1024 * 1024


def _cparams(sem):
    return pltpu.CompilerParams(dimension_semantics=sem, vmem_limit_bytes=VMEM_LIMIT)


def _dot(a, b):
    return jnp.dot(a, b, preferred_element_type=F32)


def _split3(x):
    hi = x.astype(BF16)
    r1 = x - hi.astype(F32)
    mid = r1.astype(BF16)
    lo = (r1 - mid.astype(F32)).astype(BF16)
    return hi, mid, lo


def _dot_ones_rhs(x, e):
    hi = x.astype(BF16)
    lo = (x - hi.astype(F32)).astype(BF16)
    return _dot(hi, e) + _dot(lo, e)


def _dot_exact_lhs(e, x):
    hi, mid, lo = _split3(x)
    return _dot(e, hi) + _dot(e, mid) + _dot(e, lo)


def _dot3(a, b):
    ah = a.astype(BF16)
    al = (a - ah.astype(F32)).astype(BF16)
    bh = b.astype(BF16)
    bl = (b - bh.astype(F32)).astype(BF16)
    return _dot(ah, bh) + _dot(al, bh) + _dot(ah, bl)


def _sigmoid(x):
    return 1.0 / (1.0 + jnp.exp(-x))


def _silu(x):
    return x * _sigmoid(x)


def _gelu_tanh(x):
    return 0.5 * x * (1.0 + jnp.tanh(0.7978845608028654 * (x + 0.044715 * (x * x * x))))


def _mod_kernel(c_ref, w_ref, b_ref, o_ref):
    s = _silu(c_ref[...]).astype(BF16)
    o_ref[...] = _dot(s, w_ref[...].astype(BF16)) + b_ref[...]


def _modulation(cond, w_mod, b_mod):
    depth = w_mod.shape[0]
    n = w_mod.shape[2]
    tn = 1024
    return pl.pallas_call(
        _mod_kernel,
        grid=(depth, n // tn),
        in_specs=[
            pl.BlockSpec((MOD_ROWS, D_MODEL), lambda l, j: (0, 0)),
            pl.BlockSpec((None, D_MODEL, tn), lambda l, j: (l, 0, j)),
            pl.BlockSpec((None, 1, tn), lambda l, j: (l, 0, j)),
        ],
        out_specs=pl.BlockSpec((None, MOD_ROWS, tn), lambda l, j: (l, 0, j)),
        out_shape=jax.ShapeDtypeStruct((depth, MOD_ROWS, n), F32),
        compiler_params=_cparams(("arbitrary", "arbitrary")),
        name="modulation",
    )(cond, w_mod, b_mod.reshape(depth, 1, n))


def _mod_spec(layer, slot, row_of_tile):
    return pl.BlockSpec((None, None, None, 1, D_MODEL),
                        lambda i, *_: (layer, row_of_tile(i), slot, 0, 0))


def _norm_spec(layer, slot):
    return pl.BlockSpec((None, None, 1, D_MODEL), lambda i, *_: (layer, slot, 0, 0))


def _modulated_norm(x_ref, g_ref, sc_ref, sh_ref):
    x = x_ref[...]
    rs = lax.rsqrt(jnp.mean(x * x, axis=-1, keepdims=True) + RMS_EPS)
    gain = g_ref[...] * (1.0 + sc_ref[...])
    return ((x * rs) * gain + sh_ref[...]).astype(BF16)


def _in_proj_kernel(x_ref, sh_ref, sc_ref, g_ref, wt_ref, o_ref, h_ref, *, valid_cols):
    j = pl.program_id(1)

    def step(first):
        if first:
            h = _modulated_norm(x_ref, g_ref, sc_ref, sh_ref)
            h_ref[...] = h
        else:
            h = h_ref[...]
        out = lax.dot_general(h, wt_ref[...].astype(BF16), (((1,), (1,)), ((), ())),
                              preferred_element_type=F32)
        tn = out.shape[1]
        if valid_cols % tn:
            col = j * tn + lax.broadcasted_iota(jnp.int32, (1, tn), 1)
            out = jnp.where(col < valid_cols, out, 0.0)
        o_ref[...] = out

    pl.when(j == 0)(lambda: step(True))
    pl.when(j > 0)(lambda: step(False))


def _in_proj(x, mod, norm_g, w_in_t, layer, row_of_tile, tm):
    m = x.shape[0]
    nb = Z_COLS // Z_BLOCK
    shift = C_PAD // Z_BLOCK
    assert (2 * A_WIDTH + B_WIDTH) % Z_BLOCK == 0 and -(-IN_COLS // Z_BLOCK) == nb
    return pl.pallas_call(
        functools.partial(_in_proj_kernel, valid_cols=IN_COLS),
        grid=(m // tm, nb),
        in_specs=[
            pl.BlockSpec((tm, D_MODEL), lambda i, j: (i, 0)),
            _mod_spec(layer, 3, row_of_tile),
            _mod_spec(layer, 4, row_of_tile),
            _norm_spec(layer, 1),
            pl.BlockSpec((None, Z_BLOCK, D_MODEL), lambda i, j: (layer, j, 0)),
        ],
        out_specs=pl.BlockSpec((tm, Z_BLOCK), lambda i, j: (i, (j + shift) % nb)),
        out_shape=jax.ShapeDtypeStruct((m, Z_COLS), F32),
        scratch_shapes=[pltpu.VMEM((tm, D_MODEL), BF16)],
        compiler_params=_cparams(("arbitrary", "arbitrary")),
        name="in_proj",
    )(x, mod, mod, norm_g, w_in_t)


def _ffn_kernel(x_ref, sh_ref, sc_ref, gt_ref, g_ref, wg_ref, wu_ref, wo_ref, *rest, tn, final_norm):
    fg_ref = rest[0] if final_norm else None
    o_ref, h_ref = rest[-2:]
    j = pl.program_id(1)
    last = pl.num_programs(1) - 1

    def step(first, final):
        if first:
            h = _modulated_norm(x_ref, g_ref, sc_ref, sh_ref)
            h_ref[...] = h
        else:
            h = h_ref[...]
        gate = _dot(h, wg_ref[...].astype(BF16))
        up = _dot(h, wu_ref[...].astype(BF16))
        a = (_silu(gate) * up).astype(BF16)
        for n in range(o_ref.shape[1] // tn):
            cols = slice(n * tn, (n + 1) * tn)
            acc = _dot(a, wo_ref[:, cols].astype(BF16))
            if not first:
                acc = o_ref[:, cols] + acc
            if final:
                acc = x_ref[:, cols] + (0.5 * gt_ref[:, cols]) * acc
            o_ref[:, cols] = acc
        if final and final_norm:
            y = o_ref[...]
            rs = lax.rsqrt(jnp.mean(y * y, axis=-1, keepdims=True) + RMS_EPS)
            o_ref[...] = y * rs * fg_ref[...]

    pl.when(j == 0)(lambda: step(True, False))
    pl.when(jnp.logical_and(j > 0, j < last))(lambda: step(False, False))
    pl.when(j == last)(lambda: step(False, True))


def _ffn(x, mod, norm_g, w_in, w_out, layer, sub, ffn_idx, row_of_tile, tm, n_tiles, in_tile0=0, out_tile0=0,
         out_rows=None, out_prev=None, final_g=None, tf=256):
    nf = D_FF // tf
    out_rows = n_tiles * tm if out_rows is None else out_rows
    args = [x, mod, mod, mod, norm_g, w_in, w_in, w_out]
    in_specs = [
        pl.BlockSpec((tm, D_MODEL), lambda i, j: (in_tile0 + i, 0)),
        _mod_spec(layer, 3 * sub, row_of_tile),
        _mod_spec(layer, 3 * sub + 1, row_of_tile),
        _mod_spec(layer, 3 * sub + 2, row_of_tile),
        _norm_spec(layer, sub),
        pl.BlockSpec((None, None, D_MODEL, tf), lambda i, j: (layer, ffn_idx, 0, j)),
        pl.BlockSpec((None, None, D_MODEL, tf), lambda i, j: (layer, ffn_idx, 0, nf + j)),
        pl.BlockSpec((None, None, tf, D_MODEL), lambda i, j: (layer, ffn_idx, j, 0)),
    ]
    if final_g is not None:
        args.append(final_g)
        in_specs.append(pl.BlockSpec((1, D_MODEL), lambda i, j: (0, 0)))
    aliases = {}
    if out_prev is not None:
        aliases = {len(args): 0}
        args.append(out_prev)
        in_specs.append(pl.BlockSpec(memory_space=pl.ANY))
    return pl.pallas_call(
        functools.partial(_ffn_kernel, tn=512, final_norm=final_g is not None),
        grid=(n_tiles, nf),
        in_specs=in_specs,
        out_specs=pl.BlockSpec((tm, D_MODEL), lambda i, j: (out_tile0 + i, 0)),
        out_shape=jax.ShapeDtypeStruct((out_rows, D_MODEL), F32),
        input_output_aliases=aliases,
        scratch_shapes=[pltpu.VMEM((tm, D_MODEL), BF16)],
        compiler_params=pltpu.CompilerParams(dimension_semantics=("arbitrary", "arbitrary"),
                                             vmem_limit_bytes=FFN_VMEM_LIMIT),
        name="ffn",
    )(*args)


def _resid_matmul_kernel(x_ref, gt_ref, a_ref, w_ref, o_ref, *, coef, tn):
    k = pl.program_id(1)
    last = pl.num_programs(1) - 1

    def step(first, final):
        a = a_ref[...]
        for n in range(o_ref.shape[1] // tn):
            cols = slice(n * tn, (n + 1) * tn)
            acc = _dot(a, w_ref[:, cols].astype(BF16))
            if not first:
                acc = o_ref[:, cols] + acc
            if final:
                acc = x_ref[:, cols] + (coef * gt_ref[:, cols]) * acc
            o_ref[:, cols] = acc

    pl.when(k == 0)(lambda: step(True, False))
    pl.when(jnp.logical_and(k > 0, k < last))(lambda: step(False, False))
    pl.when(k == last)(lambda: step(False, True))


def _resid_matmul(x, mod, a, w, w_spec, layer, gate_slot, coef, row_of_tile, tm, tk, name):
    m = x.shape[0]
    return pl.pallas_call(
        functools.partial(_resid_matmul_kernel, coef=coef, tn=512),
        grid=(m // tm, a.shape[1] // tk),
        in_specs=[
            pl.BlockSpec((tm, D_MODEL), lambda i, k: (i, 0)),
            _mod_spec(layer, gate_slot, row_of_tile),
            pl.BlockSpec((tm, tk), lambda i, k: (i, k)),
            w_spec,
        ],
        out_specs=pl.BlockSpec((tm, D_MODEL), lambda i, k: (i, 0)),
        out_shape=jax.ShapeDtypeStruct((m, D_MODEL), F32),
        compiler_params=_cparams(("arbitrary", "arbitrary")),
        name=name,
    )(x, mod, a, w)


def _sgu_kernel(zu_ref, zv_ref, lg_ref, lb_ref, w_ref, bias_ref, o_ref):
    u = _gelu_tanh(zu_ref[...])
    v = _gelu_tanh(zv_ref[...])
    mu = jnp.mean(v, axis=-1, keepdims=True)
    vc = v - mu
    var = jnp.mean(vc * vc, axis=-1, keepdims=True)
    vn = (vc * lax.rsqrt(var + LN_EPS) * lg_ref[...] + lb_ref[...]).astype(BF16)
    hd = A_WIDTH // A_HEADS
    for c in range(u.shape[0] // SGU_CHUNK):
        rows = slice(c * SGU_CHUNK, (c + 1) * SGU_CHUNK)
        for h in range(A_HEADS):
            cols = slice(h * hd, (h + 1) * hd)
            mixed = _dot(w_ref[h], vn[rows, cols]) + bias_ref[:, cols]
            o_ref[rows, cols] = (u[rows, cols] * mixed).astype(BF16)


def _sgu(z, ln_g, ln_b, w_s, bias, layer, tm):
    m = z.shape[0]
    cu = C_PAD // A_WIDTH
    return pl.pallas_call(
        _sgu_kernel,
        grid=(m // tm,),
        in_specs=[
            pl.BlockSpec((tm, A_WIDTH), lambda i: (i, cu)),
            pl.BlockSpec((tm, A_WIDTH), lambda i: (i, cu + 1)),
            pl.BlockSpec((None, 1, A_WIDTH), lambda i: (layer, 0, 0)),
            pl.BlockSpec((None, 1, A_WIDTH), lambda i: (layer, 0, 0)),
            pl.BlockSpec((None, A_HEADS, SGU_CHUNK, SGU_CHUNK), lambda i: (layer, 0, 0, 0)),
            pl.BlockSpec((None, SGU_CHUNK, A_WIDTH), lambda i: (layer, 0, 0)),
        ],
        out_specs=pl.BlockSpec((tm, A_WIDTH), lambda i: (i, 0)),
        out_shape=jax.ShapeDtypeStruct((m, MIX_WIDTH), BF16),
        compiler_params=_cparams(("arbitrary",)),
        name="sgu",
    )(z, z, ln_g, ln_b, w_s, bias)


def _dft_mats(n):
    idx = np.arange(n)
    ang = 2.0 * np.pi * ((idx[:, None] * idx[None, :]) % n) / n
    return np.cos(ang) / np.sqrt(n), np.sin(ang) / np.sqrt(n)


def _hi_lo(a):
    a32 = jnp.asarray(a, F32)
    hi = a32.astype(BF16)
    lo = (a32 - hi.astype(F32)).astype(BF16)
    return hi, lo


def _fnet_kernel(z_ref, fdh_ref, fdl_ref, fsh_ref, fsl_ref, flh_ref, fll_ref, mix_ref, o_ref, t_ref, *,
                 n_ctx_tiles, seq):
    del mix_ref
    tm = z_ref.shape[0]
    x = z_ref[...]
    xh = x.astype(BF16)
    xl = (x - xh.astype(F32)).astype(BF16)
    gd = B_GROUP_DIM
    for g in range(B_WIDTH // gd):
        cols = slice(g * gd, (g + 1) * gd)
        t = _dot(xh[:, cols], fdh_ref[...]) + _dot(xl[:, cols], fdh_ref[...]) + _dot(xh[:, cols], fdl_ref[...])
        t_ref[0:tm, cols] = t[:, 0:gd]
        t_ref[tm:2 * tm, cols] = t[:, gd:2 * gd]

    def position_dft(fh_ref, fl_ref, rows_in, rows_out):
        tc = t_ref[rows_in[0], :]
        ts = t_ref[rows_in[1], :]
        tt = jnp.concatenate([tc, ts], axis=0)
        th = tt.astype(BF16)
        tl = (tt - th.astype(F32)).astype(BF16)
        out = _dot(fh_ref[...], th) + _dot(fl_ref[...], th) + _dot(fh_ref[...], tl)
        o_ref[rows_out, :] = out.astype(BF16)

    is_ctx = pl.program_id(0) < n_ctx_tiles

    @pl.when(is_ctx)
    def _():
        for s in range(tm // seq):
            r0 = slice(s * seq, (s + 1) * seq)
            r1 = slice(tm + s * seq, tm + (s + 1) * seq)
            position_dft(fsh_ref, fsl_ref, (r0, r1), r0)

    @pl.when(jnp.logical_not(is_ctx))
    def _():
        position_dft(flh_ref, fll_ref, (slice(0, tm), slice(tm, 2 * tm)), slice(0, tm))


def _fnet(z, consts, mix, n_ctx_tiles, seq, tm):
    m = z.shape[0]
    cb = (C_PAD + 2 * A_WIDTH) // B_WIDTH
    full = lambda a: pl.BlockSpec(a.shape, lambda i: (0,) * a.ndim)
    return pl.pallas_call(
        functools.partial(_fnet_kernel, n_ctx_tiles=n_ctx_tiles, seq=seq),
        grid=(m // tm,),
        in_specs=([pl.BlockSpec((tm, B_WIDTH), lambda i: (i, cb))] + [full(a) for a in consts]
                  + [pl.BlockSpec(memory_space=pl.ANY)]),
        out_specs=pl.BlockSpec((tm, B_WIDTH), lambda i: (i, A_WIDTH // B_WIDTH)),
        out_shape=jax.ShapeDtypeStruct(mix.shape, mix.dtype),
        input_output_aliases={1 + len(consts): 0},
        scratch_shapes=[pltpu.VMEM((2 * tm, B_WIDTH), F32)],
        compiler_params=_cparams(("arbitrary",)),
        name="fnet",
    )(z, *consts, mix)


def _seg_sum(x, e_ref):
    w = e_ref.shape[0]
    parts = [_dot_ones_rhs(x[:, b * w:(b + 1) * w], e_ref[...]) for b in range(x.shape[1] // w)]
    return parts[0] if len(parts) == 1 else jnp.concatenate(parts, axis=1)


def _prep_kernel(zc_ref, zp_ref, zn_ref, mu_ref, w0_ref, w2_ref, a0_ref, a2_ref, kk_ref, ka_ref, rk_ref,
                 g2_ref, e_ref,
                 rvk_o, kf_o, kb_o, ld_o, bv_o, g_o, *,
                 n_ctx_tiles, tiles_per_lat):
    i = pl.program_id(0)
    tm = zc_ref.shape[0]
    is_ctx = i < n_ctx_tiles
    q = (i - n_ctx_tiles) % tiles_per_lat
    x = zc_ref[...]
    row = lax.broadcasted_iota(jnp.int32, (tm, 1), 0)
    lane = lax.broadcasted_iota(jnp.int32, (1, C_PAD), 1)
    period = jnp.where(is_ctx, tm, GRID_W)
    pos = row & (period - 1)
    prev1 = jnp.where(pos == 0, 0.0, pltpu.roll(x, 1, 0))
    next1 = jnp.where(pos == period - 1, 0.0, pltpu.roll(x, tm - 1, 0))
    up_halo = jnp.where(q > 0, zp_ref[...], 0.0)
    dn_halo = jnp.where(q < tiles_per_lat - 1, zn_ref[...], 0.0)
    up = jnp.concatenate([up_halo, x[:tm - GRID_W]], axis=0)
    down = jnp.concatenate([x[GRID_W:], dn_halo], axis=0)
    half = C_IN // 2
    quarter = C_IN // 4
    zs_ctx = jnp.where(lane < half, prev1, next1)
    zs_lat = jnp.where(lane < quarter, prev1,
                       jnp.where(lane < 2 * quarter, next1, jnp.where(lane < 3 * quarter, up, down)))
    zs = jnp.where(is_ctx, zs_ctx, zs_lat)
    z = x + (zs - x) * mu_ref[...]

    cw = C_WIDTH
    r = z[:, 0:cw]
    k = z[:, cw:2 * cw]
    v = z[:, 2 * cw:3 * cw]
    wd = z[:, 3 * cw:3 * cw + 128]
    ad = z[:, 3 * cw + 128:3 * cw + 256]
    gd = z[:, 3 * cw + 256:C_PAD]

    uw = _dot3(jnp.tanh(wd), w2_ref[...])
    ua = _dot3(ad, a2_ref[...])
    kk_raw = k * kk_ref[...]
    ss = _seg_sum(kk_raw * kk_raw, e_ref)
    kk = kk_raw / jnp.maximum(jnp.sqrt(ss), 1e-12)
    rvk_o[:, 0:cw] = r.astype(BF16)
    rvk_o[:, cw:2 * cw] = v.astype(BF16)
    rvk_o[:, 2 * cw:3 * cw] = kk.astype(BF16)
    km_sum = None
    for d, kd_o in enumerate((kf_o, kb_o)):
        u = uw[:, d * cw:(d + 1) * cw] + w0_ref[d:d + 1, :]
        ld_o[:, d * cw:(d + 1) * cw] = -float(np.exp(-0.5)) * _sigmoid(u)
        a = _sigmoid(ua[:, d * cw:(d + 1) * cw] + a0_ref[d:d + 1, :])
        km = k * (1.0 + (a - 1.0) * ka_ref[...])
        kd_o[:, 0:cw] = km.astype(BF16)
        kd_o[:, cw:2 * cw] = (kk * a).astype(BF16)
        km_sum = km if km_sum is None else km_sum + km
    bonus = _seg_sum(r * km_sum * rk_ref[...], e_ref)
    bv_o[...] = bonus * v
    g_o[...] = _dot(_sigmoid(gd).astype(BF16), g2_ref[...]).astype(BF16)


def _rwkv_prep(z, p, layer, n_ctx_tiles, tiles_per_lat, tm):
    m = z.shape[0]
    hb = tm // GRID_W
    n_halo = m // GRID_W
    lay = lambda *shape: pl.BlockSpec((None,) + shape, lambda i: (layer,) + (0,) * len(shape))
    outs = [(3, BF16), (2, BF16), (2, BF16), (2, F32), (1, F32), (1, BF16)]
    return pl.pallas_call(
        functools.partial(_prep_kernel, n_ctx_tiles=n_ctx_tiles, tiles_per_lat=tiles_per_lat),
        grid=(m // tm,),
        in_specs=[
            pl.BlockSpec((tm, C_PAD), lambda i: (i, 0)),
            pl.BlockSpec((GRID_W, C_PAD), lambda i: (jnp.maximum(i * hb - 1, 0), 0)),
            pl.BlockSpec((GRID_W, C_PAD), lambda i: (jnp.minimum(i * hb + hb, n_halo - 1), 0)),
            lay(1, C_PAD),
            lay(2, C_WIDTH),
            lay(128, 2 * C_WIDTH),
            lay(2, C_WIDTH),
            lay(128, 2 * C_WIDTH),
            lay(1, C_WIDTH),
            lay(1, C_WIDTH),
            lay(1, C_WIDTH),
            lay(256, C_WIDTH),
            pl.BlockSpec((256, 256), lambda i: (0, 0)),
        ],
        out_specs=[pl.BlockSpec((tm, n * C_WIDTH), lambda i: (i, 0)) for n, _ in outs],
        out_shape=[jax.ShapeDtypeStruct((m, n * C_WIDTH), dt) for n, dt in outs],
        compiler_params=_cparams(("arbitrary",)),
        name="rwkv_prep",
    )(z, z, z, p["mu"], p["w0"], p["w2"], p["a0"], p["a2"], p["k_k"], p["k_a"], p["r_k"], p["g2"],
      p["e256"])


def _bmm(a, b):
    return jnp.einsum("uik,ukj->uij", a.astype(BF16), b.astype(BF16), preferred_element_type=F32)


def _bmm_nt(a, b):
    return jnp.einsum("uik,ujk->uij", a.astype(BF16), b.astype(BF16), preferred_element_type=F32)


def _block_diag(y, bd_mask):
    return jnp.where(bd_mask, jnp.concatenate([y] * UNIT_HEADS, axis=1), jnp.zeros((), y.dtype))


def _to_units(x):
    return jnp.stack([x[:, p * UNIT_W:(p + 1) * UNIT_W] for p in range(N_UNITS)], axis=0)


def _scan_chunks(streams, hs):
    t = SCAN_T
    w = UNIT_W
    ti = lax.broadcasted_iota(jnp.int32, (t, w), 0)
    si = lax.broadcasted_iota(jnp.int32, (t, w), 1) & (HEAD_DIM - 1)
    eye = jnp.where(si == ti, 1.0, 0.0)
    t_row = lax.broadcasted_iota(jnp.int32, (t, t), 0)
    t_col = lax.broadcasted_iota(jnp.int32, (t, t), 1)
    bd_mask = (lax.broadcasted_iota(jnp.int32, (w, w), 0) // HEAD_DIM
               == lax.broadcasted_iota(jnp.int32, (w, w), 1) // HEAD_DIM)

    q, ai, ki, vb, vf, lhs, p_tot = [], [], [], [], [], [], []
    for reverse, r, v, kk, ld, km, ka in streams:
        tri = jnp.where((t_col >= t_row) if reverse else (t_col <= t_row), 1.0, 0.0).astype(BF16)
        cum = _dot_exact_lhs(tri, ld)
        c_end = cum[0:1, :] if reverse else cum[t - 1:t, :]
        p_end = jnp.exp(c_end - cum)
        p_inv = jnp.exp(-cum)
        q.append(_to_units(jnp.concatenate([kk * jnp.exp(cum - ld), r * jnp.exp(cum)], axis=0).astype(BF16)))
        ai.append(_to_units((ka * p_inv).astype(BF16)))
        ki.append(_to_units((km * p_inv).astype(BF16)))
        vb.append(_to_units(v.astype(BF16)))
        vf.append(_to_units(v))
        lhs.append(_to_units(jnp.concatenate([ka * p_end, km * p_end], axis=0).astype(BF16)))
        p_tot.append(_to_units(jnp.exp(c_end)))
    q, ai, ki, vb, vf, lhs, p_tot = (jnp.concatenate(a, axis=0) for a in (q, ai, ki, vb, vf, lhs, p_tot))

    def masked(x, strictly):
        parts = []
        for n, stream in enumerate(streams):
            if stream[0]:
                keep = si > ti if strictly else si >= ti
            else:
                keep = si < ti if strictly else si <= ti
            parts.append(jnp.where(keep, x[n * N_UNITS:(n + 1) * N_UNITS], 0.0))
        return jnp.concatenate(parts, axis=0)

    rr = jnp.concatenate([_block_diag(ai, bd_mask), _block_diag(ki, bd_mask)], axis=1)
    sc = _bmm_nt(q, rr)
    la = masked(sc[:, 0:t, 0:w], True)
    lk = masked(sc[:, 0:t, w:2 * w], True)
    ma = masked(sc[:, t:2 * t, 0:w], False)
    mk = masked(sc[:, t:2 * t, w:2 * w], False)

    n_pow = -la
    x_inv = eye + n_pow
    n_pow = _bmm(n_pow, _block_diag(n_pow.astype(BF16), bd_mask))
    for _ in range(int(np.log2(t)) - 2):
        both = _bmm(jnp.concatenate([n_pow, x_inv], axis=1), _block_diag(n_pow.astype(BF16), bd_mask))
        x_inv = x_inv + both[:, t:2 * t]
        n_pow = both[:, 0:t]
    x_inv = x_inv + _bmm(x_inv, _block_diag(n_pow.astype(BF16), bd_mask))

    hb = _bmm_nt(q, hs)
    lv = _bmm(jnp.concatenate([lk, mk], axis=1), _block_diag(vb, bd_mask))
    u = -_bmm(x_inv, _block_diag((hb[:, 0:t] + lv[:, 0:t]).astype(BF16), bd_mask))
    y = hb[:, t:2 * t] + lv[:, t:2 * t] + _bmm(ma, _block_diag(u.astype(BF16), bd_mask))
    rhs_t = jnp.swapaxes(jnp.concatenate([u, vf], axis=1), 1, 2)
    upd = _bmm(rhs_t, lhs)
    hs_new = p_tot * hs + jnp.where(bd_mask, upd, 0.0)
    return y, hs_new


def _scan_kernel(*refs, has_s0, emit_state, aliased_state):
    ins = refs[:6]
    pos = 6
    s0_ref = None
    if has_s0:
        s0_ref = refs[pos]
        pos += 1
    if aliased_state:
        pos += 1
    y_refs = refs[pos:pos + 2]
    pos += 2
    st_ref = None
    if emit_state:
        st_ref = refs[pos]
        pos += 1
    h_ref = refs[pos]
    c = pl.program_id(1)
    hd = HEAD_DIM

    n_group = h_ref.shape[0]

    @pl.when(c == 0)
    def _():
        if has_s0:
            zero = jnp.zeros((hd, hd), F32)
            for s in range(n_group):
                for d in range(2):
                    for p in range(N_UNITS):
                        rows = [jnp.concatenate([s0_ref[s, d, UNIT_HEADS * p + a] if a == b else zero
                                                 for b in range(UNIT_HEADS)], axis=1) for a in range(UNIT_HEADS)]
                        h_ref[s, d, p] = jnp.concatenate(rows, axis=0)
        else:
            h_ref[...] = jnp.zeros_like(h_ref)

    streams = []
    for s in range(n_group):
        for d in range(2):
            rvk_ref, kd_ref, ld_ref = ins[3 * d:3 * d + 3]
            r, v, kk = (rvk_ref[s, :, n * C_WIDTH:(n + 1) * C_WIDTH].astype(F32) for n in range(3))
            km, ka = (kd_ref[s, :, n * C_WIDTH:(n + 1) * C_WIDTH].astype(F32) for n in range(2))
            streams.append((d == 1, r, v, kk, ld_ref[s], km, ka))
    n_all = len(streams) * N_UNITS
    y, hs_new = _scan_chunks(streams, h_ref[...].reshape(n_all, UNIT_W, UNIT_W))
    h_ref[...] = hs_new.reshape(h_ref.shape)
    for s in range(n_group):
        for d, y_ref in enumerate(y_refs):
            for p in range(N_UNITS):
                y_ref[s, :, p * UNIT_W:(p + 1) * UNIT_W] = y[(2 * s + d) * N_UNITS + p]

    if emit_state:
        @pl.when(c == pl.num_programs(1) - 1)
        def _():
            for s in range(n_group):
                for d in range(2):
                    for p in range(N_UNITS):
                        hs = h_ref[s, d, p]
                        for a in range(UNIT_HEADS):
                            st_ref[s, d, UNIT_HEADS * p + a] = hs[a * hd:(a + 1) * hd, a * hd:(a + 1) * hd]


def _rwkv_scan(arrs, n_seq, n_chunk, row0, layer, group, s0=None, state_shape=None, state_prev=None):
    seq_len = n_chunk * SCAN_T
    m_all = arrs[0].shape[0]
    assert row0 % (seq_len * group) == 0 and m_all % seq_len == 0 and n_seq % group == 0
    rvk, kf, kb, ld = (a.reshape(m_all // seq_len, seq_len, a.shape[1]) for a in arrs)
    g0 = row0 // (seq_len * group)
    fwd = lambda b, c: (b, c, 0)
    bwd = lambda b, c: (b, n_chunk - 1 - c, 0)
    blk = lambda imap, n=1: pl.BlockSpec((group, SCAN_T, n * C_WIDTH), imap)
    in_specs = [blk(lambda b, c: (g0 + b, c, 0), 3), blk(lambda b, c: (g0 + b, c, 0), 2),
                blk(lambda b, c: (g0 + b, c, 0)),
                blk(lambda b, c: (g0 + b, n_chunk - 1 - c, 0), 3), blk(lambda b, c: (g0 + b, n_chunk - 1 - c, 0), 2),
                blk(lambda b, c: (g0 + b, n_chunk - 1 - c, 1))]
    args = [rvk, kf, ld, rvk, kb, ld]
    st_spec = pl.BlockSpec((group, None, 2, N_HEADS, HEAD_DIM, HEAD_DIM), lambda b, c: (b, layer, 0, 0, 0, 0))
    if s0 is not None:
        in_specs.append(st_spec)
        args.append(s0)
    aliases = {}
    if state_prev is not None:
        aliases = {len(args): 2}
        in_specs.append(pl.BlockSpec(memory_space=pl.ANY))
        args.append(state_prev)
    y_shape = jax.ShapeDtypeStruct((n_seq, seq_len, C_WIDTH), F32)
    out_specs = [blk(fwd), blk(bwd)]
    out_shape = [y_shape, y_shape]
    if state_shape is not None:
        out_specs.append(st_spec)
        out_shape.append(jax.ShapeDtypeStruct(state_shape, F32))
    outs = pl.pallas_call(
        functools.partial(_scan_kernel, has_s0=s0 is not None, emit_state=state_shape is not None,
                          aliased_state=state_prev is not None),
        grid=(n_seq // group, n_chunk),
        in_specs=in_specs,
        out_specs=out_specs,
        out_shape=out_shape,
        input_output_aliases=aliases,
        scratch_shapes=[pltpu.VMEM((group, 2, N_UNITS, UNIT_W, UNIT_W), F32)],
        compiler_params=_cparams(("arbitrary", "arbitrary")),
        name="rwkv_scan",
    )(*args)
    ys = [y.reshape(n_seq * seq_len, C_WIDTH) for y in outs[:2]]
    return ys + list(outs[2:])


def _post_kernel(yf1_ref, yb1_ref, yf2_ref, yb2_ref, bv_ref, g_ref, lg_ref, lb_ref, e_ref, mix_ref, o_ref, *,
                 n_ctx_tiles):
    del mix_ref
    is_ctx = pl.program_id(0) < n_ctx_tiles
    y = jnp.where(is_ctx, yf1_ref[...] + yb1_ref[...], yf2_ref[...] + yb2_ref[...])
    inv = 1.0 / HEAD_DIM
    m = _seg_sum(y, e_ref) * inv
    yc = y - m
    var = _seg_sum(yc * yc, e_ref) * inv
    yn = yc * lax.rsqrt(var + LNX_EPS) * lg_ref[...] + lb_ref[...]
    o_ref[...] = ((yn + bv_ref[...]) * g_ref[...]).astype(BF16)


def _rwkv_post(y_ctx, y_lat, bv, g, lnx_g, lnx_b, e256, mix, layer, n_ctx_tiles, tm):
    m = bv.shape[0]
    n_tiles = m // tm
    row = pl.BlockSpec((tm, C_WIDTH), lambda i: (i, 0))
    row_ctx = pl.BlockSpec((tm, C_WIDTH), lambda i: (jnp.minimum(i, n_ctx_tiles - 1), 0))
    row_lat = pl.BlockSpec((tm, C_WIDTH), lambda i: (jnp.maximum(i - n_ctx_tiles, 0), 0))
    vec = pl.BlockSpec((None, 1, C_WIDTH), lambda i: (layer, 0, 0))
    return pl.pallas_call(
        functools.partial(_post_kernel, n_ctx_tiles=n_ctx_tiles),
        grid=(n_tiles,),
        in_specs=[row_ctx, row_ctx, row_lat, row_lat, row, row, vec, vec,
                  pl.BlockSpec((256, 256), lambda i: (0, 0)), pl.BlockSpec(memory_space=pl.ANY)],
        out_specs=pl.BlockSpec((tm, C_WIDTH), lambda i: (i, (A_WIDTH + B_WIDTH) // C_WIDTH)),
        out_shape=jax.ShapeDtypeStruct(mix.shape, mix.dtype),
        input_output_aliases={9: 0},
        compiler_params=_cparams(("arbitrary",)),
        name="rwkv_post",
    )(y_ctx[0], y_ctx[1], y_lat[0], y_lat[1], bv, g, lnx_g, lnx_b, e256, mix)


def kernel(x_prompt, x_sample, state_wkv, c, c_ctx, norm_g, w_mod, b_mod, ffn_w_in, ffn_w_out, w_in, w_out,
           sgu_ln_g, sgu_ln_b, sgu_w, sgu_b, shift_mu, decay_w0, decay_w2, iclr_a0, iclr_a2, k_k, k_a, r_k,
           gate_w2, lnx_g, lnx_b, final_g):
    batch, seq, d = x_prompt.shape
    dec_batch, dec_seq, _ = x_sample.shape
    depth = w_mod.shape[0]
    assert d == D_MODEL and dec_batch + 1 <= MOD_ROWS
    m_ctx = batch * seq
    tm = 1024
    ts = 512
    tp = 256
    assert seq == tp and dec_seq == tm and m_ctx % tm == 0 and dec_seq % GRID_W == 0

    def row_of_tile_fn(rows):
        n_ctx = m_ctx // rows
        per_lat = dec_seq // rows
        return lambda i: jnp.where(i < n_ctx, 0, 1 + (i - n_ctx) // per_lat)

    zeros_cw = jnp.zeros((depth, DECAY_RANK, C_WIDTH), F32)

    def both_dirs(w):
        top = jnp.concatenate([w[:, 0], zeros_cw], axis=-1)
        bot = jnp.concatenate([zeros_cw, w[:, 1]], axis=-1)
        return jnp.concatenate([top, bot], axis=1)

    e_np = (np.arange(256)[:, None] // HEAD_DIM == np.arange(256)[None, :] // HEAD_DIM)
    prep_params = {
        "mu": jnp.pad(shift_mu, ((0, 0), (0, C_PAD - C_IN))).reshape(depth, 1, C_PAD),
        "w0": decay_w0, "w2": both_dirs(decay_w2), "a0": iclr_a0, "a2": both_dirs(iclr_a2),
        "k_k": k_k.reshape(depth, 1, C_WIDTH), "k_a": k_a.reshape(depth, 1, C_WIDTH),
        "r_k": r_k.reshape(depth, 1, C_WIDTH),
        "g2": jnp.pad(gate_w2, ((0, 0), (0, 256 - GATE_RANK), (0, 0))).astype(BF16),
        "e256": jnp.asarray(e_np, BF16),
    }
    cd, sd = _dft_mats(B_GROUP_DIM)
    cs, ss = _dft_mats(seq)
    cl, sl = _dft_mats(dec_seq)
    fnet_consts = (_hi_lo(np.concatenate([cd, sd], axis=1)) + _hi_lo(np.concatenate([cs, -ss], axis=1))
                   + _hi_lo(np.concatenate([cl, -sl], axis=1)))
    sgu_bias = jnp.repeat(jnp.swapaxes(sgu_b, 1, 2), A_WIDTH // A_HEADS, axis=2)
    sgu_w_b = sgu_w.astype(BF16)
    ln_g = sgu_ln_g.reshape(depth, 1, A_WIDTH)
    ln_b = sgu_ln_b.reshape(depth, 1, A_WIDTH)
    lnx_g3 = lnx_g.reshape(depth, 1, C_WIDTH)
    lnx_b3 = lnx_b.reshape(depth, 1, C_WIDTH)
    norm_g4 = norm_g.reshape(depth, 3, 1, d)

    cond = jnp.concatenate([c_ctx[None, :], c, jnp.zeros((MOD_ROWS - 1 - dec_batch, d), F32)], axis=0)
    mod = _modulation(cond, w_mod, b_mod).reshape(depth, MOD_ROWS, N_MOD, 1, d)

    m = m_ctx + dec_batch * dec_seq
    n_ctx_t = m_ctx // tm
    n_lat_t = dec_batch * dec_seq // tm
    rot = row_of_tile_fn(tm)
    ctx_row = lambda i: 0
    lat_row = lambda i: 1 + i
    n_chunk_ctx = seq // SCAN_T
    n_chunk_lat = dec_seq // SCAN_T
    state_shape = (batch, depth, 2, N_HEADS, HEAD_DIM, HEAD_DIM)
    tk = 512
    scan_group = lambda n_seq: 4 if n_seq % 4 == 0 else 1
    w_in_t = jnp.swapaxes(w_in, 1, 2).astype(BF16)
    w_out_b = w_out.astype(BF16)
    new_state = None
    x = None
    for l in range(depth):
        ffn_args = (mod, norm_g4, ffn_w_in, ffn_w_out, l)
        if l == 0:
            x = _ffn(x_prompt.reshape(m_ctx, d), *ffn_args, 0, 0, ctx_row, tm, n_ctx_t, out_rows=m)
            x = _ffn(x_sample.reshape(m - m_ctx, d), *ffn_args, 0, 0, lat_row, tm, n_lat_t, out_tile0=n_ctx_t,
                     out_rows=m, out_prev=x)
        else:
            x = _ffn(x, *ffn_args, 0, 0, rot, tm, n_ctx_t + n_lat_t)
        z = _in_proj(x, mod, norm_g4, w_in_t, l, rot, tm)
        mix = _sgu(z, ln_g, ln_b, sgu_w_b, sgu_bias, l, ts)
        mix = _fnet(z, fnet_consts, mix, m_ctx // tm, seq, tm)
        *scan_in, bv, g = _rwkv_prep(z, prep_params, l, m_ctx // tp, dec_seq // tp, tp)
        yf_c, yb_c, new_state = _rwkv_scan(scan_in, batch, n_chunk_ctx, 0, l, scan_group(batch),
                                           state_shape=state_shape, state_prev=new_state)
        yf_l, yb_l = _rwkv_scan(scan_in, dec_batch, n_chunk_lat, m_ctx, l, scan_group(dec_batch), s0=state_wkv)
        mix = _rwkv_post((yf_c, yb_c), (yf_l, yb_l), bv, g, lnx_g3, lnx_b3, prep_params["e256"], mix, l,
                         m_ctx // ts, ts)
        x = _resid_matmul(x, mod, mix, w_out_b, pl.BlockSpec((None, tk, D_MODEL), lambda i, k: (l, k, 0)),
                          l, 5, 1.0, rot, tm, tk, "mix_out")
        if l < depth - 1:
            x = _ffn(x, *ffn_args, 2, 1, rot, tm, n_ctx_t + n_lat_t)
    fg = final_g.reshape(1, d)
    y_ctx = _ffn(x, *ffn_args, 2, 1, ctx_row, tm, n_ctx_t, final_g=fg)
    y_lat = _ffn(x, *ffn_args, 2, 1, lat_row, tm, n_lat_t, in_tile0=n_ctx_t, final_g=fg)
    return (y_ctx.reshape(batch, seq, d), y_lat.reshape(dec_batch, dec_seq, d), new_state)
```

```python
import functools

import numpy as np
import jax
import jax.numpy as jnp
from jax import lax
from jax.experimental import pallas as pl
from jax.experimental.pallas import tpu as pltpu

F32 = jnp.float32
BF16 = jnp.bfloat16

D_MODEL = 2048
GRID_W = 64
SGU_CHUNK = 128
A_HEADS = 4
A_WIDTH = 512
B_WIDTH = 512
B_GROUP_DIM = 128
C_WIDTH = 1024
MIX_WIDTH = A_WIDTH + B_WIDTH + C_WIDTH
HEAD_DIM = 64
N_HEADS = 16
UNIT_HEADS = 2
UNIT_W = UNIT_HEADS * HEAD_DIM
N_UNITS = N_HEADS // UNIT_HEADS
DECAY_RANK = 64
GATE_RANK = 160
C_IN = 3488
C_PAD = 3584
IN_COLS = 2 * A_WIDTH + B_WIDTH + C_IN
Z_BLOCK = 512
Z_COLS = C_PAD + 2 * A_WIDTH + B_WIDTH
D_FF = 5632
N_MOD = 9
RMS_EPS = 1e-6
LN_EPS = 1e-5
LNX_EPS = 64e-5
SCAN_T = 64
MOD_ROWS = 8

VMEM_LIMIT = 56 * 1024 * 1024
FFN_VMEM_LIMIT = 60 * 1024 * 1024


def _cparams(sem):
    return pltpu.CompilerParams(dimension_semantics=sem, vmem_limit_bytes=VMEM_LIMIT)


def _dot(a, b):
    return jnp.dot(a, b, preferred_element_type=F32)


def _split3(x):
    hi = x.astype(BF16)
    r1 = x - hi.astype(F32)
    mid = r1.astype(BF16)
    lo = (r1 - mid.astype(F32)).astype(BF16)
    return hi, mid, lo


def _dot_ones_rhs(x, e2):
    hi = x.astype(BF16)
    lo = (x - hi.astype(F32)).astype(BF16)
    return _dot(jnp.concatenate([hi, lo], axis=1), e2)


def _dot_exact_lhs(e3, x):
    return _dot(e3, jnp.concatenate(_split3(x), axis=0))


def _dot3(a, b):
    ah = a.astype(BF16)
    al = (a - ah.astype(F32)).astype(BF16)
    bh = b.astype(BF16)
    bl = (b - bh.astype(F32)).astype(BF16)
    return _dot(jnp.concatenate([ah, al, ah], axis=1), jnp.concatenate([bh, bh, bl], axis=0))


def _sigmoid(x):
    return 0.5 * jnp.tanh(0.5 * x) + 0.5


def _silu(x):
    return x * _sigmoid(x)


def _gelu_tanh(x):
    return 0.5 * x * (1.0 + jnp.tanh(0.7978845608028654 * (x + 0.044715 * (x * x * x))))


def _mod_kernel(c_ref, w_ref, b_ref, o_ref):
    s = _silu(c_ref[...]).astype(BF16)
    o_ref[...] = _dot(s, w_ref[...].astype(BF16)) + b_ref[...]


def _modulation(cond, w_mod, b_mod):
    depth = w_mod.shape[0]
    n = w_mod.shape[2]
    tn = 1024
    return pl.pallas_call(
        _mod_kernel,
        grid=(depth, n // tn),
        in_specs=[
            pl.BlockSpec((MOD_ROWS, D_MODEL), lambda l, j: (0, 0)),
            pl.BlockSpec((None, D_MODEL, tn), lambda l, j: (l, 0, j)),
            pl.BlockSpec((None, 1, tn), lambda l, j: (l, 0, j)),
        ],
        out_specs=pl.BlockSpec((None, MOD_ROWS, tn), lambda l, j: (l, 0, j)),
        out_shape=jax.ShapeDtypeStruct((depth, MOD_ROWS, n), F32),
        compiler_params=_cparams(("arbitrary", "arbitrary")),
        name="modulation",
    )(cond, w_mod, b_mod.reshape(depth, 1, n))


def _mod_spec(layer, slot, row_of_tile):
    return pl.BlockSpec((None, None, None, 1, D_MODEL),
                        lambda i, *_: (layer, row_of_tile(i), slot, 0, 0))


def _norm_spec(layer, slot):
    return pl.BlockSpec((None, None, 1, D_MODEL), lambda i, *_: (layer, slot, 0, 0))


def _modulated_norm(x_ref, g_ref, sc_ref, sh_ref):
    x = x_ref[...]
    rs = lax.rsqrt(jnp.mean(x * x, axis=-1, keepdims=True) + RMS_EPS)
    gain = g_ref[...] * (1.0 + sc_ref[...])
    return ((x * rs) * gain + sh_ref[...]).astype(BF16)


def _in_proj_kernel(x_ref, sh_ref, sc_ref, g_ref, wt_ref, o_ref, h_ref, *, valid_cols):
    j = pl.program_id(1)

    def step(first):
        if first:
            h = _modulated_norm(x_ref, g_ref, sc_ref, sh_ref)
            h_ref[...] = h
        else:
            h = h_ref[...]
        out = lax.dot_general(h, wt_ref[...].astype(BF16), (((1,), (1,)), ((), ())),
                              preferred_element_type=F32)
        tn = out.shape[1]
        if valid_cols % tn:
            col = j * tn + lax.broadcasted_iota(jnp.int32, (1, tn), 1)
            out = jnp.where(col < valid_cols, out, 0.0)
        o_ref[...] = out

    pl.when(j == 0)(lambda: step(True))
    pl.when(j > 0)(lambda: step(False))


def _in_proj(x, mod, norm_g, w_in_t, layer, row_of_tile, tm):
    m = x.shape[0]
    nb = Z_COLS // Z_BLOCK
    shift = C_PAD // Z_BLOCK
    assert (2 * A_WIDTH + B_WIDTH) % Z_BLOCK == 0 and -(-IN_COLS // Z_BLOCK) == nb
    return pl.pallas_call(
        functools.partial(_in_proj_kernel, valid_cols=IN_COLS),
        grid=(m // tm, nb),
        in_specs=[
            pl.BlockSpec((tm, D_MODEL), lambda i, j: (i, 0)),
            _mod_spec(layer, 3, row_of_tile),
            _mod_spec(layer, 4, row_of_tile),
            _norm_spec(layer, 1),
            pl.BlockSpec((None, Z_BLOCK, D_MODEL), lambda i, j: (layer, j, 0)),
        ],
        out_specs=pl.BlockSpec((tm, Z_BLOCK), lambda i, j: (i, (j + shift) % nb)),
        out_shape=jax.ShapeDtypeStruct((m, Z_COLS), F32),
        scratch_shapes=[pltpu.VMEM((tm, D_MODEL), BF16)],
        compiler_params=_cparams(("arbitrary", "arbitrary")),
        name="in_proj",
    )(x, mod, mod, norm_g, w_in_t)


def _ffn_kernel(x_ref, sh_ref, sc_ref, gt_ref, g_ref, wg_ref, wu_ref, wo_ref, *rest, tn, final_norm):
    fg_ref = rest[0] if final_norm else None
    o_ref, h_ref = rest[-2:]
    j = pl.program_id(1)
    last = pl.num_programs(1) - 1

    def step(first, final):
        if first:
            h = _modulated_norm(x_ref, g_ref, sc_ref, sh_ref)
            h_ref[...] = h
        else:
            h = h_ref[...]
        gate = _dot(h, wg_ref[...].astype(BF16))
        up = _dot(h, wu_ref[...].astype(BF16))
        a = (_silu(gate) * up).astype(BF16)
        for n in range(o_ref.shape[1] // tn):
            cols = slice(n * tn, (n + 1) * tn)
            acc = _dot(a, wo_ref[:, cols].astype(BF16))
            if not first:
                acc = o_ref[:, cols] + acc
            if final:
                acc = x_ref[:, cols] + (0.5 * gt_ref[:, cols]) * acc
            o_ref[:, cols] = acc
        if final and final_norm:
            y = o_ref[...]
            rs = lax.rsqrt(jnp.mean(y * y, axis=-1, keepdims=True) + RMS_EPS)
            o_ref[...] = y * rs * fg_ref[...]

    pl.when(j == 0)(lambda: step(True, False))
    pl.when(jnp.logical_and(j > 0, j < last))(lambda: step(False, False))
    pl.when(j == last)(lambda: step(False, True))


def _ffn(x, mod, norm_g, w_in, w_out, layer, sub, ffn_idx, row_of_tile, tm, n_tiles, in_tile0=0, out_tile0=0,
         out_rows=None, out_prev=None, final_g=None, tf=256):
    nf = D_FF // tf
    out_rows = n_tiles * tm if out_rows is None else out_rows
    args = [x, mod, mod, mod, norm_g, w_in, w_in, w_out]
    in_specs = [
        pl.BlockSpec((tm, D_MODEL), lambda i, j: (in_tile0 + i, 0)),
        _mod_spec(layer, 3 * sub, row_of_tile),
        _mod_spec(layer, 3 * sub + 1, row_of_tile),
        _mod_spec(layer, 3 * sub + 2, row_of_tile),
        _norm_spec(layer, sub),
        pl.BlockSpec((None, None, D_MODEL, tf), lambda i, j: (layer, ffn_idx, 0, j)),
        pl.BlockSpec((None, None, D_MODEL, tf), lambda i, j: (layer, ffn_idx, 0, nf + j)),
        pl.BlockSpec((None, None, tf, D_MODEL), lambda i, j: (layer, ffn_idx, j, 0)),
    ]
    if final_g is not None:
        args.append(final_g)
        in_specs.append(pl.BlockSpec((1, D_MODEL), lambda i, j: (0, 0)))
    aliases = {}
    if out_prev is not None:
        aliases = {len(args): 0}
        args.append(out_prev)
        in_specs.append(pl.BlockSpec(memory_space=pl.ANY))
    return pl.pallas_call(
        functools.partial(_ffn_kernel, tn=512, final_norm=final_g is not None),
        grid=(n_tiles, nf),
        in_specs=in_specs,
        out_specs=pl.BlockSpec((tm, D_MODEL), lambda i, j: (out_tile0 + i, 0)),
        out_shape=jax.ShapeDtypeStruct((out_rows, D_MODEL), F32),
        input_output_aliases=aliases,
        scratch_shapes=[pltpu.VMEM((tm, D_MODEL), BF16)],
        compiler_params=pltpu.CompilerParams(dimension_semantics=("arbitrary", "arbitrary"),
                                             vmem_limit_bytes=FFN_VMEM_LIMIT),
        name="ffn",
    )(*args)


def _resid_matmul_kernel(x_ref, gt_ref, a_ref, w_ref, o_ref, *, coef, tn):
    k = pl.program_id(1)
    last = pl.num_programs(1) - 1

    def step(first, final):
        a = a_ref[...]
        for n in range(o_ref.shape[1] // tn):
            cols = slice(n * tn, (n + 1) * tn)
            acc = _dot(a, w_ref[:, cols].astype(BF16))
            if not first:
                acc = o_ref[:, cols] + acc
            if final:
                acc = x_ref[:, cols] + (coef * gt_ref[:, cols]) * acc
            o_ref[:, cols] = acc

    pl.when(k == 0)(lambda: step(True, False))
    pl.when(jnp.logical_and(k > 0, k < last))(lambda: step(False, False))
    pl.when(k == last)(lambda: step(False, True))


def _resid_matmul(x, mod, a, w, w_spec, layer, gate_slot, coef, row_of_tile, tm, tk, name):
    m = x.shape[0]
    return pl.pallas_call(
        functools.partial(_resid_matmul_kernel, coef=coef, tn=512),
        grid=(m // tm, a.shape[1] // tk),
        in_specs=[
            pl.BlockSpec((tm, D_MODEL), lambda i, k: (i, 0)),
            _mod_spec(layer, gate_slot, row_of_tile),
            pl.BlockSpec((tm, tk), lambda i, k: (i, k)),
            w_spec,
        ],
        out_specs=pl.BlockSpec((tm, D_MODEL), lambda i, k: (i, 0)),
        out_shape=jax.ShapeDtypeStruct((m, D_MODEL), F32),
        compiler_params=_cparams(("arbitrary", "arbitrary")),
        name=name,
    )(x, mod, a, w)


def _sgu_kernel(zu_ref, zv_ref, lg_ref, lb_ref, w_ref, bias_ref, o_ref):
    u = _gelu_tanh(zu_ref[...])
    v = _gelu_tanh(zv_ref[...])
    mu = jnp.mean(v, axis=-1, keepdims=True)
    vc = v - mu
    var = jnp.mean(vc * vc, axis=-1, keepdims=True)
    vn = (vc * lax.rsqrt(var + LN_EPS) * lg_ref[...] + lb_ref[...]).astype(BF16)
    hd = A_WIDTH // A_HEADS
    for c in range(u.shape[0] // SGU_CHUNK):
        rows = slice(c * SGU_CHUNK, (c + 1) * SGU_CHUNK)
        for h in range(A_HEADS):
            cols = slice(h * hd, (h + 1) * hd)
            mixed = _dot(w_ref[h], vn[rows, cols]) + bias_ref[:, cols]
            o_ref[rows, cols] = (u[rows, cols] * mixed).astype(BF16)


def _sgu(z, ln_g, ln_b, w_s, bias, layer, tm):
    m = z.shape[0]
    cu = C_PAD // A_WIDTH
    return pl.pallas_call(
        _sgu_kernel,
        grid=(m // tm,),
        in_specs=[
            pl.BlockSpec((tm, A_WIDTH), lambda i: (i, cu)),
            pl.BlockSpec((tm, A_WIDTH), lambda i: (i, cu + 1)),
            pl.BlockSpec((None, 1, A_WIDTH), lambda i: (layer, 0, 0)),
            pl.BlockSpec((None, 1, A_WIDTH), lambda i: (layer, 0, 0)),
            pl.BlockSpec((None, A_HEADS, SGU_CHUNK, SGU_CHUNK), lambda i: (layer, 0, 0, 0)),
            pl.BlockSpec((None, SGU_CHUNK, A_WIDTH), lambda i: (layer, 0, 0)),
        ],
        out_specs=pl.BlockSpec((tm, A_WIDTH), lambda i: (i, 0)),
        out_shape=jax.ShapeDtypeStruct((m, MIX_WIDTH), BF16),
        compiler_params=_cparams(("arbitrary",)),
        name="sgu",
    )(z, z, ln_g, ln_b, w_s, bias)


def _dft_mats(n):
    idx = np.arange(n)
    ang = 2.0 * np.pi * ((idx[:, None] * idx[None, :]) % n) / n
    return np.cos(ang) / np.sqrt(n), np.sin(ang) / np.sqrt(n)


def _hi_lo(a):
    a32 = jnp.asarray(a, F32)
    hi = a32.astype(BF16)
    lo = (a32 - hi.astype(F32)).astype(BF16)
    return hi, lo


def _fnet_kernel(z_ref, fdh_ref, fdl_ref, fsh_ref, fsl_ref, flh_ref, fll_ref, mix_ref, o_ref, t_ref, *,
                 n_ctx_tiles, seq):
    del mix_ref
    tm = z_ref.shape[0]
    x = z_ref[...]
    xh = x.astype(BF16)
    xl = (x - xh.astype(F32)).astype(BF16)
    gd = B_GROUP_DIM
    for g in range(B_WIDTH // gd):
        cols = slice(g * gd, (g + 1) * gd)
        t = _dot(xh[:, cols], fdh_ref[...]) + _dot(xl[:, cols], fdh_ref[...]) + _dot(xh[:, cols], fdl_ref[...])
        t_ref[0:tm, cols] = t[:, 0:gd]
        t_ref[tm:2 * tm, cols] = t[:, gd:2 * gd]

    def position_dft(fh_ref, fl_ref, rows_in, rows_out):
        tc = t_ref[rows_in[0], :]
        ts = t_ref[rows_in[1], :]
        tt = jnp.concatenate([tc, ts], axis=0)
        th = tt.astype(BF16)
        tl = (tt - th.astype(F32)).astype(BF16)
        out = _dot(fh_ref[...], th) + _dot(fl_ref[...], th) + _dot(fh_ref[...], tl)
        o_ref[rows_out, :] = out.astype(BF16)

    is_ctx = pl.program_id(0) < n_ctx_tiles

    @pl.when(is_ctx)
    def _():
        for s in range(tm // seq):
            r0 = slice(s * seq, (s + 1) * seq)
            r1 = slice(tm + s * seq, tm + (s + 1) * seq)
            position_dft(fsh_ref, fsl_ref, (r0, r1), r0)

    @pl.when(jnp.logical_not(is_ctx))
    def _():
        position_dft(flh_ref, fll_ref, (slice(0, tm), slice(tm, 2 * tm)), slice(0, tm))


def _fnet(z, consts, mix, n_ctx_tiles, seq, tm):
    m = z.shape[0]
    cb = (C_PAD + 2 * A_WIDTH) // B_WIDTH
    full = lambda a: pl.BlockSpec(a.shape, lambda i: (0,) * a.ndim)
    return pl.pallas_call(
        functools.partial(_fnet_kernel, n_ctx_tiles=n_ctx_tiles, seq=seq),
        grid=(m // tm,),
        in_specs=([pl.BlockSpec((tm, B_WIDTH), lambda i: (i, cb))] + [full(a) for a in consts]
                  + [pl.BlockSpec(memory_space=pl.ANY)]),
        out_specs=pl.BlockSpec((tm, B_WIDTH), lambda i: (i, A_WIDTH // B_WIDTH)),
        out_shape=jax.ShapeDtypeStruct(mix.shape, mix.dtype),
        input_output_aliases={1 + len(consts): 0},
        scratch_shapes=[pltpu.VMEM((2 * tm, B_WIDTH), F32)],
        compiler_params=_cparams(("arbitrary",)),
        name="fnet",
    )(z, *consts, mix)


def _seg_sum(x, e_ref):
    w = e_ref.shape[1]
    parts = [_dot_ones_rhs(x[:, b * w:(b + 1) * w], e_ref[...]) for b in range(x.shape[1] // w)]
    return parts[0] if len(parts) == 1 else jnp.concatenate(parts, axis=1)


def _token_shift_mix(z_ref, zc_ref, zp_ref, zn_ref, mu_ref, bounds, period, up_ok, dn_ok):
    tm = zc_ref.shape[0]
    row = lax.broadcasted_iota(jnp.int32, (tm, 1), 0)
    pos = row & (period - 1)
    lanes = 128

    def neighbour(kind, cols):
        xs = zc_ref[:, cols]
        if kind == 0:
            return jnp.where(pos == 0, 0.0, pltpu.roll(xs, 1, 0))
        if kind == 1:
            return jnp.where(pos == period - 1, 0.0, pltpu.roll(xs, tm - 1, 0))
        if kind == 2:
            return jnp.concatenate([jnp.where(up_ok, zp_ref[:, cols], 0.0), xs[:tm - GRID_W]], axis=0)
        return jnp.concatenate([xs[GRID_W:], jnp.where(dn_ok, zn_ref[:, cols], 0.0)], axis=0)

    edges = sorted({0, C_PAD} | {b // lanes * lanes for b in bounds} | {-(-b // lanes) * lanes for b in bounds})
    for lo, hi in zip(edges[:-1], edges[1:]):
        cols = slice(lo, hi)
        kinds = [n for n in range(len(bounds) + 1)
                 if (bounds[n - 1] if n else 0) < hi and lo < (bounds[n] if n < len(bounds) else C_PAD)]
        zs = neighbour(kinds[-1], cols)
        lane = lo + lax.broadcasted_iota(jnp.int32, (1, hi - lo), 1)
        for n in reversed(kinds[:-1]):
            zs = jnp.where(lane < bounds[n], neighbour(n, cols), zs)
        xs = zc_ref[:, cols]
        z_ref[:, cols] = xs + (zs - xs) * mu_ref[:, cols]


def _prep_kernel(zc_ref, zp_ref, zn_ref, mu_ref, w0_ref, w2_ref, a0_ref, a2_ref, kk_ref, ka_ref, rk_ref,
                 g2_ref, e_ref,
                 rvk_o, kf_o, kb_o, ld_o, bv_o, g_o, z_ref, *,
                 n_ctx_tiles, tiles_per_lat):
    i = pl.program_id(0)
    tm = zc_ref.shape[0]
    is_ctx = i < n_ctx_tiles
    q = (i - n_ctx_tiles) % tiles_per_lat
    half = C_IN // 2
    quarter = C_IN // 4

    @pl.when(is_ctx)
    def _():
        _token_shift_mix(z_ref, zc_ref, zp_ref, zn_ref, mu_ref, (half,), tm, False, False)

    @pl.when(jnp.logical_not(is_ctx))
    def _():
        _token_shift_mix(z_ref, zc_ref, zp_ref, zn_ref, mu_ref, (quarter, 2 * quarter, 3 * quarter), GRID_W,
                         q > 0, q < tiles_per_lat - 1)

    cw = C_WIDTH
    r = z_ref[:, 0:cw]
    k = z_ref[:, cw:2 * cw]
    v = z_ref[:, 2 * cw:3 * cw]
    wd = z_ref[:, 3 * cw:3 * cw + 128]
    ad = z_ref[:, 3 * cw + 128:3 * cw + 256]
    gd = z_ref[:, 3 * cw + 256:C_PAD]

    uw = _dot3(jnp.tanh(wd), w2_ref[...])
    ua = _dot3(ad, a2_ref[...])
    kk_raw = k * kk_ref[...]
    ss = _seg_sum(kk_raw * kk_raw, e_ref)
    kk = kk_raw * lax.rsqrt(jnp.maximum(ss, 1e-24))
    rvk_o[:, 0:cw] = r.astype(BF16)
    rvk_o[:, cw:2 * cw] = v.astype(BF16)
    rvk_o[:, 2 * cw:3 * cw] = kk.astype(BF16)
    km_sum = None
    for d, kd_o in enumerate((kf_o, kb_o)):
        u = uw[:, d * cw:(d + 1) * cw] + w0_ref[d:d + 1, :]
        ld_o[:, d * cw:(d + 1) * cw] = -float(np.exp(-0.5)) * _sigmoid(u)
        a = _sigmoid(ua[:, d * cw:(d + 1) * cw] + a0_ref[d:d + 1, :])
        km = k * (1.0 + (a - 1.0) * ka_ref[...])
        kd_o[:, 0:cw] = km.astype(BF16)
        kd_o[:, cw:2 * cw] = (kk * a).astype(BF16)
        km_sum = km if km_sum is None else km_sum + km
    bonus = _seg_sum(r * km_sum * rk_ref[...], e_ref)
    bv_o[...] = bonus * v
    g_o[...] = _dot(_sigmoid(gd).astype(BF16), g2_ref[...]).astype(BF16)


def _rwkv_prep(z, p, layer, n_ctx_tiles, tiles_per_lat, tm):
    m = z.shape[0]
    hb = tm // GRID_W
    n_halo = m // GRID_W
    lay = lambda *shape: pl.BlockSpec((None,) + shape, lambda i: (layer,) + (0,) * len(shape))
    outs = [(3, BF16), (2, BF16), (2, BF16), (2, F32), (1, F32), (1, BF16)]
    return pl.pallas_call(
        functools.partial(_prep_kernel, n_ctx_tiles=n_ctx_tiles, tiles_per_lat=tiles_per_lat),
        grid=(m // tm,),
        in_specs=[
            pl.BlockSpec((tm, C_PAD), lambda i: (i, 0)),
            pl.BlockSpec((GRID_W, C_PAD), lambda i: (jnp.maximum(i * hb - 1, 0), 0)),
            pl.BlockSpec((GRID_W, C_PAD), lambda i: (jnp.minimum(i * hb + hb, n_halo - 1), 0)),
            lay(1, C_PAD),
            lay(2, C_WIDTH),
            lay(128, 2 * C_WIDTH),
            lay(2, C_WIDTH),
            lay(128, 2 * C_WIDTH),
            lay(1, C_WIDTH),
            lay(1, C_WIDTH),
            lay(1, C_WIDTH),
            lay(256, C_WIDTH),
            pl.BlockSpec((512, 256), lambda i: (0, 0)),
        ],
        out_specs=[pl.BlockSpec((tm, n * C_WIDTH), lambda i: (i, 0)) for n, _ in outs],
        out_shape=[jax.ShapeDtypeStruct((m, n * C_WIDTH), dt) for n, dt in outs],
        scratch_shapes=[pltpu.VMEM((tm, C_PAD), F32)],
        compiler_params=_cparams(("arbitrary",)),
        name="rwkv_prep",
    )(z, z, z, p["mu"], p["w0"], p["w2"], p["a0"], p["a2"], p["k_k"], p["k_a"], p["r_k"], p["g2"],
      p["e256"])


def _bmm(a, b):
    return jnp.einsum("uik,ukj->uij", a.astype(BF16), b.astype(BF16), preferred_element_type=F32)


def _bmm_nt(a, b):
    return jnp.einsum("uik,ujk->uij", a.astype(BF16), b.astype(BF16), preferred_element_type=F32)


def _block_diag(y, bd_mask):
    return jnp.where(bd_mask, jnp.concatenate([y] * UNIT_HEADS, axis=1), jnp.zeros((), y.dtype))


def _to_units(x):
    return jnp.stack([x[:, p * UNIT_W:(p + 1) * UNIT_W] for p in range(N_UNITS)], axis=0)


def _scan_chunks(streams, hs):
    t = SCAN_T
    w = UNIT_W
    ti = lax.broadcasted_iota(jnp.int32, (t, w), 0)
    si = lax.broadcasted_iota(jnp.int32, (t, w), 1) & (HEAD_DIM - 1)
    eye = jnp.where(si == ti, 1.0, 0.0)
    t_row = lax.broadcasted_iota(jnp.int32, (t, 3 * t), 0)
    t_col = lax.broadcasted_iota(jnp.int32, (t, 3 * t), 1) & (t - 1)
    bd_mask = (lax.broadcasted_iota(jnp.int32, (w, w), 0) // HEAD_DIM
               == lax.broadcasted_iota(jnp.int32, (w, w), 1) // HEAD_DIM)

    q, ai, ki, vb, vf, lhs, p_tot = [], [], [], [], [], [], []
    for reverse, r, v, kk, ld, km, ka in streams:
        tri = jnp.where((t_col >= t_row) if reverse else (t_col <= t_row), 1.0, 0.0).astype(BF16)
        cum = _dot_exact_lhs(tri, ld)
        c_end = cum[0:1, :] if reverse else cum[t - 1:t, :]
        p_end = jnp.exp(c_end - cum)
        p_inv = jnp.exp(-cum)
        q.append(_to_units(jnp.concatenate([kk * jnp.exp(cum - ld), r * jnp.exp(cum)], axis=0).astype(BF16)))
        ai.append(_to_units((ka * p_inv).astype(BF16)))
        ki.append(_to_units((km * p_inv).astype(BF16)))
        vb.append(_to_units(v.astype(BF16)))
        vf.append(_to_units(v))
        lhs.append(_to_units(jnp.concatenate([ka * p_end, km * p_end], axis=0).astype(BF16)))
        p_tot.append(_to_units(jnp.exp(c_end)))
    q, ai, ki, vb, vf, lhs, p_tot = (jnp.concatenate(a, axis=0) for a in (q, ai, ki, vb, vf, lhs, p_tot))

    def masked(x, strictly):
        parts = []
        for n, stream in enumerate(streams):
            if stream[0]:
                keep = si > ti if strictly else si >= ti
            else:
                keep = si < ti if strictly else si <= ti
            parts.append(jnp.where(keep, x[n * N_UNITS:(n + 1) * N_UNITS], 0.0))
        return jnp.concatenate(parts, axis=0)

    rr = jnp.concatenate([_block_diag(ai, bd_mask), _block_diag(ki, bd_mask)], axis=1)
    sc = _bmm_nt(q, rr)
    la = masked(sc[:, 0:t, 0:w], True)
    lk = masked(sc[:, 0:t, w:2 * w], True)
    ma = masked(sc[:, t:2 * t, 0:w], False)
    mk = masked(sc[:, t:2 * t, w:2 * w], False)

    n_pow = -la
    x_inv = eye + n_pow
    n_pow = _bmm(n_pow, _block_diag(n_pow.astype(BF16), bd_mask))
    for _ in range(int(np.log2(t)) - 2):
        both = _bmm(jnp.concatenate([n_pow, x_inv], axis=1), _block_diag(n_pow.astype(BF16), bd_mask))
        x_inv = x_inv + both[:, t:2 * t]
        n_pow = both[:, 0:t]
    x_inv = x_inv + _bmm(x_inv, _block_diag(n_pow.astype(BF16), bd_mask))

    hb = _bmm_nt(q, hs)
    lv = _bmm(jnp.concatenate([lk, mk], axis=1), _block_diag(vb, bd_mask))
    u = -_bmm(x_inv, _block_diag((hb[:, 0:t] + lv[:, 0:t]).astype(BF16), bd_mask))
    y = hb[:, t:2 * t] + lv[:, t:2 * t] + _bmm(ma, _block_diag(u.astype(BF16), bd_mask))
    rhs_t = jnp.swapaxes(jnp.concatenate([u, vf], axis=1), 1, 2)
    upd = _bmm(rhs_t, lhs)
    hs_new = p_tot * hs + jnp.where(bd_mask, upd, 0.0)
    return y, hs_new


def _scan_kernel(*refs, has_s0, emit_state, aliased_state):
    ins = refs[:6]
    pos = 6
    s0_ref = None
    if has_s0:
        s0_ref = refs[pos]
        pos += 1
    if aliased_state:
        pos += 1
    y_refs = refs[pos:pos + 2]
    pos += 2
    st_ref = None
    if emit_state:
        st_ref = refs[pos]
        pos += 1
    h_ref = refs[pos]
    c = pl.program_id(1)
    hd = HEAD_DIM

    n_group = h_ref.shape[0]

    @pl.when(c == 0)
    def _():
        if has_s0:
            zero = jnp.zeros((hd, hd), F32)
            for s in range(n_group):
                for d in range(2):
                    for p in range(N_UNITS):
                        rows = [jnp.concatenate([s0_ref[s, d, UNIT_HEADS * p + a] if a == b else zero
                                                 for b in range(UNIT_HEADS)], axis=1) for a in range(UNIT_HEADS)]
                        h_ref[s, d, p] = jnp.concatenate(rows, axis=0)
        else:
            h_ref[...] = jnp.zeros_like(h_ref)

    streams = []
    for s in range(n_group):
        for d in range(2):
            rvk_ref, kd_ref, ld_ref = ins[3 * d:3 * d + 3]
            r, v, kk = (rvk_ref[s, :, n * C_WIDTH:(n + 1) * C_WIDTH].astype(F32) for n in range(3))
            km, ka = (kd_ref[s, :, n * C_WIDTH:(n + 1) * C_WIDTH].astype(F32) for n in range(2))
            streams.append((d == 1, r, v, kk, ld_ref[s], km, ka))
    n_all = len(streams) * N_UNITS
    y, hs_new = _scan_chunks(streams, h_ref[...].reshape(n_all, UNIT_W, UNIT_W))
    h_ref[...] = hs_new.reshape(h_ref.shape)
    for s in range(n_group):
        for d, y_ref in enumerate(y_refs):
            for p in range(N_UNITS):
                y_ref[s, :, p * UNIT_W:(p + 1) * UNIT_W] = y[(2 * s + d) * N_UNITS + p]

    if emit_state:
        @pl.when(c == pl.num_programs(1) - 1)
        def _():
            for s in range(n_group):
                for d in range(2):
                    for p in range(N_UNITS):
                        hs = h_ref[s, d, p]
                        for a in range(UNIT_HEADS):
                            st_ref[s, d, UNIT_HEADS * p + a] = hs[a * hd:(a + 1) * hd, a * hd:(a + 1) * hd]


def _rwkv_scan(arrs, n_seq, n_chunk, row0, layer, group, s0=None, state_shape=None, state_prev=None):
    seq_len = n_chunk * SCAN_T
    m_all = arrs[0].shape[0]
    assert row0 % (seq_len * group) == 0 and m_all % seq_len == 0 and n_seq % group == 0
    rvk, kf, kb, ld = (a.reshape(m_all // seq_len, seq_len, a.shape[1]) for a in arrs)
    g0 = row0 // (seq_len * group)
    fwd = lambda b, c: (b, c, 0)
    bwd = lambda b, c: (b, n_chunk - 1 - c, 0)
    blk = lambda imap, n=1: pl.BlockSpec((group, SCAN_T, n * C_WIDTH), imap)
    in_specs = [blk(lambda b, c: (g0 + b, c, 0), 3), blk(lambda b, c: (g0 + b, c, 0), 2),
                blk(lambda b, c: (g0 + b, c, 0)),
                blk(lambda b, c: (g0 + b, n_chunk - 1 - c, 0), 3), blk(lambda b, c: (g0 + b, n_chunk - 1 - c, 0), 2),
                blk(lambda b, c: (g0 + b, n_chunk - 1 - c, 1))]
    args = [rvk, kf, ld, rvk, kb, ld]
    st_spec = pl.BlockSpec((group, None, 2, N_HEADS, HEAD_DIM, HEAD_DIM), lambda b, c: (b, layer, 0, 0, 0, 0))
    if s0 is not None:
        in_specs.append(st_spec)
        args.append(s0)
    aliases = {}
    if state_prev is not None:
        aliases = {len(args): 2}
        in_specs.append(pl.BlockSpec(memory_space=pl.ANY))
        args.append(state_prev)
    y_shape = jax.ShapeDtypeStruct((n_seq, seq_len, C_WIDTH), F32)
    out_specs = [blk(fwd), blk(bwd)]
    out_shape = [y_shape, y_shape]
    if state_shape is not None:
        out_specs.append(st_spec)
        out_shape.append(jax.ShapeDtypeStruct(state_shape, F32))
    outs = pl.pallas_call(
        functools.partial(_scan_kernel, has_s0=s0 is not None, emit_state=state_shape is not None,
                          aliased_state=state_prev is not None),
        grid=(n_seq // group, n_chunk),
        in_specs=in_specs,
        out_specs=out_specs,
        out_shape=out_shape,
        input_output_aliases=aliases,
        scratch_shapes=[pltpu.VMEM((group, 2, N_UNITS, UNIT_W, UNIT_W), F32)],
        compiler_params=_cparams(("arbitrary", "arbitrary")),
        name="rwkv_scan",
    )(*args)
    ys = [y.reshape(n_seq * seq_len, C_WIDTH) for y in outs[:2]]
    return ys + list(outs[2:])


def _post_kernel(yf1_ref, yb1_ref, yf2_ref, yb2_ref, bv_ref, g_ref, lg_ref, lb_ref, e_ref, mix_ref, o_ref, *,
                 n_ctx_tiles):
    del mix_ref
    is_ctx = pl.program_id(0) < n_ctx_tiles
    y = jnp.where(is_ctx, yf1_ref[...] + yb1_ref[...], yf2_ref[...] + yb2_ref[...])
    inv = 1.0 / HEAD_DIM
    m = _seg_sum(y, e_ref) * inv
    yc = y - m
    var = _seg_sum(yc * yc, e_ref) * inv
    yn = yc * lax.rsqrt(var + LNX_EPS) * lg_ref[...] + lb_ref[...]
    o_ref[...] = ((yn + bv_ref[...]) * g_ref[...]).astype(BF16)


def _rwkv_post(y_ctx, y_lat, bv, g, lnx_g, lnx_b, e256, mix, layer, n_ctx_tiles, tm):
    m = bv.shape[0]
    n_tiles = m // tm
    row = pl.BlockSpec((tm, C_WIDTH), lambda i: (i, 0))
    row_ctx = pl.BlockSpec((tm, C_WIDTH), lambda i: (jnp.minimum(i, n_ctx_tiles - 1), 0))
    row_lat = pl.BlockSpec((tm, C_WIDTH), lambda i: (jnp.maximum(i - n_ctx_tiles, 0), 0))
    vec = pl.BlockSpec((None, 1, C_WIDTH), lambda i: (layer, 0, 0))
    return pl.pallas_call(
        functools.partial(_post_kernel, n_ctx_tiles=n_ctx_tiles),
        grid=(n_tiles,),
        in_specs=[row_ctx, row_ctx, row_lat, row_lat, row, row, vec, vec,
                  pl.BlockSpec((512, 256), lambda i: (0, 0)), pl.BlockSpec(memory_space=pl.ANY)],
        out_specs=pl.BlockSpec((tm, C_WIDTH), lambda i: (i, (A_WIDTH + B_WIDTH) // C_WIDTH)),
        out_shape=jax.ShapeDtypeStruct(mix.shape, mix.dtype),
        input_output_aliases={9: 0},
        compiler_params=_cparams(("arbitrary",)),
        name="rwkv_post",
    )(y_ctx[0], y_ctx[1], y_lat[0], y_lat[1], bv, g, lnx_g, lnx_b, e256, mix)


def kernel(x_prompt, x_sample, state_wkv, c, c_ctx, norm_g, w_mod, b_mod, ffn_w_in, ffn_w_out, w_in, w_out,
           sgu_ln_g, sgu_ln_b, sgu_w, sgu_b, shift_mu, decay_w0, decay_w2, iclr_a0, iclr_a2, k_k, k_a, r_k,
           gate_w2, lnx_g, lnx_b, final_g):
    batch, seq, d = x_prompt.shape
    dec_batch, dec_seq, _ = x_sample.shape
    depth = w_mod.shape[0]
    assert d == D_MODEL and dec_batch + 1 <= MOD_ROWS
    m_ctx = batch * seq
    tm = 1024
    ts = 512
    tp = 256
    assert seq == tp and dec_seq == tm and m_ctx % tm == 0 and dec_seq % GRID_W == 0

    def row_of_tile_fn(rows):
        n_ctx = m_ctx // rows
        per_lat = dec_seq // rows
        return lambda i: jnp.where(i < n_ctx, 0, 1 + (i - n_ctx) // per_lat)

    zeros_cw = jnp.zeros((depth, DECAY_RANK, C_WIDTH), F32)

    def both_dirs(w):
        top = jnp.concatenate([w[:, 0], zeros_cw], axis=-1)
        bot = jnp.concatenate([zeros_cw, w[:, 1]], axis=-1)
        return jnp.concatenate([top, bot], axis=1)

    e_np = (np.arange(256)[:, None] // HEAD_DIM == np.arange(256)[None, :] // HEAD_DIM)
    prep_params = {
        "mu": jnp.pad(shift_mu, ((0, 0), (0, C_PAD - C_IN))).reshape(depth, 1, C_PAD),
        "w0": decay_w0, "w2": both_dirs(decay_w2), "a0": iclr_a0, "a2": both_dirs(iclr_a2),
        "k_k": k_k.reshape(depth, 1, C_WIDTH), "k_a": k_a.reshape(depth, 1, C_WIDTH),
        "r_k": r_k.reshape(depth, 1, C_WIDTH),
        "g2": jnp.pad(gate_w2, ((0, 0), (0, 256 - GATE_RANK), (0, 0))).astype(BF16),
        "e256": jnp.asarray(np.concatenate([e_np, e_np], axis=0), BF16),
    }
    cd, sd = _dft_mats(B_GROUP_DIM)
    cs, ss = _dft_mats(seq)
    cl, sl = _dft_mats(dec_seq)
    fnet_consts = (_hi_lo(np.concatenate([cd, sd], axis=1)) + _hi_lo(np.concatenate([cs, -ss], axis=1))
                   + _hi_lo(np.concatenate([cl, -sl], axis=1)))
    sgu_bias = jnp.repeat(jnp.swapaxes(sgu_b, 1, 2), A_WIDTH // A_HEADS, axis=2)
    sgu_w_b = sgu_w.astype(BF16)
    ln_g = sgu_ln_g.reshape(depth, 1, A_WIDTH)
    ln_b = sgu_ln_b.reshape(depth, 1, A_WIDTH)
    lnx_g3 = lnx_g.reshape(depth, 1, C_WIDTH)
    lnx_b3 = lnx_b.reshape(depth, 1, C_WIDTH)
    norm_g4 = norm_g.reshape(depth, 3, 1, d)

    cond = jnp.concatenate([c_ctx[None, :], c, jnp.zeros((MOD_ROWS - 1 - dec_batch, d), F32)], axis=0)
    mod = _modulation(cond, w_mod, b_mod).reshape(depth, MOD_ROWS, N_MOD, 1, d)

    m = m_ctx + dec_batch * dec_seq
    n_ctx_t = m_ctx // tm
    n_lat_t = dec_batch * dec_seq // tm
    rot = row_of_tile_fn(tm)
    ctx_row = lambda i: 0
    lat_row = lambda i: 1 + i
    n_chunk_ctx = seq // SCAN_T
    n_chunk_lat = dec_seq // SCAN_T
    state_shape = (batch, depth, 2, N_HEADS, HEAD_DIM, HEAD_DIM)
    tk = 512
    scan_group = lambda n_seq: 4 if n_seq % 4 == 0 else 1
    w_in_t = jnp.swapaxes(w_in, 1, 2).astype(BF16)
    w_out_b = w_out.astype(BF16)
    new_state = None
    x = None
    for l in range(depth):
        ffn_args = (mod, norm_g4, ffn_w_in, ffn_w_out, l)
        if l == 0:
            x = _ffn(x_prompt.reshape(m_ctx, d), *ffn_args, 0, 0, ctx_row, tm, n_ctx_t, out_rows=m)
            x = _ffn(x_sample.reshape(m - m_ctx, d), *ffn_args, 0, 0, lat_row, tm, n_lat_t, out_tile0=n_ctx_t,
                     out_rows=m, out_prev=x)
        else:
            x = _ffn(x, *ffn_args, 0, 0, rot, tm, n_ctx_t + n_lat_t)
        z = _in_proj(x, mod, norm_g4, w_in_t, l, rot, tm)
        mix = _sgu(z, ln_g, ln_b, sgu_w_b, sgu_bias, l, ts)
        mix = _fnet(z, fnet_consts, mix, m_ctx // tm, seq, tm)
        *scan_in, bv, g = _rwkv_prep(z, prep_params, l, m_ctx // tp, dec_seq // tp, tp)
        yf_c, yb_c, new_state = _rwkv_scan(scan_in, batch, n_chunk_ctx, 0, l, scan_group(batch),
                                           state_shape=state_shape, state_prev=new_state)
        yf_l, yb_l = _rwkv_scan(scan_in, dec_batch, n_chunk_lat, m_ctx, l, scan_group(dec_batch), s0=state_wkv)
        mix = _rwkv_post((yf_c, yb_c), (yf_l, yb_l), bv, g, lnx_g3, lnx_b3, prep_params["e256"], mix, l,
                         m_ctx // ts, ts)
        x = _resid_matmul(x, mod, mix, w_out_b, pl.BlockSpec((None, tk, D_MODEL), lambda i, k: (l, k, 0)),
                          l, 5, 1.0, rot, tm, tk, "mix_out")
        if l < depth - 1:
            x = _ffn(x, *ffn_args, 2, 1, rot, tm, n_ctx_t + n_lat_t)
    fg = final_g.reshape(1, d)
    y_ctx = _ffn(x, *ffn_args, 2, 1, ctx_row, tm, n_ctx_t, final_g=fg)
    y_lat = _ffn(x, *ffn_args, 2, 1, lat_row, tm, n_lat_t, in_tile0=n_ctx_t, final_g=fg)
    return (y_ctx.reshape(batch, seq, d), y_lat.reshape(dec_batch, dec_seq, d), new_state)
```

```python
import functools

import numpy as np
import jax
import jax.numpy as jnp
from jax import lax
from jax.experimental import pallas as pl
from jax.experimental.pallas import tpu as pltpu

F32 = jnp.float32
BF16 = jnp.bfloat16

D_MODEL = 2048
GRID_W = 64
SGU_CHUNK = 128
A_HEADS = 4
A_WIDTH = 512
B_WIDTH = 512
B_GROUP_DIM = 128
C_WIDTH = 1024
MIX_WIDTH = A_WIDTH + B_WIDTH + C_WIDTH
HEAD_DIM = 64
N_HEADS = 16
UNIT_HEADS = 2
UNIT_W = UNIT_HEADS * HEAD_DIM
N_UNITS = N_HEADS // UNIT_HEADS
DECAY_RANK = 64
GATE_RANK = 160
C_IN = 3488
C_PAD = 3584
IN_COLS = 2 * A_WIDTH + B_WIDTH + C_IN
Z_BLOCK = 1024
Z_C0 = 2 * A_WIDTH + B_WIDTH
Z_COLS = Z_C0 + C_PAD
D_FF = 5632
N_MOD = 9
RMS_EPS = 1e-6
LN_EPS = 1e-5
LNX_EPS = 64e-5
SCAN_T = 64
MOD_ROWS = 8

VMEM_LIMIT = 56 * 1024 * 1024
FFN_VMEM_LIMIT = 60 * 1024 * 1024


def _cparams(sem):
    return pltpu.CompilerParams(dimension_semantics=sem, vmem_limit_bytes=VMEM_LIMIT)


def _dot(a, b):
    return jnp.dot(a, b, preferred_element_type=F32)


def _split3(x):
    hi = x.astype(BF16)
    r1 = x - hi.astype(F32)
    mid = r1.astype(BF16)
    lo = (r1 - mid.astype(F32)).astype(BF16)
    return hi, mid, lo


def _dot_ones_rhs(x, e2):
    hi = x.astype(BF16)
    lo = (x - hi.astype(F32)).astype(BF16)
    return _dot(jnp.concatenate([hi, lo], axis=1), e2)


def _dot_exact_lhs(e3, x):
    return _dot(e3, jnp.concatenate(_split3(x), axis=0))


def _split_rows3(b):
    bh = b.astype(BF16)
    bl = (b - bh.astype(F32)).astype(BF16)
    return jnp.concatenate([bh, bh, bl], axis=-2)


def _dot3_presplit(a, b3):
    ah = a.astype(BF16)
    al = (a - ah.astype(F32)).astype(BF16)
    return _dot(jnp.concatenate([ah, al, ah], axis=1), b3)


def _sigmoid(x):
    return 0.5 * jnp.tanh(0.5 * x) + 0.5


def _silu(x):
    return x * _sigmoid(x)


def _gelu_tanh(x):
    return 0.5 * x * (1.0 + jnp.tanh(0.7978845608028654 * (x + 0.044715 * (x * x * x))))


def _mod_kernel(c_ref, w_ref, b_ref, o_ref):
    s = _silu(c_ref[...]).astype(BF16)
    o_ref[...] = _dot(s, w_ref[...].astype(BF16)) + b_ref[...]


def _modulation(cond, w_mod, b_mod):
    depth = w_mod.shape[0]
    n = w_mod.shape[2]
    tn = 1024
    return pl.pallas_call(
        _mod_kernel,
        grid=(depth, n // tn),
        in_specs=[
            pl.BlockSpec((MOD_ROWS, D_MODEL), lambda l, j: (0, 0)),
            pl.BlockSpec((None, D_MODEL, tn), lambda l, j: (l, 0, j)),
            pl.BlockSpec((None, 1, tn), lambda l, j: (l, 0, j)),
        ],
        out_specs=pl.BlockSpec((None, MOD_ROWS, tn), lambda l, j: (l, 0, j)),
        out_shape=jax.ShapeDtypeStruct((depth, MOD_ROWS, n), F32),
        compiler_params=_cparams(("arbitrary", "arbitrary")),
        name="modulation",
    )(cond, w_mod, b_mod.reshape(depth, 1, n))


def _mod_spec(layer, slot, row_of_tile):
    return pl.BlockSpec((None, None, None, 1, D_MODEL),
                        lambda i, *_: (layer, row_of_tile(i), slot, 0, 0))


def _norm_spec(layer, slot):
    return pl.BlockSpec((None, None, 1, D_MODEL), lambda i, *_: (layer, slot, 0, 0))


def _modulated_norm(x_ref, g_ref, sc_ref, sh_ref):
    x = x_ref[...]
    rs = lax.rsqrt(jnp.mean(x * x, axis=-1, keepdims=True) + RMS_EPS)
    gain = g_ref[...] * (1.0 + sc_ref[...])
    return ((x * rs) * gain + sh_ref[...]).astype(BF16)


def _in_proj_kernel(x_ref, sh_ref, sc_ref, g_ref, wt_ref, o_ref, h_ref, *, valid_cols):
    j = pl.program_id(1)

    def step(first):
        if first:
            h = _modulated_norm(x_ref, g_ref, sc_ref, sh_ref)
            h_ref[...] = h
        else:
            h = h_ref[...]
        out = lax.dot_general(h, wt_ref[...].astype(BF16), (((1,), (1,)), ((), ())),
                              preferred_element_type=F32)
        tn = out.shape[1]
        if valid_cols % tn:
            col = j * tn + lax.broadcasted_iota(jnp.int32, (1, tn), 1)
            out = jnp.where(col < valid_cols, out, 0.0)
        o_ref[...] = out

    pl.when(j == 0)(lambda: step(True))
    pl.when(j > 0)(lambda: step(False))


def _in_proj(x, mod, norm_g, w_in_t, layer, row_of_tile, tm):
    m = x.shape[0]
    nb = Z_COLS // Z_BLOCK
    assert -(-IN_COLS // Z_BLOCK) == nb
    return pl.pallas_call(
        functools.partial(_in_proj_kernel, valid_cols=IN_COLS),
        grid=(m // tm, nb),
        in_specs=[
            pl.BlockSpec((tm, D_MODEL), lambda i, j: (i, 0)),
            _mod_spec(layer, 3, row_of_tile),
            _mod_spec(layer, 4, row_of_tile),
            _norm_spec(layer, 1),
            pl.BlockSpec((None, Z_BLOCK, D_MODEL), lambda i, j: (layer, j, 0)),
        ],
        out_specs=pl.BlockSpec((tm, Z_BLOCK), lambda i, j: (i, j)),
        out_shape=jax.ShapeDtypeStruct((m, Z_COLS), F32),
        scratch_shapes=[pltpu.VMEM((tm, D_MODEL), BF16)],
        compiler_params=_cparams(("arbitrary", "arbitrary")),
        name="in_proj",
    )(x, mod, mod, norm_g, w_in_t)


def _ffn_kernel(x_ref, sh_ref, sc_ref, gt_ref, g_ref, wg_ref, wu_ref, wo_ref, *rest, tn, final_norm):
    fg_ref = rest[0] if final_norm else None
    o_ref, h_ref = rest[-2:]
    j = pl.program_id(1)
    last = pl.num_programs(1) - 1

    def step(first, final):
        if first:
            h = _modulated_norm(x_ref, g_ref, sc_ref, sh_ref)
            h_ref[...] = h
        else:
            h = h_ref[...]
        gate = _dot(h, wg_ref[...].astype(BF16))
        up = _dot(h, wu_ref[...].astype(BF16))
        a = (_silu(gate) * up).astype(BF16)
        for n in range(o_ref.shape[1] // tn):
            cols = slice(n * tn, (n + 1) * tn)
            acc = _dot(a, wo_ref[:, cols].astype(BF16))
            if not first:
                acc = o_ref[:, cols] + acc
            if final:
                acc = x_ref[:, cols] + (0.5 * gt_ref[:, cols]) * acc
            o_ref[:, cols] = acc
        if final and final_norm:
            y = o_ref[...]
            rs = lax.rsqrt(jnp.mean(y * y, axis=-1, keepdims=True) + RMS_EPS)
            o_ref[...] = y * rs * fg_ref[...]

    pl.when(j == 0)(lambda: step(True, False))
    pl.when(jnp.logical_and(j > 0, j < last))(lambda: step(False, False))
    pl.when(j == last)(lambda: step(False, True))


def _ffn(x, mod, norm_g, w_in, w_out, layer, sub, ffn_idx, row_of_tile, tm, n_tiles, in_tile0=0, out_tile0=0,
         out_rows=None, out_prev=None, final_g=None, tf=256):
    nf = D_FF // tf
    out_rows = n_tiles * tm if out_rows is None else out_rows
    args = [x, mod, mod, mod, norm_g, w_in, w_in, w_out]
    in_specs = [
        pl.BlockSpec((tm, D_MODEL), lambda i, j: (in_tile0 + i, 0)),
        _mod_spec(layer, 3 * sub, row_of_tile),
        _mod_spec(layer, 3 * sub + 1, row_of_tile),
        _mod_spec(layer, 3 * sub + 2, row_of_tile),
        _norm_spec(layer, sub),
        pl.BlockSpec((None, None, D_MODEL, tf), lambda i, j: (layer, ffn_idx, 0, j)),
        pl.BlockSpec((None, None, D_MODEL, tf), lambda i, j: (layer, ffn_idx, 0, nf + j)),
        pl.BlockSpec((None, None, tf, D_MODEL), lambda i, j: (layer, ffn_idx, j, 0)),
    ]
    if final_g is not None:
        args.append(final_g)
        in_specs.append(pl.BlockSpec((1, D_MODEL), lambda i, j: (0, 0)))
    aliases = {}
    if out_prev is not None:
        aliases = {len(args): 0}
        args.append(out_prev)
        in_specs.append(pl.BlockSpec(memory_space=pl.ANY))
    return pl.pallas_call(
        functools.partial(_ffn_kernel, tn=512, final_norm=final_g is not None),
        grid=(n_tiles, nf),
        in_specs=in_specs,
        out_specs=pl.BlockSpec((tm, D_MODEL), lambda i, j: (out_tile0 + i, 0)),
        out_shape=jax.ShapeDtypeStruct((out_rows, D_MODEL), F32),
        input_output_aliases=aliases,
        scratch_shapes=[pltpu.VMEM((tm, D_MODEL), BF16)],
        compiler_params=pltpu.CompilerParams(dimension_semantics=("arbitrary", "arbitrary"),
                                             vmem_limit_bytes=FFN_VMEM_LIMIT),
        name="ffn",
    )(*args)


def _resid_matmul_kernel(x_ref, gt_ref, a_ref, w_ref, o_ref, *, coef, tn):
    k = pl.program_id(1)
    last = pl.num_programs(1) - 1

    def step(first, final):
        a = a_ref[...]
        for n in range(o_ref.shape[1] // tn):
            cols = slice(n * tn, (n + 1) * tn)
            acc = _dot(a, w_ref[:, cols].astype(BF16))
            if not first:
                acc = o_ref[:, cols] + acc
            if final:
                acc = x_ref[:, cols] + (coef * gt_ref[:, cols]) * acc
            o_ref[:, cols] = acc

    pl.when(k == 0)(lambda: step(True, False))
    pl.when(jnp.logical_and(k > 0, k < last))(lambda: step(False, False))
    pl.when(k == last)(lambda: step(False, True))


def _resid_matmul(x, mod, a, w, w_spec, layer, gate_slot, coef, row_of_tile, tm, tk, name):
    m = x.shape[0]
    return pl.pallas_call(
        functools.partial(_resid_matmul_kernel, coef=coef, tn=512),
        grid=(m // tm, a.shape[1] // tk),
        in_specs=[
            pl.BlockSpec((tm, D_MODEL), lambda i, k: (i, 0)),
            _mod_spec(layer, gate_slot, row_of_tile),
            pl.BlockSpec((tm, tk), lambda i, k: (i, k)),
            w_spec,
        ],
        out_specs=pl.BlockSpec((tm, D_MODEL), lambda i, k: (i, 0)),
        out_shape=jax.ShapeDtypeStruct((m, D_MODEL), F32),
        compiler_params=_cparams(("arbitrary", "arbitrary")),
        name=name,
    )(x, mod, a, w)


def _sgu_kernel(zu_ref, zv_ref, lg_ref, lb_ref, w_ref, bias_ref, o_ref):
    u = _gelu_tanh(zu_ref[...])
    v = _gelu_tanh(zv_ref[...])
    mu = jnp.mean(v, axis=-1, keepdims=True)
    vc = v - mu
    var = jnp.mean(vc * vc, axis=-1, keepdims=True)
    vn = (vc * lax.rsqrt(var + LN_EPS) * lg_ref[...] + lb_ref[...]).astype(BF16)
    hd = A_WIDTH // A_HEADS
    for c in range(u.shape[0] // SGU_CHUNK):
        rows = slice(c * SGU_CHUNK, (c + 1) * SGU_CHUNK)
        for h in range(A_HEADS):
            cols = slice(h * hd, (h + 1) * hd)
            mixed = _dot(w_ref[h], vn[rows, cols]) + bias_ref[:, cols]
            o_ref[rows, cols] = (u[rows, cols] * mixed).astype(BF16)


def _sgu(z, ln_g, ln_b, w_s, bias, layer, tm):
    m = z.shape[0]
    cu = 0
    return pl.pallas_call(
        _sgu_kernel,
        grid=(m // tm,),
        in_specs=[
            pl.BlockSpec((tm, A_WIDTH), lambda i: (i, cu)),
            pl.BlockSpec((tm, A_WIDTH), lambda i: (i, cu + 1)),
            pl.BlockSpec((None, 1, A_WIDTH), lambda i: (layer, 0, 0)),
            pl.BlockSpec((None, 1, A_WIDTH), lambda i: (layer, 0, 0)),
            pl.BlockSpec((None, A_HEADS, SGU_CHUNK, SGU_CHUNK), lambda i: (layer, 0, 0, 0)),
            pl.BlockSpec((None, SGU_CHUNK, A_WIDTH), lambda i: (layer, 0, 0)),
        ],
        out_specs=pl.BlockSpec((tm, A_WIDTH), lambda i: (i, 0)),
        out_shape=jax.ShapeDtypeStruct((m, MIX_WIDTH), BF16),
        compiler_params=_cparams(("arbitrary",)),
        name="sgu",
    )(z, z, ln_g, ln_b, w_s, bias)


def _dft_mats(n):
    idx = np.arange(n)
    ang = 2.0 * np.pi * ((idx[:, None] * idx[None, :]) % n) / n
    return np.cos(ang) / np.sqrt(n), np.sin(ang) / np.sqrt(n)


def _hi_lo(a):
    a32 = jnp.asarray(a, F32)
    hi = a32.astype(BF16)
    lo = (a32 - hi.astype(F32)).astype(BF16)
    return hi, lo


def _fnet_kernel(z_ref, fdh_ref, fdl_ref, fsh_ref, fsl_ref, flh_ref, fll_ref, mix_ref, o_ref, t_ref, *,
                 n_ctx_tiles, seq):
    del mix_ref
    tm = z_ref.shape[0]
    x = z_ref[...]
    xh = x.astype(BF16)
    xl = (x - xh.astype(F32)).astype(BF16)
    gd = B_GROUP_DIM
    for g in range(B_WIDTH // gd):
        cols = slice(g * gd, (g + 1) * gd)
        t = _dot(xh[:, cols], fdh_ref[...]) + _dot(xl[:, cols], fdh_ref[...]) + _dot(xh[:, cols], fdl_ref[...])
        t_ref[0:tm, cols] = t[:, 0:gd]
        t_ref[tm:2 * tm, cols] = t[:, gd:2 * gd]

    def position_dft(fh_ref, fl_ref, rows_in, rows_out):
        tc = t_ref[rows_in[0], :]
        ts = t_ref[rows_in[1], :]
        tt = jnp.concatenate([tc, ts], axis=0)
        th = tt.astype(BF16)
        tl = (tt - th.astype(F32)).astype(BF16)
        out = _dot(fh_ref[...], th) + _dot(fl_ref[...], th) + _dot(fh_ref[...], tl)
        o_ref[rows_out, :] = out.astype(BF16)

    is_ctx = pl.program_id(0) < n_ctx_tiles

    @pl.when(is_ctx)
    def _():
        for s in range(tm // seq):
            r0 = slice(s * seq, (s + 1) * seq)
            r1 = slice(tm + s * seq, tm + (s + 1) * seq)
            position_dft(fsh_ref, fsl_ref, (r0, r1), r0)

    @pl.when(jnp.logical_not(is_ctx))
    def _():
        position_dft(flh_ref, fll_ref, (slice(0, tm), slice(tm, 2 * tm)), slice(0, tm))


def _fnet(z, consts, mix, n_ctx_tiles, seq, tm):
    m = z.shape[0]
    cb = 2 * A_WIDTH // B_WIDTH
    full = lambda a: pl.BlockSpec(a.shape, lambda i: (0,) * a.ndim)
    return pl.pallas_call(
        functools.partial(_fnet_kernel, n_ctx_tiles=n_ctx_tiles, seq=seq),
        grid=(m // tm,),
        in_specs=([pl.BlockSpec((tm, B_WIDTH), lambda i: (i, cb))] + [full(a) for a in consts]
                  + [pl.BlockSpec(memory_space=pl.ANY)]),
        out_specs=pl.BlockSpec((tm, B_WIDTH), lambda i: (i, A_WIDTH // B_WIDTH)),
        out_shape=jax.ShapeDtypeStruct(mix.shape, mix.dtype),
        input_output_aliases={1 + len(consts): 0},
        scratch_shapes=[pltpu.VMEM((2 * tm, B_WIDTH), F32)],
        compiler_params=_cparams(("arbitrary",)),
        name="fnet",
    )(z, *consts, mix)


def _seg_sum(x, e_ref):
    w = e_ref.shape[1]
    parts = [_dot_ones_rhs(x[:, b * w:(b + 1) * w], e_ref[...]) for b in range(x.shape[1] // w)]
    return parts[0] if len(parts) == 1 else jnp.concatenate(parts, axis=1)


def _token_shift_mix(z_ref, zc_ref, zp_ref, zn_ref, mu_ref, bounds, period, up_ok, dn_ok):
    tm = zc_ref.shape[0]
    row = lax.broadcasted_iota(jnp.int32, (tm, 1), 0)
    pos = row & (period - 1)
    lanes = 128

    def neighbour(kind, src):
        xs = zc_ref[:, src]
        if kind == 0:
            return jnp.where(pos == 0, 0.0, pltpu.roll(xs, 1, 0))
        if kind == 1:
            return jnp.where(pos == period - 1, 0.0, pltpu.roll(xs, tm - 1, 0))
        if kind == 2:
            return jnp.concatenate([jnp.where(up_ok, zp_ref[:, src], 0.0), xs[:tm - GRID_W]], axis=0)
        return jnp.concatenate([xs[GRID_W:], jnp.where(dn_ok, zn_ref[:, src], 0.0)], axis=0)

    edges = sorted({0, C_PAD} | {b // lanes * lanes for b in bounds} | {-(-b // lanes) * lanes for b in bounds})
    for lo, hi in zip(edges[:-1], edges[1:]):
        cols = slice(lo, hi)
        src = slice(Z_C0 + lo, Z_C0 + hi)
        kinds = [n for n in range(len(bounds) + 1)
                 if (bounds[n - 1] if n else 0) < hi and lo < (bounds[n] if n < len(bounds) else C_PAD)]
        zs = neighbour(kinds[-1], src)
        lane = lo + lax.broadcasted_iota(jnp.int32, (1, hi - lo), 1)
        for n in reversed(kinds[:-1]):
            zs = jnp.where(lane < bounds[n], neighbour(n, src), zs)
        xs = zc_ref[:, src]
        z_ref[:, cols] = xs + (zs - xs) * mu_ref[:, cols]


def _prep_kernel(zc_ref, zp_ref, zn_ref, mu_ref, w0_ref, w2_ref, a0_ref, a2_ref, kk_ref, ka_ref, rk_ref,
                 g2_ref, e_ref,
                 rvk_o, kf_o, kb_o, ld_o, bv_o, g_o, z_ref, *,
                 n_ctx_tiles, tiles_per_lat):
    i = pl.program_id(0)
    tm = zc_ref.shape[0]
    is_ctx = i < n_ctx_tiles
    q = (i - n_ctx_tiles) % tiles_per_lat
    half = C_IN // 2
    quarter = C_IN // 4

    @pl.when(is_ctx)
    def _():
        _token_shift_mix(z_ref, zc_ref, zp_ref, zn_ref, mu_ref, (half,), tm, False, False)

    @pl.when(jnp.logical_not(is_ctx))
    def _():
        _token_shift_mix(z_ref, zc_ref, zp_ref, zn_ref, mu_ref, (quarter, 2 * quarter, 3 * quarter), GRID_W,
                         q > 0, q < tiles_per_lat - 1)

    cw = C_WIDTH
    r = z_ref[:, 0:cw]
    k = z_ref[:, cw:2 * cw]
    v = z_ref[:, 2 * cw:3 * cw]
    wd = z_ref[:, 3 * cw:3 * cw + 128]
    ad = z_ref[:, 3 * cw + 128:3 * cw + 256]
    gd = z_ref[:, 3 * cw + 256:C_PAD]

    uw = _dot3_presplit(jnp.tanh(wd), w2_ref[...])
    ua = _dot3_presplit(ad, a2_ref[...])
    kk_raw = k * kk_ref[...]
    ss = _seg_sum(kk_raw * kk_raw, e_ref)
    kk = kk_raw * lax.rsqrt(jnp.maximum(ss, 1e-24))
    rvk_o[:, 0:cw] = r.astype(BF16)
    rvk_o[:, cw:2 * cw] = v.astype(BF16)
    rvk_o[:, 2 * cw:3 * cw] = kk.astype(BF16)
    km_sum = None
    for d, kd_o in enumerate((kf_o, kb_o)):
        u = uw[:, d * cw:(d + 1) * cw] + w0_ref[d:d + 1, :]
        ld_o[:, d * cw:(d + 1) * cw] = -float(np.exp(-0.5)) * _sigmoid(u)
        a = _sigmoid(ua[:, d * cw:(d + 1) * cw] + a0_ref[d:d + 1, :])
        km = k * (1.0 + (a - 1.0) * ka_ref[...])
        kd_o[:, 0:cw] = km.astype(BF16)
        kd_o[:, cw:2 * cw] = (kk * a).astype(BF16)
        km_sum = km if km_sum is None else km_sum + km
    bonus = _seg_sum(r * km_sum * rk_ref[...], e_ref)
    bv_o[...] = bonus * v
    g_o[...] = _dot(_sigmoid(gd).astype(BF16), g2_ref[...]).astype(BF16)


def _rwkv_prep(z, p, layer, n_ctx_tiles, tiles_per_lat, tm):
    m = z.shape[0]
    hb = tm // GRID_W
    n_halo = m // GRID_W
    lay = lambda *shape: pl.BlockSpec((None,) + shape, lambda i: (layer,) + (0,) * len(shape))
    outs = [(3, BF16), (2, BF16), (2, BF16), (2, F32), (1, F32), (1, BF16)]
    return pl.pallas_call(
        functools.partial(_prep_kernel, n_ctx_tiles=n_ctx_tiles, tiles_per_lat=tiles_per_lat),
        grid=(m // tm,),
        in_specs=[
            pl.BlockSpec((tm, Z_COLS), lambda i: (i, 0)),
            pl.BlockSpec((GRID_W, Z_COLS), lambda i: (jnp.maximum(i * hb - 1, 0), 0)),
            pl.BlockSpec((GRID_W, Z_COLS), lambda i: (jnp.minimum(i * hb + hb, n_halo - 1), 0)),
            lay(1, C_PAD),
            lay(2, C_WIDTH),
            lay(384, 2 * C_WIDTH),
            lay(2, C_WIDTH),
            lay(384, 2 * C_WIDTH),
            lay(1, C_WIDTH),
            lay(1, C_WIDTH),
            lay(1, C_WIDTH),
            lay(256, C_WIDTH),
            pl.BlockSpec((512, 256), lambda i: (0, 0)),
        ],
        out_specs=[pl.BlockSpec((tm, n * C_WIDTH), lambda i: (i, 0)) for n, _ in outs],
        out_shape=[jax.ShapeDtypeStruct((m, n * C_WIDTH), dt) for n, dt in outs],
        scratch_shapes=[pltpu.VMEM((tm, C_PAD), F32)],
        compiler_params=_cparams(("arbitrary",)),
        name="rwkv_prep",
    )(z, z, z, p["mu"], p["w0"], p["w2"], p["a0"], p["a2"], p["k_k"], p["k_a"], p["r_k"], p["g2"],
      p["e256"])


def _bmm(a, b):
    return jnp.einsum("uik,ukj->uij", a.astype(BF16), b.astype(BF16), preferred_element_type=F32)


def _bmm_nt(a, b):
    return jnp.einsum("uik,ujk->uij", a.astype(BF16), b.astype(BF16), preferred_element_type=F32)


def _block_diag(y, bd_mask):
    return jnp.where(bd_mask, jnp.concatenate([y] * UNIT_HEADS, axis=1), jnp.zeros((), y.dtype))


def _to_units(x):
    return jnp.stack([x[:, p * UNIT_W:(p + 1) * UNIT_W] for p in range(N_UNITS)], axis=0)


def _scan_chunks(streams, hs):
    t = SCAN_T
    w = UNIT_W
    ti = lax.broadcasted_iota(jnp.int32, (t, w), 0)
    si = lax.broadcasted_iota(jnp.int32, (t, w), 1) & (HEAD_DIM - 1)
    eye = jnp.where(si == ti, 1.0, 0.0)
    t_row = lax.broadcasted_iota(jnp.int32, (t, 3 * t), 0)
    t_col = lax.broadcasted_iota(jnp.int32, (t, 3 * t), 1) & (t - 1)
    bd_mask = (lax.broadcasted_iota(jnp.int32, (w, w), 0) // HEAD_DIM
               == lax.broadcasted_iota(jnp.int32, (w, w), 1) // HEAD_DIM)

    q, ai, ki, vb, vf, lhs, p_tot = [], [], [], [], [], [], []
    for reverse, r, v, kk, ld, km, ka in streams:
        tri = jnp.where((t_col >= t_row) if reverse else (t_col <= t_row), 1.0, 0.0).astype(BF16)
        cum = _dot_exact_lhs(tri, ld)
        c_end = cum[0:1, :] if reverse else cum[t - 1:t, :]
        p_end = jnp.exp(c_end - cum)
        p_inv = jnp.exp(-cum)
        q.append(_to_units(jnp.concatenate([kk * jnp.exp(cum - ld), r * jnp.exp(cum)], axis=0).astype(BF16)))
        ai.append(_to_units((ka * p_inv).astype(BF16)))
        ki.append(_to_units((km * p_inv).astype(BF16)))
        vb.append(_to_units(v.astype(BF16)))
        vf.append(_to_units(v))
        lhs.append(_to_units(jnp.concatenate([ka * p_end, km * p_end], axis=0).astype(BF16)))
        p_tot.append(_to_units(jnp.exp(c_end)))
    q, ai, ki, vb, vf, lhs, p_tot = (jnp.concatenate(a, axis=0) for a in (q, ai, ki, vb, vf, lhs, p_tot))

    def masked(x, strictly):
        parts = []
        for n, stream in enumerate(streams):
            if stream[0]:
                keep = si > ti if strictly else si >= ti
            else:
                keep = si < ti if strictly else si <= ti
            parts.append(jnp.where(keep, x[n * N_UNITS:(n + 1) * N_UNITS], 0.0))
        return jnp.concatenate(parts, axis=0)

    rr = jnp.concatenate([_block_diag(ai, bd_mask), _block_diag(ki, bd_mask)], axis=1)
    sc = _bmm_nt(q, rr)
    la = masked(sc[:, 0:t, 0:w], True)
    lk = masked(sc[:, 0:t, w:2 * w], True)
    ma = masked(sc[:, t:2 * t, 0:w], False)
    mk = masked(sc[:, t:2 * t, w:2 * w], False)

    n_pow = -la
    x_inv = eye + n_pow
    n_pow = _bmm(n_pow, _block_diag(n_pow.astype(BF16), bd_mask))
    for _ in range(int(np.log2(t)) - 2):
        both = _bmm(jnp.concatenate([n_pow, x_inv], axis=1), _block_diag(n_pow.astype(BF16), bd_mask))
        x_inv = x_inv + both[:, t:2 * t]
        n_pow = both[:, 0:t]
    x_inv = x_inv + _bmm(x_inv, _block_diag(n_pow.astype(BF16), bd_mask))

    hb = _bmm_nt(q, hs)
    lv = _bmm(jnp.concatenate([lk, mk], axis=1), _block_diag(vb, bd_mask))
    u = -_bmm(x_inv, _block_diag((hb[:, 0:t] + lv[:, 0:t]).astype(BF16), bd_mask))
    y = hb[:, t:2 * t] + lv[:, t:2 * t] + _bmm(ma, _block_diag(u.astype(BF16), bd_mask))
    rhs_t = jnp.swapaxes(jnp.concatenate([u, vf], axis=1), 1, 2)
    upd = _bmm(rhs_t, lhs)
    hs_new = p_tot * hs + jnp.where(bd_mask, upd, 0.0)
    return y, hs_new


def _scan_kernel(*refs, has_s0, emit_state, aliased_state):
    ins = refs[:6]
    pos = 6
    s0_ref = None
    if has_s0:
        s0_ref = refs[pos]
        pos += 1
    if aliased_state:
        pos += 1
    y_refs = refs[pos:pos + 2]
    pos += 2
    st_ref = None
    if emit_state:
        st_ref = refs[pos]
        pos += 1
    h_ref = refs[pos]
    c = pl.program_id(1)
    hd = HEAD_DIM

    n_group = h_ref.shape[0]

    @pl.when(c == 0)
    def _():
        if has_s0:
            zero = jnp.zeros((hd, hd), F32)
            for s in range(n_group):
                for d in range(2):
                    for p in range(N_UNITS):
                        rows = [jnp.concatenate([s0_ref[s, d, UNIT_HEADS * p + a] if a == b else zero
                                                 for b in range(UNIT_HEADS)], axis=1) for a in range(UNIT_HEADS)]
                        h_ref[s, d, p] = jnp.concatenate(rows, axis=0)
        else:
            h_ref[...] = jnp.zeros_like(h_ref)

    streams = []
    for s in range(n_group):
        for d in range(2):
            rvk_ref, kd_ref, ld_ref = ins[3 * d:3 * d + 3]
            r, v, kk = (rvk_ref[s, :, n * C_WIDTH:(n + 1) * C_WIDTH].astype(F32) for n in range(3))
            km, ka = (kd_ref[s, :, n * C_WIDTH:(n + 1) * C_WIDTH].astype(F32) for n in range(2))
            streams.append((d == 1, r, v, kk, ld_ref[s], km, ka))
    n_all = len(streams) * N_UNITS
    y, hs_new = _scan_chunks(streams, h_ref[...].reshape(n_all, UNIT_W, UNIT_W))
    h_ref[...] = hs_new.reshape(h_ref.shape)
    for s in range(n_group):
        for d, y_ref in enumerate(y_refs):
            for p in range(N_UNITS):
                y_ref[s, :, p * UNIT_W:(p + 1) * UNIT_W] = y[(2 * s + d) * N_UNITS + p]

    if emit_state:
        @pl.when(c == pl.num_programs(1) - 1)
        def _():
            for s in range(n_group):
                for d in range(2):
                    for p in range(N_UNITS):
                        hs = h_ref[s, d, p]
                        for a in range(UNIT_HEADS):
                            st_ref[s, d, UNIT_HEADS * p + a] = hs[a * hd:(a + 1) * hd, a * hd:(a + 1) * hd]


def _rwkv_scan(arrs, n_seq, n_chunk, row0, layer, group, s0=None, state_shape=None, state_prev=None):
    seq_len = n_chunk * SCAN_T
    m_all = arrs[0].shape[0]
    assert row0 % (seq_len * group) == 0 and m_all % seq_len == 0 and n_seq % group == 0
    rvk, kf, kb, ld = (a.reshape(m_all // seq_len, seq_len, a.shape[1]) for a in arrs)
    g0 = row0 // (seq_len * group)
    fwd = lambda b, c: (b, c, 0)
    bwd = lambda b, c: (b, n_chunk - 1 - c, 0)
    blk = lambda imap, n=1: pl.BlockSpec((group, SCAN_T, n * C_WIDTH), imap)
    in_specs = [blk(lambda b, c: (g0 + b, c, 0), 3), blk(lambda b, c: (g0 + b, c, 0), 2),
                blk(lambda b, c: (g0 + b, c, 0)),
                blk(lambda b, c: (g0 + b, n_chunk - 1 - c, 0), 3), blk(lambda b, c: (g0 + b, n_chunk - 1 - c, 0), 2),
                blk(lambda b, c: (g0 + b, n_chunk - 1 - c, 1))]
    args = [rvk, kf, ld, rvk, kb, ld]
    st_spec = pl.BlockSpec((group, None, 2, N_HEADS, HEAD_DIM, HEAD_DIM), lambda b, c: (b, layer, 0, 0, 0, 0))
    if s0 is not None:
        in_specs.append(st_spec)
        args.append(s0)
    aliases = {}
    if state_prev is not None:
        aliases = {len(args): 2}
        in_specs.append(pl.BlockSpec(memory_space=pl.ANY))
        args.append(state_prev)
    y_shape = jax.ShapeDtypeStruct((n_seq, seq_len, C_WIDTH), F32)
    out_specs = [blk(fwd), blk(bwd)]
    out_shape = [y_shape, y_shape]
    if state_shape is not None:
        out_specs.append(st_spec)
        out_shape.append(jax.ShapeDtypeStruct(state_shape, F32))
    outs = pl.pallas_call(
        functools.partial(_scan_kernel, has_s0=s0 is not None, emit_state=state_shape is not None,
                          aliased_state=state_prev is not None),
        grid=(n_seq // group, n_chunk),
        in_specs=in_specs,
        out_specs=out_specs,
        out_shape=out_shape,
        input_output_aliases=aliases,
        scratch_shapes=[pltpu.VMEM((group, 2, N_UNITS, UNIT_W, UNIT_W), F32)],
        compiler_params=_cparams(("arbitrary", "arbitrary")),
        name="rwkv_scan",
    )(*args)
    ys = [y.reshape(n_seq * seq_len, C_WIDTH) for y in outs[:2]]
    return ys + list(outs[2:])


def _post_kernel(yf1_ref, yb1_ref, yf2_ref, yb2_ref, bv_ref, g_ref, lg_ref, lb_ref, e_ref, mix_ref, o_ref, *,
                 n_ctx_tiles):
    del mix_ref
    is_ctx = pl.program_id(0) < n_ctx_tiles
    y = jnp.where(is_ctx, yf1_ref[...] + yb1_ref[...], yf2_ref[...] + yb2_ref[...])
    inv = 1.0 / HEAD_DIM
    m = _seg_sum(y, e_ref) * inv
    yc = y - m
    var = _seg_sum(yc * yc, e_ref) * inv
    yn = yc * lax.rsqrt(var + LNX_EPS) * lg_ref[...] + lb_ref[...]
    o_ref[...] = ((yn + bv_ref[...]) * g_ref[...]).astype(BF16)


def _rwkv_post(y_ctx, y_lat, bv, g, lnx_g, lnx_b, e256, mix, layer, n_ctx_tiles, tm):
    m = bv.shape[0]
    n_tiles = m // tm
    row = pl.BlockSpec((tm, C_WIDTH), lambda i: (i, 0))
    row_ctx = pl.BlockSpec((tm, C_WIDTH), lambda i: (jnp.minimum(i, n_ctx_tiles - 1), 0))
    row_lat = pl.BlockSpec((tm, C_WIDTH), lambda i: (jnp.maximum(i - n_ctx_tiles, 0), 0))
    vec = pl.BlockSpec((None, 1, C_WIDTH), lambda i: (layer, 0, 0))
    return pl.pallas_call(
        functools.partial(_post_kernel, n_ctx_tiles=n_ctx_tiles),
        grid=(n_tiles,),
        in_specs=[row_ctx, row_ctx, row_lat, row_lat, row, row, vec, vec,
                  pl.BlockSpec((512, 256), lambda i: (0, 0)), pl.BlockSpec(memory_space=pl.ANY)],
        out_specs=pl.BlockSpec((tm, C_WIDTH), lambda i: (i, (A_WIDTH + B_WIDTH) // C_WIDTH)),
        out_shape=jax.ShapeDtypeStruct(mix.shape, mix.dtype),
        input_output_aliases={9: 0},
        compiler_params=_cparams(("arbitrary",)),
        name="rwkv_post",
    )(y_ctx[0], y_ctx[1], y_lat[0], y_lat[1], bv, g, lnx_g, lnx_b, e256, mix)


def kernel(x_prompt, x_sample, state_wkv, c, c_ctx, norm_g, w_mod, b_mod, ffn_w_in, ffn_w_out, w_in, w_out,
           sgu_ln_g, sgu_ln_b, sgu_w, sgu_b, shift_mu, decay_w0, decay_w2, iclr_a0, iclr_a2, k_k, k_a, r_k,
           gate_w2, lnx_g, lnx_b, final_g):
    batch, seq, d = x_prompt.shape
    dec_batch, dec_seq, _ = x_sample.shape
    depth = w_mod.shape[0]
    assert d == D_MODEL and dec_batch + 1 <= MOD_ROWS
    m_ctx = batch * seq
    tm = 1024
    ts = 512
    tp = 256
    assert seq == tp and dec_seq == tm and m_ctx % tm == 0 and dec_seq % GRID_W == 0

    def row_of_tile_fn(rows):
        n_ctx = m_ctx // rows
        per_lat = dec_seq // rows
        return lambda i: jnp.where(i < n_ctx, 0, 1 + (i - n_ctx) // per_lat)

    zeros_cw = jnp.zeros((depth, DECAY_RANK, C_WIDTH), F32)

    def both_dirs(w):
        top = jnp.concatenate([w[:, 0], zeros_cw], axis=-1)
        bot = jnp.concatenate([zeros_cw, w[:, 1]], axis=-1)
        return jnp.concatenate([top, bot], axis=1)

    e_np = (np.arange(256)[:, None] // HEAD_DIM == np.arange(256)[None, :] // HEAD_DIM)
    prep_params = {
        "mu": jnp.pad(shift_mu, ((0, 0), (0, C_PAD - C_IN))).reshape(depth, 1, C_PAD),
        "w0": decay_w0, "w2": _split_rows3(both_dirs(decay_w2)), "a0": iclr_a0, "a2": _split_rows3(both_dirs(iclr_a2)),
        "k_k": k_k.reshape(depth, 1, C_WIDTH), "k_a": k_a.reshape(depth, 1, C_WIDTH),
        "r_k": r_k.reshape(depth, 1, C_WIDTH),
        "g2": jnp.pad(gate_w2, ((0, 0), (0, 256 - GATE_RANK), (0, 0))).astype(BF16),
        "e256": jnp.asarray(np.concatenate([e_np, e_np], axis=0), BF16),
    }
    cd, sd = _dft_mats(B_GROUP_DIM)
    cs, ss = _dft_mats(seq)
    cl, sl = _dft_mats(dec_seq)
    fnet_consts = (_hi_lo(np.concatenate([cd, sd], axis=1)) + _hi_lo(np.concatenate([cs, -ss], axis=1))
                   + _hi_lo(np.concatenate([cl, -sl], axis=1)))
    sgu_bias = jnp.repeat(jnp.swapaxes(sgu_b, 1, 2), A_WIDTH // A_HEADS, axis=2)
    sgu_w_b = sgu_w.astype(BF16)
    ln_g = sgu_ln_g.reshape(depth, 1, A_WIDTH)
    ln_b = sgu_ln_b.reshape(depth, 1, A_WIDTH)
    lnx_g3 = lnx_g.reshape(depth, 1, C_WIDTH)
    lnx_b3 = lnx_b.reshape(depth, 1, C_WIDTH)
    norm_g4 = norm_g.reshape(depth, 3, 1, d)

    cond = jnp.concatenate([c_ctx[None, :], c, jnp.zeros((MOD_ROWS - 1 - dec_batch, d), F32)], axis=0)
    mod = _modulation(cond, w_mod, b_mod).reshape(depth, MOD_ROWS, N_MOD, 1, d)

    m = m_ctx + dec_batch * dec_seq
    n_ctx_t = m_ctx // tm
    n_lat_t = dec_batch * dec_seq // tm
    rot = row_of_tile_fn(tm)
    ctx_row = lambda i: 0
    lat_row = lambda i: 1 + i
    n_chunk_ctx = seq // SCAN_T
    n_chunk_lat = dec_seq // SCAN_T
    state_shape = (batch, depth, 2, N_HEADS, HEAD_DIM, HEAD_DIM)
    tk = 1024
    scan_group = lambda n_seq: 4 if n_seq % 4 == 0 else 1
    w_in_t = jnp.swapaxes(w_in, 1, 2).astype(BF16)
    w_out_b = w_out.astype(BF16)
    new_state = None
    x = None
    for l in range(depth):
        ffn_args = (mod, norm_g4, ffn_w_in, ffn_w_out, l)
        if l == 0:
            x = _ffn(x_prompt.reshape(m_ctx, d), *ffn_args, 0, 0, ctx_row, tm, n_ctx_t, out_rows=m)
            x = _ffn(x_sample.reshape(m - m_ctx, d), *ffn_args, 0, 0, lat_row, tm, n_lat_t, out_tile0=n_ctx_t,
                     out_rows=m, out_prev=x)
        else:
            x = _ffn(x, *ffn_args, 0, 0, rot, tm, n_ctx_t + n_lat_t)
        z = _in_proj(x, mod, norm_g4, w_in_t, l, rot, tm)
        mix = _sgu(z, ln_g, ln_b, sgu_w_b, sgu_bias, l, ts)
        mix = _fnet(z, fnet_consts, mix, m_ctx // tm, seq, tm)
        *scan_in, bv, g = _rwkv_prep(z, prep_params, l, m_ctx // tp, dec_seq // tp, tp)
        yf_c, yb_c, new_state = _rwkv_scan(scan_in, batch, n_chunk_ctx, 0, l, scan_group(batch),
                                           state_shape=state_shape, state_prev=new_state)
        yf_l, yb_l = _rwkv_scan(scan_in, dec_batch, n_chunk_lat, m_ctx, l, scan_group(dec_batch), s0=state_wkv)
        mix = _rwkv_post((yf_c, yb_c), (yf_l, yb_l), bv, g, lnx_g3, lnx_b3, prep_params["e256"], mix, l,
                         m_ctx // ts, ts)
        x = _resid_matmul(x, mod, mix, w_out_b, pl.BlockSpec((None, tk, D_MODEL), lambda i, k: (l, k, 0)),
                          l, 5, 1.0, rot, tm, tk, "mix_out")
        if l < depth - 1:
            x = _ffn(x, *ffn_args, 2, 1, rot, tm, n_ctx_t + n_lat_t)
    fg = final_g.reshape(1, d)
    y_ctx = _ffn(x, *ffn_args, 2, 1, ctx_row, tm, n_ctx_t, final_g=fg)
    y_lat = _ffn(x, *ffn_args, 2, 1, lat_row, tm, n_lat_t, in_tile0=n_ctx_t, final_g=fg)
    return (y_ctx.reshape(batch, seq, d), y_lat.reshape(dec_batch, dec_seq, d), new_state)
```

```python
import functools

import numpy as np
import jax
import jax.numpy as jnp
from jax import lax
from jax.experimental import pallas as pl
from jax.experimental.pallas import tpu as pltpu

F32 = jnp.float32
BF16 = jnp.bfloat16

D_MODEL = 2048
GRID_W = 64
SGU_CHUNK = 128
A_HEADS = 4
A_WIDTH = 512
B_WIDTH = 512
B_GROUP_DIM = 128
C_WIDTH = 1024
MIX_WIDTH = A_WIDTH + B_WIDTH + C_WIDTH
HEAD_DIM = 64
N_HEADS = 16
UNIT_HEADS = 2
UNIT_W = UNIT_HEADS * HEAD_DIM
N_UNITS = N_HEADS // UNIT_HEADS
DECAY_RANK = 64
GATE_RANK = 160
C_IN = 3488
C_PAD = 3584
IN_COLS = 2 * A_WIDTH + B_WIDTH + C_IN
Z_BLOCK = 1024
Z_C0 = 2 * A_WIDTH + B_WIDTH
Z_COLS = Z_C0 + C_PAD
D_FF = 5632
N_MOD = 9
RMS_EPS = 1e-6
LN_EPS = 1e-5
LNX_EPS = 64e-5
SCAN_T = 64
MOD_ROWS = 8

VMEM_LIMIT = 56 * 1024 * 1024
FFN_VMEM_LIMIT = 60 * 1024 * 1024


def _cparams(sem):
    return pltpu.CompilerParams(dimension_semantics=sem, vmem_limit_bytes=VMEM_LIMIT)


def _dot(a, b):
    return jnp.dot(a, b, preferred_element_type=F32)


def _split3(x):
    hi = x.astype(BF16)
    r1 = x - hi.astype(F32)
    mid = r1.astype(BF16)
    lo = (r1 - mid.astype(F32)).astype(BF16)
    return hi, mid, lo


def _dot_ones_rhs(x, e2):
    hi = x.astype(BF16)
    lo = (x - hi.astype(F32)).astype(BF16)
    return _dot(jnp.concatenate([hi, lo], axis=1), e2)


def _dot_exact_lhs(e3, x):
    return _dot(e3, jnp.concatenate(_split3(x), axis=0))


def _split_rows3(b):
    bh = b.astype(BF16)
    bl = (b - bh.astype(F32)).astype(BF16)
    return jnp.concatenate([bh, bh, bl], axis=-2)


def _dot3_presplit(a, b3):
    ah = a.astype(BF16)
    al = (a - ah.astype(F32)).astype(BF16)
    return _dot(jnp.concatenate([ah, al, ah], axis=1), b3)


def _sigmoid(x):
    return 0.5 * jnp.tanh(0.5 * x) + 0.5


def _silu(x):
    return x * _sigmoid(x)


def _gelu_tanh(x):
    return 0.5 * x * (1.0 + jnp.tanh(0.7978845608028654 * (x + 0.044715 * (x * x * x))))


def _mod_kernel(c_ref, w_ref, b_ref, o_ref):
    s = _silu(c_ref[...]).astype(BF16)
    o_ref[...] = _dot(s, w_ref[...].astype(BF16)) + b_ref[...]


def _modulation(cond, w_mod, b_mod):
    depth = w_mod.shape[0]
    n = w_mod.shape[2]
    tn = 1024
    return pl.pallas_call(
        _mod_kernel,
        grid=(depth, n // tn),
        in_specs=[
            pl.BlockSpec((MOD_ROWS, D_MODEL), lambda l, j: (0, 0)),
            pl.BlockSpec((None, D_MODEL, tn), lambda l, j: (l, 0, j)),
            pl.BlockSpec((None, 1, tn), lambda l, j: (l, 0, j)),
        ],
        out_specs=pl.BlockSpec((None, MOD_ROWS, tn), lambda l, j: (l, 0, j)),
        out_shape=jax.ShapeDtypeStruct((depth, MOD_ROWS, n), F32),
        compiler_params=_cparams(("arbitrary", "arbitrary")),
        name="modulation",
    )(cond, w_mod, b_mod.reshape(depth, 1, n))


def _mod_spec(layer, slot, row_of_tile):
    return pl.BlockSpec((None, None, None, 1, D_MODEL),
                        lambda i, *_: (layer, row_of_tile(i), slot, 0, 0))


def _norm_spec(layer, slot):
    return pl.BlockSpec((None, None, 1, D_MODEL), lambda i, *_: (layer, slot, 0, 0))


def _modulated_norm(x_ref, g_ref, sc_ref, sh_ref):
    x = x_ref[...]
    rs = lax.rsqrt(jnp.mean(x * x, axis=-1, keepdims=True) + RMS_EPS)
    gain = g_ref[...] * (1.0 + sc_ref[...])
    return ((x * rs) * gain + sh_ref[...]).astype(BF16)


def _in_proj_kernel(x_ref, sh_ref, sc_ref, g_ref, wt_ref, o_ref, h_ref, *, valid_cols):
    j = pl.program_id(1)

    def step(first):
        if first:
            h = _modulated_norm(x_ref, g_ref, sc_ref, sh_ref)
            h_ref[...] = h
        else:
            h = h_ref[...]
        out = lax.dot_general(h, wt_ref[...].astype(BF16), (((1,), (1,)), ((), ())),
                              preferred_element_type=F32)
        tn = out.shape[1]
        if valid_cols % tn:
            col = j * tn + lax.broadcasted_iota(jnp.int32, (1, tn), 1)
            out = jnp.where(col < valid_cols, out, 0.0)
        o_ref[...] = out

    pl.when(j == 0)(lambda: step(True))
    pl.when(j > 0)(lambda: step(False))


def _in_proj(x, mod, norm_g, w_in_t, layer, row_of_tile, tm):
    m = x.shape[0]
    nb = Z_COLS // Z_BLOCK
    assert -(-IN_COLS // Z_BLOCK) == nb
    return pl.pallas_call(
        functools.partial(_in_proj_kernel, valid_cols=IN_COLS),
        grid=(m // tm, nb),
        in_specs=[
            pl.BlockSpec((tm, D_MODEL), lambda i, j: (i, 0)),
            _mod_spec(layer, 3, row_of_tile),
            _mod_spec(layer, 4, row_of_tile),
            _norm_spec(layer, 1),
            pl.BlockSpec((None, Z_BLOCK, D_MODEL), lambda i, j: (layer, j, 0)),
        ],
        out_specs=pl.BlockSpec((tm, Z_BLOCK), lambda i, j: (i, j)),
        out_shape=jax.ShapeDtypeStruct((m, Z_COLS), F32),
        scratch_shapes=[pltpu.VMEM((tm, D_MODEL), BF16)],
        compiler_params=_cparams(("arbitrary", "arbitrary")),
        name="in_proj",
    )(x, mod, mod, norm_g, w_in_t)


def _ffn_kernel(x_ref, sh_ref, sc_ref, gt_ref, g_ref, wg_ref, wu_ref, wo_ref, *rest, tn, final_norm):
    fg_ref = rest[0] if final_norm else None
    o_ref, h_ref = rest[-2:]
    j = pl.program_id(1)
    last = pl.num_programs(1) - 1

    def step(first, final):
        if first:
            h = _modulated_norm(x_ref, g_ref, sc_ref, sh_ref)
            h_ref[...] = h
        else:
            h = h_ref[...]
        gate = _dot(h, wg_ref[...].astype(BF16))
        up = _dot(h, wu_ref[...].astype(BF16))
        a = (_silu(gate) * up).astype(BF16)
        for n in range(o_ref.shape[1] // tn):
            cols = slice(n * tn, (n + 1) * tn)
            acc = _dot(a, wo_ref[:, cols].astype(BF16))
            if not first:
                acc = o_ref[:, cols] + acc
            if final:
                acc = x_ref[:, cols] + (0.5 * gt_ref[:, cols]) * acc
            o_ref[:, cols] = acc
        if final and final_norm:
            y = o_ref[...]
            rs = lax.rsqrt(jnp.mean(y * y, axis=-1, keepdims=True) + RMS_EPS)
            o_ref[...] = y * rs * fg_ref[...]

    pl.when(j == 0)(lambda: step(True, False))
    pl.when(jnp.logical_and(j > 0, j < last))(lambda: step(False, False))
    pl.when(j == last)(lambda: step(False, True))


def _ffn(x, mod, norm_g, w_in, w_out, layer, sub, ffn_idx, row_of_tile, tm, n_tiles, in_tile0=0, out_tile0=0,
         out_rows=None, out_prev=None, final_g=None, tf=256):
    nf = D_FF // tf
    out_rows = n_tiles * tm if out_rows is None else out_rows
    args = [x, mod, mod, mod, norm_g, w_in, w_in, w_out]
    in_specs = [
        pl.BlockSpec((tm, D_MODEL), lambda i, j: (in_tile0 + i, 0)),
        _mod_spec(layer, 3 * sub, row_of_tile),
        _mod_spec(layer, 3 * sub + 1, row_of_tile),
        _mod_spec(layer, 3 * sub + 2, row_of_tile),
        _norm_spec(layer, sub),
        pl.BlockSpec((None, None, D_MODEL, tf), lambda i, j: (layer, ffn_idx, 0, j)),
        pl.BlockSpec((None, None, D_MODEL, tf), lambda i, j: (layer, ffn_idx, 0, nf + j)),
        pl.BlockSpec((None, None, tf, D_MODEL), lambda i, j: (layer, ffn_idx, j, 0)),
    ]
    if final_g is not None:
        args.append(final_g)
        in_specs.append(pl.BlockSpec((1, D_MODEL), lambda i, j: (0, 0)))
    aliases = {}
    if out_prev is not None:
        aliases = {len(args): 0}
        args.append(out_prev)
        in_specs.append(pl.BlockSpec(memory_space=pl.ANY))
    return pl.pallas_call(
        functools.partial(_ffn_kernel, tn=512, final_norm=final_g is not None),
        grid=(n_tiles, nf),
        in_specs=in_specs,
        out_specs=pl.BlockSpec((tm, D_MODEL), lambda i, j: (out_tile0 + i, 0)),
        out_shape=jax.ShapeDtypeStruct((out_rows, D_MODEL), F32),
        input_output_aliases=aliases,
        scratch_shapes=[pltpu.VMEM((tm, D_MODEL), BF16)],
        compiler_params=pltpu.CompilerParams(dimension_semantics=("arbitrary", "arbitrary"),
                                             vmem_limit_bytes=FFN_VMEM_LIMIT),
        name="ffn",
    )(*args)


def _resid_matmul_kernel(x_ref, gt_ref, a_ref, w_ref, o_ref, *, coef, tn):
    k = pl.program_id(1)
    last = pl.num_programs(1) - 1

    def step(first, final):
        a = a_ref[...]
        for n in range(o_ref.shape[1] // tn):
            cols = slice(n * tn, (n + 1) * tn)
            acc = _dot(a, w_ref[:, cols].astype(BF16))
            if not first:
                acc = o_ref[:, cols] + acc
            if final:
                acc = x_ref[:, cols] + (coef * gt_ref[:, cols]) * acc
            o_ref[:, cols] = acc

    pl.when(k == 0)(lambda: step(True, False))
    pl.when(jnp.logical_and(k > 0, k < last))(lambda: step(False, False))
    pl.when(k == last)(lambda: step(False, True))


def _resid_matmul(x, mod, a, w, w_spec, layer, gate_slot, coef, row_of_tile, tm, tk, name):
    m = x.shape[0]
    return pl.pallas_call(
        functools.partial(_resid_matmul_kernel, coef=coef, tn=512),
        grid=(m // tm, a.shape[1] // tk),
        in_specs=[
            pl.BlockSpec((tm, D_MODEL), lambda i, k: (i, 0)),
            _mod_spec(layer, gate_slot, row_of_tile),
            pl.BlockSpec((tm, tk), lambda i, k: (i, k)),
            w_spec,
        ],
        out_specs=pl.BlockSpec((tm, D_MODEL), lambda i, k: (i, 0)),
        out_shape=jax.ShapeDtypeStruct((m, D_MODEL), F32),
        compiler_params=_cparams(("arbitrary", "arbitrary")),
        name=name,
    )(x, mod, a, w)


def _sgu_kernel(zu_ref, zv_ref, lg_ref, lb_ref, w_ref, bias_ref, o_ref):
    u = _gelu_tanh(zu_ref[...])
    v = _gelu_tanh(zv_ref[...])
    mu = jnp.mean(v, axis=-1, keepdims=True)
    vc = v - mu
    var = jnp.mean(vc * vc, axis=-1, keepdims=True)
    vn = (vc * lax.rsqrt(var + LN_EPS) * lg_ref[...] + lb_ref[...]).astype(BF16)
    hd = A_WIDTH // A_HEADS
    for c in range(u.shape[0] // SGU_CHUNK):
        rows = slice(c * SGU_CHUNK, (c + 1) * SGU_CHUNK)
        for h in range(A_HEADS):
            cols = slice(h * hd, (h + 1) * hd)
            mixed = _dot(w_ref[h], vn[rows, cols]) + bias_ref[:, cols]
            o_ref[rows, cols] = (u[rows, cols] * mixed).astype(BF16)


def _sgu(z, ln_g, ln_b, w_s, bias, layer, tm):
    m = z.shape[0]
    cu = 0
    return pl.pallas_call(
        _sgu_kernel,
        grid=(m // tm,),
        in_specs=[
            pl.BlockSpec((tm, A_WIDTH), lambda i: (i, cu)),
            pl.BlockSpec((tm, A_WIDTH), lambda i: (i, cu + 1)),
            pl.BlockSpec((None, 1, A_WIDTH), lambda i: (layer, 0, 0)),
            pl.BlockSpec((None, 1, A_WIDTH), lambda i: (layer, 0, 0)),
            pl.BlockSpec((None, A_HEADS, SGU_CHUNK, SGU_CHUNK), lambda i: (layer, 0, 0, 0)),
            pl.BlockSpec((None, SGU_CHUNK, A_WIDTH), lambda i: (layer, 0, 0)),
        ],
        out_specs=pl.BlockSpec((tm, A_WIDTH), lambda i: (i, 0)),
        out_shape=jax.ShapeDtypeStruct((m, MIX_WIDTH), BF16),
        compiler_params=_cparams(("arbitrary",)),
        name="sgu",
    )(z, z, ln_g, ln_b, w_s, bias)


def _dft_mats(n):
    idx = np.arange(n)
    ang = 2.0 * np.pi * ((idx[:, None] * idx[None, :]) % n) / n
    return np.cos(ang) / np.sqrt(n), np.sin(ang) / np.sqrt(n)


def _hi_lo(a):
    a32 = jnp.asarray(a, F32)
    hi = a32.astype(BF16)
    lo = (a32 - hi.astype(F32)).astype(BF16)
    return hi, lo


def _fnet_kernel(z_ref, fdh_ref, fdl_ref, fsh_ref, fsl_ref, flh_ref, fll_ref, mix_ref, o_ref, t_ref, *,
                 n_ctx_tiles, seq):
    del mix_ref
    tm = z_ref.shape[0]
    x = z_ref[...]
    xh = x.astype(BF16)
    xl = (x - xh.astype(F32)).astype(BF16)
    gd = B_GROUP_DIM
    for g in range(B_WIDTH // gd):
        cols = slice(g * gd, (g + 1) * gd)
        t = _dot(xh[:, cols], fdh_ref[...]) + _dot(xl[:, cols], fdh_ref[...]) + _dot(xh[:, cols], fdl_ref[...])
        t_ref[0:tm, cols] = t[:, 0:gd]
        t_ref[tm:2 * tm, cols] = t[:, gd:2 * gd]

    def position_dft(fh_ref, fl_ref, rows_in, rows_out):
        tc = t_ref[rows_in[0], :]
        ts = t_ref[rows_in[1], :]
        tt = jnp.concatenate([tc, ts], axis=0)
        th = tt.astype(BF16)
        tl = (tt - th.astype(F32)).astype(BF16)
        out = _dot(fh_ref[...], th) + _dot(fl_ref[...], th) + _dot(fh_ref[...], tl)
        o_ref[rows_out, :] = out.astype(BF16)

    is_ctx = pl.program_id(0) < n_ctx_tiles

    @pl.when(is_ctx)
    def _():
        for s in range(tm // seq):
            r0 = slice(s * seq, (s + 1) * seq)
            r1 = slice(tm + s * seq, tm + (s + 1) * seq)
            position_dft(fsh_ref, fsl_ref, (r0, r1), r0)

    @pl.when(jnp.logical_not(is_ctx))
    def _():
        position_dft(flh_ref, fll_ref, (slice(0, tm), slice(tm, 2 * tm)), slice(0, tm))


def _fnet(z, consts, mix, n_ctx_tiles, seq, tm):
    m = z.shape[0]
    cb = 2 * A_WIDTH // B_WIDTH
    full = lambda a: pl.BlockSpec(a.shape, lambda i: (0,) * a.ndim)
    return pl.pallas_call(
        functools.partial(_fnet_kernel, n_ctx_tiles=n_ctx_tiles, seq=seq),
        grid=(m // tm,),
        in_specs=([pl.BlockSpec((tm, B_WIDTH), lambda i: (i, cb))] + [full(a) for a in consts]
                  + [pl.BlockSpec(memory_space=pl.ANY)]),
        out_specs=pl.BlockSpec((tm, B_WIDTH), lambda i: (i, A_WIDTH // B_WIDTH)),
        out_shape=jax.ShapeDtypeStruct(mix.shape, mix.dtype),
        input_output_aliases={1 + len(consts): 0},
        scratch_shapes=[pltpu.VMEM((2 * tm, B_WIDTH), F32)],
        compiler_params=_cparams(("arbitrary",)),
        name="fnet",
    )(z, *consts, mix)


def _seg_sum(x, e_ref):
    w = e_ref.shape[1]
    parts = [_dot_ones_rhs(x[:, b * w:(b + 1) * w], e_ref[...]) for b in range(x.shape[1] // w)]
    return parts[0] if len(parts) == 1 else jnp.concatenate(parts, axis=1)


def _token_shift_mix(z_ref, zc_ref, zp_ref, zn_ref, mu_ref, bounds, period, up_ok, dn_ok):
    tm = zc_ref.shape[0]
    row = lax.broadcasted_iota(jnp.int32, (tm, 1), 0)
    pos = row & (period - 1)
    lanes = 128

    def neighbour(kind, src):
        xs = zc_ref[:, src]
        if kind == 0:
            return jnp.where(pos == 0, 0.0, pltpu.roll(xs, 1, 0))
        if kind == 1:
            return jnp.where(pos == period - 1, 0.0, pltpu.roll(xs, tm - 1, 0))
        if kind == 2:
            return jnp.concatenate([jnp.where(up_ok, zp_ref[:, src], 0.0), xs[:tm - GRID_W]], axis=0)
        return jnp.concatenate([xs[GRID_W:], jnp.where(dn_ok, zn_ref[:, src], 0.0)], axis=0)

    edges = sorted({0, C_PAD} | {b // lanes * lanes for b in bounds} | {-(-b // lanes) * lanes for b in bounds})
    for lo, hi in zip(edges[:-1], edges[1:]):
        cols = slice(lo, hi)
        src = slice(Z_C0 + lo, Z_C0 + hi)
        kinds = [n for n in range(len(bounds) + 1)
                 if (bounds[n - 1] if n else 0) < hi and lo < (bounds[n] if n < len(bounds) else C_PAD)]
        zs = neighbour(kinds[-1], src)
        lane = lo + lax.broadcasted_iota(jnp.int32, (1, hi - lo), 1)
        for n in reversed(kinds[:-1]):
            zs = jnp.where(lane < bounds[n], neighbour(n, src), zs)
        xs = zc_ref[:, src]
        z_ref[:, cols] = xs + (zs - xs) * mu_ref[:, cols]


def _prep_kernel(zc_ref, zp_ref, zn_ref, mu_ref, w0_ref, w2_ref, a0_ref, a2_ref, kk_ref, ka_ref, rk_ref,
                 g2_ref, e_ref,
                 rvk_o, kf_o, kb_o, ld_o, bv_o, g_o, z_ref, *,
                 n_ctx_tiles, tiles_per_lat):
    i = pl.program_id(0)
    tm = zc_ref.shape[0]
    is_ctx = i < n_ctx_tiles
    q = (i - n_ctx_tiles) % tiles_per_lat
    half = C_IN // 2
    quarter = C_IN // 4

    @pl.when(is_ctx)
    def _():
        _token_shift_mix(z_ref, zc_ref, zp_ref, zn_ref, mu_ref, (half,), tm, False, False)

    @pl.when(jnp.logical_not(is_ctx))
    def _():
        _token_shift_mix(z_ref, zc_ref, zp_ref, zn_ref, mu_ref, (quarter, 2 * quarter, 3 * quarter), GRID_W,
                         q > 0, q < tiles_per_lat - 1)

    cw = C_WIDTH
    r = z_ref[:, 0:cw]
    k = z_ref[:, cw:2 * cw]
    v = z_ref[:, 2 * cw:3 * cw]
    wd = z_ref[:, 3 * cw:3 * cw + 128]
    ad = z_ref[:, 3 * cw + 128:3 * cw + 256]
    gd = z_ref[:, 3 * cw + 256:C_PAD]

    uw = _dot3_presplit(jnp.tanh(wd), w2_ref[...])
    ua = _dot3_presplit(ad, a2_ref[...])
    kk_raw = k * kk_ref[...]
    ss = _seg_sum(kk_raw * kk_raw, e_ref)
    kk = kk_raw * lax.rsqrt(jnp.maximum(ss, 1e-24))
    rvk_o[:, 0:cw] = r.astype(BF16)
    rvk_o[:, cw:2 * cw] = v.astype(BF16)
    rvk_o[:, 2 * cw:3 * cw] = kk.astype(BF16)
    km_sum = None
    for d, kd_o in enumerate((kf_o, kb_o)):
        u = uw[:, d * cw:(d + 1) * cw] + w0_ref[d:d + 1, :]
        ld_o[:, d * cw:(d + 1) * cw] = -float(np.exp(-0.5)) * _sigmoid(u)
        a = _sigmoid(ua[:, d * cw:(d + 1) * cw] + a0_ref[d:d + 1, :])
        km = k * (1.0 + (a - 1.0) * ka_ref[...])
        kd_o[:, 0:cw] = km.astype(BF16)
        kd_o[:, cw:2 * cw] = (kk * a).astype(BF16)
        km_sum = km if km_sum is None else km_sum + km
    bonus = _seg_sum(r * km_sum * rk_ref[...], e_ref)
    bv_o[...] = (bonus * v).astype(BF16)
    g_o[...] = _dot(_sigmoid(gd).astype(BF16), g2_ref[...]).astype(BF16)


def _rwkv_prep(z, p, layer, n_ctx_tiles, tiles_per_lat, tm):
    m = z.shape[0]
    hb = tm // GRID_W
    n_halo = m // GRID_W
    lay = lambda *shape: pl.BlockSpec((None,) + shape, lambda i: (layer,) + (0,) * len(shape))
    outs = [(3, BF16), (2, BF16), (2, BF16), (2, F32), (1, BF16), (1, BF16)]
    return pl.pallas_call(
        functools.partial(_prep_kernel, n_ctx_tiles=n_ctx_tiles, tiles_per_lat=tiles_per_lat),
        grid=(m // tm,),
        in_specs=[
            pl.BlockSpec((tm, Z_COLS), lambda i: (i, 0)),
            pl.BlockSpec((GRID_W, Z_COLS), lambda i: (jnp.where(i < n_ctx_tiles, 0, i * hb - 1), 0)),
            pl.BlockSpec((GRID_W, Z_COLS),
                         lambda i: (jnp.where(i < n_ctx_tiles, 0, jnp.minimum(i * hb + hb, n_halo - 1)), 0)),
            lay(1, C_PAD),
            lay(2, C_WIDTH),
            lay(384, 2 * C_WIDTH),
            lay(2, C_WIDTH),
            lay(384, 2 * C_WIDTH),
            lay(1, C_WIDTH),
            lay(1, C_WIDTH),
            lay(1, C_WIDTH),
            lay(256, C_WIDTH),
            pl.BlockSpec((512, 256), lambda i: (0, 0)),
        ],
        out_specs=[pl.BlockSpec((tm, n * C_WIDTH), lambda i: (i, 0)) for n, _ in outs],
        out_shape=[jax.ShapeDtypeStruct((m, n * C_WIDTH), dt) for n, dt in outs],
        scratch_shapes=[pltpu.VMEM((tm, C_PAD), F32)],
        compiler_params=_cparams(("arbitrary",)),
        name="rwkv_prep",
    )(z, z, z, p["mu"], p["w0"], p["w2"], p["a0"], p["a2"], p["k_k"], p["k_a"], p["r_k"], p["g2"],
      p["e256"])


def _bmm(a, b):
    return jnp.einsum("uik,ukj->uij", a.astype(BF16), b.astype(BF16), preferred_element_type=F32)


def _bmm_nt(a, b):
    return jnp.einsum("uik,ujk->uij", a.astype(BF16), b.astype(BF16), preferred_element_type=F32)


def _block_diag(y, bd_mask):
    return jnp.where(bd_mask, jnp.concatenate([y] * UNIT_HEADS, axis=1), jnp.zeros((), y.dtype))


def _to_units(x):
    return jnp.stack([x[:, p * UNIT_W:(p + 1) * UNIT_W] for p in range(N_UNITS)], axis=0)


def _scan_chunks(streams, hs):
    t = SCAN_T
    w = UNIT_W
    ti = lax.broadcasted_iota(jnp.int32, (t, w), 0)
    si = lax.broadcasted_iota(jnp.int32, (t, w), 1) & (HEAD_DIM - 1)
    eye = jnp.where(si == ti, 1.0, 0.0)
    t_row = lax.broadcasted_iota(jnp.int32, (t, 3 * t), 0)
    t_col = lax.broadcasted_iota(jnp.int32, (t, 3 * t), 1) & (t - 1)
    bd_mask = (lax.broadcasted_iota(jnp.int32, (w, w), 0) // HEAD_DIM
               == lax.broadcasted_iota(jnp.int32, (w, w), 1) // HEAD_DIM)

    q, ai, ki, vb, vf, lhs, p_tot = [], [], [], [], [], [], []
    for reverse, r, v, kk, ld, km, ka in streams:
        tri = jnp.where((t_col >= t_row) if reverse else (t_col <= t_row), 1.0, 0.0).astype(BF16)
        cum = _dot_exact_lhs(tri, ld)
        c_end = cum[0:1, :] if reverse else cum[t - 1:t, :]
        p_end = jnp.exp(c_end - cum)
        p_inv = jnp.exp(-cum)
        q.append(_to_units(jnp.concatenate([kk * jnp.exp(cum - ld), r * jnp.exp(cum)], axis=0).astype(BF16)))
        ai.append(_to_units((ka * p_inv).astype(BF16)))
        ki.append(_to_units((km * p_inv).astype(BF16)))
        vb.append(_to_units(v.astype(BF16)))
        vf.append(_to_units(v))
        lhs.append(_to_units(jnp.concatenate([ka * p_end, km * p_end], axis=0).astype(BF16)))
        p_tot.append(_to_units(jnp.exp(c_end)))
    q, ai, ki, vb, vf, lhs, p_tot = (jnp.concatenate(a, axis=0) for a in (q, ai, ki, vb, vf, lhs, p_tot))

    def masked(x, strictly):
        parts = []
        for n, stream in enumerate(streams):
            if stream[0]:
                keep = si > ti if strictly else si >= ti
            else:
                keep = si < ti if strictly else si <= ti
            parts.append(jnp.where(keep, x[n * N_UNITS:(n + 1) * N_UNITS], 0.0))
        return jnp.concatenate(parts, axis=0)

    rr = jnp.concatenate([_block_diag(ai, bd_mask), _block_diag(ki, bd_mask)], axis=1)
    sc = _bmm_nt(q, rr)
    la = masked(sc[:, 0:t, 0:w], True)
    lk = masked(sc[:, 0:t, w:2 * w], True)
    ma = masked(sc[:, t:2 * t, 0:w], False)
    mk = masked(sc[:, t:2 * t, w:2 * w], False)

    n_pow = -la
    x_inv = eye + n_pow
    n_pow = _bmm(n_pow, _block_diag(n_pow.astype(BF16), bd_mask))
    for _ in range(int(np.log2(t)) - 2):
        both = _bmm(jnp.concatenate([n_pow, x_inv], axis=1), _block_diag(n_pow.astype(BF16), bd_mask))
        x_inv = x_inv + both[:, t:2 * t]
        n_pow = both[:, 0:t]
    x_inv = x_inv + _bmm(x_inv, _block_diag(n_pow.astype(BF16), bd_mask))

    hb = _bmm_nt(q, hs)
    lv = _bmm(jnp.concatenate([lk, mk], axis=1), _block_diag(vb, bd_mask))
    u = -_bmm(x_inv, _block_diag((hb[:, 0:t] + lv[:, 0:t]).astype(BF16), bd_mask))
    y = hb[:, t:2 * t] + lv[:, t:2 * t] + _bmm(ma, _block_diag(u.astype(BF16), bd_mask))
    rhs_t = jnp.swapaxes(jnp.concatenate([u, vf], axis=1), 1, 2)
    upd = _bmm(rhs_t, lhs)
    hs_new = p_tot * hs + jnp.where(bd_mask, upd, 0.0)
    return y, hs_new


def _scan_kernel(*refs, has_s0, emit_state, aliased_state):
    ins = refs[:6]
    pos = 6
    s0_ref = None
    if has_s0:
        s0_ref = refs[pos]
        pos += 1
    if aliased_state:
        pos += 1
    y_refs = refs[pos:pos + 2]
    pos += 2
    st_ref = None
    if emit_state:
        st_ref = refs[pos]
        pos += 1
    h_ref = refs[pos]
    c = pl.program_id(1)
    hd = HEAD_DIM

    n_group = h_ref.shape[0]

    @pl.when(c == 0)
    def _():
        if has_s0:
            zero = jnp.zeros((hd, hd), F32)
            for s in range(n_group):
                for d in range(2):
                    for p in range(N_UNITS):
                        rows = [jnp.concatenate([s0_ref[s, d, UNIT_HEADS * p + a] if a == b else zero
                                                 for b in range(UNIT_HEADS)], axis=1) for a in range(UNIT_HEADS)]
                        h_ref[s, d, p] = jnp.concatenate(rows, axis=0)
        else:
            h_ref[...] = jnp.zeros_like(h_ref)

    streams = []
    for s in range(n_group):
        for d in range(2):
            rvk_ref, kd_ref, ld_ref = ins[3 * d:3 * d + 3]
            r, v, kk = (rvk_ref[s, :, n * C_WIDTH:(n + 1) * C_WIDTH].astype(F32) for n in range(3))
            km, ka = (kd_ref[s, :, n * C_WIDTH:(n + 1) * C_WIDTH].astype(F32) for n in range(2))
            streams.append((d == 1, r, v, kk, ld_ref[s], km, ka))
    n_all = len(streams) * N_UNITS
    y, hs_new = _scan_chunks(streams, h_ref[...].reshape(n_all, UNIT_W, UNIT_W))
    h_ref[...] = hs_new.reshape(h_ref.shape)
    for s in range(n_group):
        for d, y_ref in enumerate(y_refs):
            for p in range(N_UNITS):
                y_ref[s, :, p * UNIT_W:(p + 1) * UNIT_W] = y[(2 * s + d) * N_UNITS + p].astype(BF16)

    if emit_state:
        @pl.when(c == pl.num_programs(1) - 1)
        def _():
            for s in range(n_group):
                for d in range(2):
                    for p in range(N_UNITS):
                        hs = h_ref[s, d, p]
                        for a in range(UNIT_HEADS):
                            st_ref[s, d, UNIT_HEADS * p + a] = hs[a * hd:(a + 1) * hd, a * hd:(a + 1) * hd]


def _rwkv_scan(arrs, n_seq, n_chunk, row0, layer, group, s0=None, state_shape=None, state_prev=None):
    seq_len = n_chunk * SCAN_T
    m_all = arrs[0].shape[0]
    assert row0 % (seq_len * group) == 0 and m_all % seq_len == 0 and n_seq % group == 0
    rvk, kf, kb, ld = (a.reshape(m_all // seq_len, seq_len, a.shape[1]) for a in arrs)
    g0 = row0 // (seq_len * group)
    fwd = lambda b, c: (b, c, 0)
    bwd = lambda b, c: (b, n_chunk - 1 - c, 0)
    blk = lambda imap, n=1: pl.BlockSpec((group, SCAN_T, n * C_WIDTH), imap)
    in_specs = [blk(lambda b, c: (g0 + b, c, 0), 3), blk(lambda b, c: (g0 + b, c, 0), 2),
                blk(lambda b, c: (g0 + b, c, 0)),
                blk(lambda b, c: (g0 + b, n_chunk - 1 - c, 0), 3), blk(lambda b, c: (g0 + b, n_chunk - 1 - c, 0), 2),
                blk(lambda b, c: (g0 + b, n_chunk - 1 - c, 1))]
    args = [rvk, kf, ld, rvk, kb, ld]
    st_spec = pl.BlockSpec((group, None, 2, N_HEADS, HEAD_DIM, HEAD_DIM), lambda b, c: (b, layer, 0, 0, 0, 0))
    if s0 is not None:
        in_specs.append(st_spec)
        args.append(s0)
    aliases = {}
    if state_prev is not None:
        aliases = {len(args): 2}
        in_specs.append(pl.BlockSpec(memory_space=pl.ANY))
        args.append(state_prev)
    y_shape = jax.ShapeDtypeStruct((n_seq, seq_len, C_WIDTH), BF16)
    out_specs = [blk(fwd), blk(bwd)]
    out_shape = [y_shape, y_shape]
    if state_shape is not None:
        out_specs.append(st_spec)
        out_shape.append(jax.ShapeDtypeStruct(state_shape, F32))
    outs = pl.pallas_call(
        functools.partial(_scan_kernel, has_s0=s0 is not None, emit_state=state_shape is not None,
                          aliased_state=state_prev is not None),
        grid=(n_seq // group, n_chunk),
        in_specs=in_specs,
        out_specs=out_specs,
        out_shape=out_shape,
        input_output_aliases=aliases,
        scratch_shapes=[pltpu.VMEM((group, 2, N_UNITS, UNIT_W, UNIT_W), F32)],
        compiler_params=_cparams(("arbitrary", "arbitrary")),
        name="rwkv_scan",
    )(*args)
    ys = [y.reshape(n_seq * seq_len, C_WIDTH) for y in outs[:2]]
    return ys + list(outs[2:])


def _post_kernel(yf1_ref, yb1_ref, yf2_ref, yb2_ref, bv_ref, g_ref, lg_ref, lb_ref, e_ref, mix_ref, o_ref, *,
                 n_ctx_tiles):
    del mix_ref
    is_ctx = pl.program_id(0) < n_ctx_tiles
    f32 = lambda ref: ref[...].astype(F32)
    y = jnp.where(is_ctx, f32(yf1_ref) + f32(yb1_ref), f32(yf2_ref) + f32(yb2_ref))
    inv = 1.0 / HEAD_DIM
    m = _seg_sum(y, e_ref) * inv
    yc = y - m
    var = _seg_sum(yc * yc, e_ref) * inv
    yn = yc * lax.rsqrt(var + LNX_EPS) * lg_ref[...] + lb_ref[...]
    o_ref[...] = ((yn + bv_ref[...]) * g_ref[...]).astype(BF16)


def _rwkv_post(y_ctx, y_lat, bv, g, lnx_g, lnx_b, e256, mix, layer, n_ctx_tiles, tm):
    m = bv.shape[0]
    n_tiles = m // tm
    row = pl.BlockSpec((tm, C_WIDTH), lambda i: (i, 0))
    row_ctx = pl.BlockSpec((tm, C_WIDTH), lambda i: (jnp.minimum(i, n_ctx_tiles - 1), 0))
    row_lat = pl.BlockSpec((tm, C_WIDTH), lambda i: (jnp.maximum(i - n_ctx_tiles, 0), 0))
    vec = pl.BlockSpec((None, 1, C_WIDTH), lambda i: (layer, 0, 0))
    return pl.pallas_call(
        functools.partial(_post_kernel, n_ctx_tiles=n_ctx_tiles),
        grid=(n_tiles,),
        in_specs=[row_ctx, row_ctx, row_lat, row_lat, row, row, vec, vec,
                  pl.BlockSpec((512, 256), lambda i: (0, 0)), pl.BlockSpec(memory_space=pl.ANY)],
        out_specs=pl.BlockSpec((tm, C_WIDTH), lambda i: (i, (A_WIDTH + B_WIDTH) // C_WIDTH)),
        out_shape=jax.ShapeDtypeStruct(mix.shape, mix.dtype),
        input_output_aliases={9: 0},
        compiler_params=_cparams(("arbitrary",)),
        name="rwkv_post",
    )(y_ctx[0], y_ctx[1], y_lat[0], y_lat[1], bv, g, lnx_g, lnx_b, e256, mix)


def kernel(x_prompt, x_sample, state_wkv, c, c_ctx, norm_g, w_mod, b_mod, ffn_w_in, ffn_w_out, w_in, w_out,
           sgu_ln_g, sgu_ln_b, sgu_w, sgu_b, shift_mu, decay_w0, decay_w2, iclr_a0, iclr_a2, k_k, k_a, r_k,
           gate_w2, lnx_g, lnx_b, final_g):
    batch, seq, d = x_prompt.shape
    dec_batch, dec_seq, _ = x_sample.shape
    depth = w_mod.shape[0]
    assert d == D_MODEL and dec_batch + 1 <= MOD_ROWS
    m_ctx = batch * seq
    tm = 1024
    ts = 512
    tp = 256
    assert seq == tp and dec_seq == tm and m_ctx % tm == 0 and dec_seq % GRID_W == 0

    def row_of_tile_fn(rows):
        n_ctx = m_ctx // rows
        per_lat = dec_seq // rows
        return lambda i: jnp.where(i < n_ctx, 0, 1 + (i - n_ctx) // per_lat)

    zeros_cw = jnp.zeros((depth, DECAY_RANK, C_WIDTH), F32)

    def both_dirs(w):
        top = jnp.concatenate([w[:, 0], zeros_cw], axis=-1)
        bot = jnp.concatenate([zeros_cw, w[:, 1]], axis=-1)
        return jnp.concatenate([top, bot], axis=1)

    e_np = (np.arange(256)[:, None] // HEAD_DIM == np.arange(256)[None, :] // HEAD_DIM)
    prep_params = {
        "mu": jnp.pad(shift_mu, ((0, 0), (0, C_PAD - C_IN))).reshape(depth, 1, C_PAD),
        "w0": decay_w0, "w2": _split_rows3(both_dirs(decay_w2)), "a0": iclr_a0, "a2": _split_rows3(both_dirs(iclr_a2)),
        "k_k": k_k.reshape(depth, 1, C_WIDTH), "k_a": k_a.reshape(depth, 1, C_WIDTH),
        "r_k": r_k.reshape(depth, 1, C_WIDTH),
        "g2": jnp.pad(gate_w2, ((0, 0), (0, 256 - GATE_RANK), (0, 0))).astype(BF16),
        "e256": jnp.asarray(np.concatenate([e_np, e_np], axis=0), BF16),
    }
    cd, sd = _dft_mats(B_GROUP_DIM)
    cs, ss = _dft_mats(seq)
    cl, sl = _dft_mats(dec_seq)
    fnet_consts = (_hi_lo(np.concatenate([cd, sd], axis=1)) + _hi_lo(np.concatenate([cs, -ss], axis=1))
                   + _hi_lo(np.concatenate([cl, -sl], axis=1)))
    sgu_bias = jnp.repeat(jnp.swapaxes(sgu_b, 1, 2), A_WIDTH // A_HEADS, axis=2)
    sgu_w_b = sgu_w.astype(BF16)
    ln_g = sgu_ln_g.reshape(depth, 1, A_WIDTH)
    ln_b = sgu_ln_b.reshape(depth, 1, A_WIDTH)
    lnx_g3 = lnx_g.reshape(depth, 1, C_WIDTH)
    lnx_b3 = lnx_b.reshape(depth, 1, C_WIDTH)
    norm_g4 = norm_g.reshape(depth, 3, 1, d)

    cond = jnp.concatenate([c_ctx[None, :], c, jnp.zeros((MOD_ROWS - 1 - dec_batch, d), F32)], axis=0)
    mod = _modulation(cond, w_mod, b_mod).reshape(depth, MOD_ROWS, N_MOD, 1, d)

    m = m_ctx + dec_batch * dec_seq
    n_ctx_t = m_ctx // tm
    n_lat_t = dec_batch * dec_seq // tm
    rot = row_of_tile_fn(tm)
    ctx_row = lambda i: 0
    lat_row = lambda i: 1 + i
    n_chunk_ctx = seq // SCAN_T
    n_chunk_lat = dec_seq // SCAN_T
    state_shape = (batch, depth, 2, N_HEADS, HEAD_DIM, HEAD_DIM)
    tk = 1024
    scan_group = lambda n_seq: 4 if n_seq % 4 == 0 else 1
    w_in_t = jnp.swapaxes(w_in, 1, 2).astype(BF16)
    w_out_b = w_out.astype(BF16)
    new_state = None
    x = None
    for l in range(depth):
        ffn_args = (mod, norm_g4, ffn_w_in, ffn_w_out, l)
        if l == 0:
            x = _ffn(x_prompt.reshape(m_ctx, d), *ffn_args, 0, 0, ctx_row, tm, n_ctx_t, out_rows=m)
            x = _ffn(x_sample.reshape(m - m_ctx, d), *ffn_args, 0, 0, lat_row, tm, n_lat_t, out_tile0=n_ctx_t,
                     out_rows=m, out_prev=x)
        else:
            x = _ffn(x, *ffn_args, 0, 0, rot, tm, n_ctx_t + n_lat_t)
        z = _in_proj(x, mod, norm_g4, w_in_t, l, rot, tm)
        mix = _sgu(z, ln_g, ln_b, sgu_w_b, sgu_bias, l, ts)
        mix = _fnet(z, fnet_consts, mix, m_ctx // tm, seq, tm)
        *scan_in, bv, g = _rwkv_prep(z, prep_params, l, m_ctx // tp, dec_seq // tp, tp)
        yf_c, yb_c, new_state = _rwkv_scan(scan_in, batch, n_chunk_ctx, 0, l, scan_group(batch),
                                           state_shape=state_shape, state_prev=new_state)
        yf_l, yb_l = _rwkv_scan(scan_in, dec_batch, n_chunk_lat, m_ctx, l, scan_group(dec_batch), s0=state_wkv)
        mix = _rwkv_post((yf_c, yb_c), (yf_l, yb_l), bv, g, lnx_g3, lnx_b3, prep_params["e256"], mix, l,
                         m_ctx // ts, ts)
        x = _resid_matmul(x, mod, mix, w_out_b, pl.BlockSpec((None, tk, D_MODEL), lambda i, k: (l, k, 0)),
                          l, 5, 1.0, rot, tm, tk, "mix_out")
        if l < depth - 1:
            x = _ffn(x, *ffn_args, 2, 1, rot, tm, n_ctx_t + n_lat_t)
    fg = final_g.reshape(1, d)
    y_ctx = _ffn(x, *ffn_args, 2, 1, ctx_row, tm, n_ctx_t, final_g=fg)
    y_lat = _ffn(x, *ffn_args, 2, 1, lat_row, tm, n_lat_t, in_tile0=n_ctx_t, final_g=fg)
    return (y_ctx.reshape(batch, seq, d), y_lat.reshape(dec_batch, dec_seq, d), new_state)
```

```python
import functools

import numpy as np
import jax
import jax.numpy as jnp
from jax import lax
from jax.experimental import pallas as pl
from jax.experimental.pallas import tpu as pltpu

F32 = jnp.float32
BF16 = jnp.bfloat16

D_MODEL = 2048
GRID_W = 64
SGU_CHUNK = 128
A_HEADS = 4
A_WIDTH = 512
B_WIDTH = 512
B_GROUP_DIM = 128
C_WIDTH = 1024
MIX_WIDTH = A_WIDTH + B_WIDTH + C_WIDTH
HEAD_DIM = 64
N_HEADS = 16
UNIT_HEADS = 2
UNIT_W = UNIT_HEADS * HEAD_DIM
N_UNITS = N_HEADS // UNIT_HEADS
DECAY_RANK = 64
GATE_RANK = 160
C_IN = 3488
C_PAD = 3584
IN_COLS = 2 * A_WIDTH + B_WIDTH + C_IN
Z_BLOCK = 1024
Z_C0 = 2 * A_WIDTH + B_WIDTH
Z_COLS = Z_C0 + C_PAD
D_FF = 5632
N_MOD = 9
RMS_EPS = 1e-6
LN_EPS = 1e-5
LNX_EPS = 64e-5
SCAN_T = 64
MOD_ROWS = 8

VMEM_LIMIT = 56 * 1024 * 1024
FFN_VMEM_LIMIT = 60 * 1024 * 1024


def _cparams(sem):
    return pltpu.CompilerParams(dimension_semantics=sem, vmem_limit_bytes=VMEM_LIMIT)


def _dot(a, b):
    return jnp.dot(a, b, preferred_element_type=F32)


def _split3(x):
    hi = x.astype(BF16)
    r1 = x - hi.astype(F32)
    mid = r1.astype(BF16)
    lo = (r1 - mid.astype(F32)).astype(BF16)
    return hi, mid, lo


def _dot_ones_rhs(x, e2):
    hi = x.astype(BF16)
    lo = (x - hi.astype(F32)).astype(BF16)
    return _dot(jnp.concatenate([hi, lo], axis=1), e2)


def _dot_exact_lhs(e3, x):
    return _dot(e3, jnp.concatenate(_split3(x), axis=0))


def _split_rows3(b):
    bh = b.astype(BF16)
    bl = (b - bh.astype(F32)).astype(BF16)
    return jnp.concatenate([bh, bh, bl], axis=-2)


def _dot3_presplit(a, b3):
    ah = a.astype(BF16)
    al = (a - ah.astype(F32)).astype(BF16)
    return _dot(jnp.concatenate([ah, al, ah], axis=1), b3)


def _sigmoid(x):
    return 0.5 * jnp.tanh(0.5 * x) + 0.5


def _silu(x):
    return x * _sigmoid(x)


def _gelu_tanh(x):
    return 0.5 * x * (1.0 + jnp.tanh(0.7978845608028654 * (x + 0.044715 * (x * x * x))))


def _mod_kernel(c_ref, w_ref, b_ref, o_ref):
    s = _silu(c_ref[...]).astype(BF16)
    o_ref[...] = _dot(s, w_ref[...].astype(BF16)) + b_ref[...]


def _modulation(cond, w_mod, b_mod):
    depth = w_mod.shape[0]
    n = w_mod.shape[2]
    tn = 1024
    return pl.pallas_call(
        _mod_kernel,
        grid=(depth, n // tn),
        in_specs=[
            pl.BlockSpec((MOD_ROWS, D_MODEL), lambda l, j: (0, 0)),
            pl.BlockSpec((None, D_MODEL, tn), lambda l, j: (l, 0, j)),
            pl.BlockSpec((None, 1, tn), lambda l, j: (l, 0, j)),
        ],
        out_specs=pl.BlockSpec((None, MOD_ROWS, tn), lambda l, j: (l, 0, j)),
        out_shape=jax.ShapeDtypeStruct((depth, MOD_ROWS, n), F32),
        compiler_params=_cparams(("arbitrary", "arbitrary")),
        name="modulation",
    )(cond, w_mod, b_mod.reshape(depth, 1, n))


def _mod_spec(layer, slot, row_of_tile):
    return pl.BlockSpec((None, None, None, 1, D_MODEL),
                        lambda i, *_: (layer, row_of_tile(i), slot, 0, 0))


def _norm_spec(layer, slot):
    return pl.BlockSpec((None, None, 1, D_MODEL), lambda i, *_: (layer, slot, 0, 0))


def _modulated_norm(x_ref, g_ref, sc_ref, sh_ref):
    x = x_ref[...]
    rs = lax.rsqrt(jnp.mean(x * x, axis=-1, keepdims=True) + RMS_EPS)
    gain = g_ref[...] * (1.0 + sc_ref[...])
    return ((x * rs) * gain + sh_ref[...]).astype(BF16)


def _in_proj_kernel(x_ref, sh_ref, sc_ref, g_ref, wt_ref, o_ref, h_ref, *, valid_cols):
    j = pl.program_id(1)

    def step(first):
        if first:
            h = _modulated_norm(x_ref, g_ref, sc_ref, sh_ref)
            h_ref[...] = h
        else:
            h = h_ref[...]
        out = lax.dot_general(h, wt_ref[...].astype(BF16), (((1,), (1,)), ((), ())),
                              preferred_element_type=F32)
        tn = out.shape[1]
        if valid_cols % tn:
            col = j * tn + lax.broadcasted_iota(jnp.int32, (1, tn), 1)
            out = jnp.where(col < valid_cols, out, 0.0)
        o_ref[...] = out

    pl.when(j == 0)(lambda: step(True))
    pl.when(j > 0)(lambda: step(False))


def _in_proj(x, mod, norm_g, w_in_t, layer, row_of_tile, tm):
    m = x.shape[0]
    nb = Z_COLS // Z_BLOCK
    assert -(-IN_COLS // Z_BLOCK) == nb
    return pl.pallas_call(
        functools.partial(_in_proj_kernel, valid_cols=IN_COLS),
        grid=(m // tm, nb),
        in_specs=[
            pl.BlockSpec((tm, D_MODEL), lambda i, j: (i, 0)),
            _mod_spec(layer, 3, row_of_tile),
            _mod_spec(layer, 4, row_of_tile),
            _norm_spec(layer, 1),
            pl.BlockSpec((None, Z_BLOCK, D_MODEL), lambda i, j: (layer, j, 0)),
        ],
        out_specs=pl.BlockSpec((tm, Z_BLOCK), lambda i, j: (i, j)),
        out_shape=jax.ShapeDtypeStruct((m, Z_COLS), F32),
        scratch_shapes=[pltpu.VMEM((tm, D_MODEL), BF16)],
        compiler_params=_cparams(("arbitrary", "arbitrary")),
        name="in_proj",
    )(x, mod, mod, norm_g, w_in_t)


def _ffn_kernel(x_ref, sh_ref, sc_ref, gt_ref, g_ref, wg_ref, wu_ref, wo_ref, *rest, tn, final_norm):
    fg_ref = rest[0] if final_norm else None
    o_ref, h_ref = rest[-2:]
    j = pl.program_id(1)
    last = pl.num_programs(1) - 1

    def step(first, final):
        if first:
            h = _modulated_norm(x_ref, g_ref, sc_ref, sh_ref)
            h_ref[...] = h
        else:
            h = h_ref[...]
        gate = _dot(h, wg_ref[...].astype(BF16))
        up = _dot(h, wu_ref[...].astype(BF16))
        a = (_silu(gate) * up).astype(BF16)
        for n in range(o_ref.shape[1] // tn):
            cols = slice(n * tn, (n + 1) * tn)
            acc = _dot(a, wo_ref[:, cols].astype(BF16))
            if not first:
                acc = o_ref[:, cols] + acc
            if final:
                acc = x_ref[:, cols] + (0.5 * gt_ref[:, cols]) * acc
            o_ref[:, cols] = acc
        if final and final_norm:
            y = o_ref[...]
            rs = lax.rsqrt(jnp.mean(y * y, axis=-1, keepdims=True) + RMS_EPS)
            o_ref[...] = y * rs * fg_ref[...]

    pl.when(j == 0)(lambda: step(True, False))
    pl.when(jnp.logical_and(j > 0, j < last))(lambda: step(False, False))
    pl.when(j == last)(lambda: step(False, True))


def _ffn(x, mod, norm_g, w_in, w_out, layer, sub, ffn_idx, row_of_tile, tm, n_tiles, in_tile0=0, out_tile0=0,
         out_rows=None, out_prev=None, final_g=None, tf=256):
    nf = D_FF // tf
    out_rows = n_tiles * tm if out_rows is None else out_rows
    args = [x, mod, mod, mod, norm_g, w_in, w_in, w_out]
    in_specs = [
        pl.BlockSpec((tm, D_MODEL), lambda i, j: (in_tile0 + i, 0)),
        _mod_spec(layer, 3 * sub, row_of_tile),
        _mod_spec(layer, 3 * sub + 1, row_of_tile),
        _mod_spec(layer, 3 * sub + 2, row_of_tile),
        _norm_spec(layer, sub),
        pl.BlockSpec((None, None, D_MODEL, tf), lambda i, j: (layer, ffn_idx, 0, j)),
        pl.BlockSpec((None, None, D_MODEL, tf), lambda i, j: (layer, ffn_idx, 0, nf + j)),
        pl.BlockSpec((None, None, tf, D_MODEL), lambda i, j: (layer, ffn_idx, j, 0)),
    ]
    if final_g is not None:
        args.append(final_g)
        in_specs.append(pl.BlockSpec((1, D_MODEL), lambda i, j: (0, 0)))
    aliases = {}
    if out_prev is not None:
        aliases = {len(args): 0}
        args.append(out_prev)
        in_specs.append(pl.BlockSpec(memory_space=pl.ANY))
    return pl.pallas_call(
        functools.partial(_ffn_kernel, tn=512, final_norm=final_g is not None),
        grid=(n_tiles, nf),
        in_specs=in_specs,
        out_specs=pl.BlockSpec((tm, D_MODEL), lambda i, j: (out_tile0 + i, 0)),
        out_shape=jax.ShapeDtypeStruct((out_rows, D_MODEL), F32),
        input_output_aliases=aliases,
        scratch_shapes=[pltpu.VMEM((tm, D_MODEL), BF16)],
        compiler_params=pltpu.CompilerParams(dimension_semantics=("arbitrary", "arbitrary"),
                                             vmem_limit_bytes=FFN_VMEM_LIMIT),
        name="ffn",
    )(*args)


def _resid_matmul_kernel(x_ref, gt_ref, a_ref, w_ref, o_ref, *, coef, tn):
    k = pl.program_id(1)
    last = pl.num_programs(1) - 1

    def step(first, final):
        a = a_ref[...]
        for n in range(o_ref.shape[1] // tn):
            cols = slice(n * tn, (n + 1) * tn)
            acc = _dot(a, w_ref[:, cols].astype(BF16))
            if not first:
                acc = o_ref[:, cols] + acc
            if final:
                acc = x_ref[:, cols] + (coef * gt_ref[:, cols]) * acc
            o_ref[:, cols] = acc

    pl.when(k == 0)(lambda: step(True, False))
    pl.when(jnp.logical_and(k > 0, k < last))(lambda: step(False, False))
    pl.when(k == last)(lambda: step(False, True))


def _resid_matmul(x, mod, a, w, w_spec, layer, gate_slot, coef, row_of_tile, tm, tk, name):
    m = x.shape[0]
    return pl.pallas_call(
        functools.partial(_resid_matmul_kernel, coef=coef, tn=512),
        grid=(m // tm, a.shape[1] // tk),
        in_specs=[
            pl.BlockSpec((tm, D_MODEL), lambda i, k: (i, 0)),
            _mod_spec(layer, gate_slot, row_of_tile),
            pl.BlockSpec((tm, tk), lambda i, k: (i, k)),
            w_spec,
        ],
        out_specs=pl.BlockSpec((tm, D_MODEL), lambda i, k: (i, 0)),
        out_shape=jax.ShapeDtypeStruct((m, D_MODEL), F32),
        compiler_params=_cparams(("arbitrary", "arbitrary")),
        name=name,
    )(x, mod, a, w)


def _spatial_gating(zu, zv, lg_ref, lb_ref, w_ref, bias_ref, o_ref):
    u = _gelu_tanh(zu)
    v = _gelu_tanh(zv)
    mu = jnp.mean(v, axis=-1, keepdims=True)
    vc = v - mu
    var = jnp.mean(vc * vc, axis=-1, keepdims=True)
    vn = (vc * lax.rsqrt(var + LN_EPS) * lg_ref[...] + lb_ref[...]).astype(BF16)
    hd = A_WIDTH // A_HEADS
    for c in range(u.shape[0] // SGU_CHUNK):
        rows = slice(c * SGU_CHUNK, (c + 1) * SGU_CHUNK)
        for h in range(A_HEADS):
            cols = slice(h * hd, (h + 1) * hd)
            mixed = _dot(w_ref[h], vn[rows, cols]) + bias_ref[:, cols]
            o_ref[rows, cols] = (u[rows, cols] * mixed).astype(BF16)


def _dft_mats(n):
    idx = np.arange(n)
    ang = 2.0 * np.pi * ((idx[:, None] * idx[None, :]) % n) / n
    return np.cos(ang) / np.sqrt(n), np.sin(ang) / np.sqrt(n)


def _hi_lo(a):
    a32 = jnp.asarray(a, F32)
    hi = a32.astype(BF16)
    lo = (a32 - hi.astype(F32)).astype(BF16)
    return hi, lo


def _fnet_kernel(z_ref, fdh_ref, fdl_ref, fsh_ref, fsl_ref, flh_ref, fll_ref, mix_ref, o_ref, t_ref, *,
                 n_ctx_tiles, seq):
    del mix_ref
    tm = z_ref.shape[0]
    x = z_ref[...]
    xh = x.astype(BF16)
    xl = (x - xh.astype(F32)).astype(BF16)
    gd = B_GROUP_DIM
    for g in range(B_WIDTH // gd):
        cols = slice(g * gd, (g + 1) * gd)
        t = _dot(xh[:, cols], fdh_ref[...]) + _dot(xl[:, cols], fdh_ref[...]) + _dot(xh[:, cols], fdl_ref[...])
        t_ref[0:tm, cols] = t[:, 0:gd]
        t_ref[tm:2 * tm, cols] = t[:, gd:2 * gd]

    def position_dft(fh_ref, fl_ref, rows_in, rows_out):
        tc = t_ref[rows_in[0], :]
        ts = t_ref[rows_in[1], :]
        tt = jnp.concatenate([tc, ts], axis=0)
        th = tt.astype(BF16)
        tl = (tt - th.astype(F32)).astype(BF16)
        out = _dot(fh_ref[...], th) + _dot(fl_ref[...], th) + _dot(fh_ref[...], tl)
        o_ref[rows_out, :] = out.astype(BF16)

    is_ctx = pl.program_id(0) < n_ctx_tiles

    @pl.when(is_ctx)
    def _():
        for s in range(tm // seq):
            r0 = slice(s * seq, (s + 1) * seq)
            r1 = slice(tm + s * seq, tm + (s + 1) * seq)
            position_dft(fsh_ref, fsl_ref, (r0, r1), r0)

    @pl.when(jnp.logical_not(is_ctx))
    def _():
        position_dft(flh_ref, fll_ref, (slice(0, tm), slice(tm, 2 * tm)), slice(0, tm))


def _fnet(z, consts, mix, n_ctx_tiles, seq, tm):
    m = z.shape[0]
    cb = 2 * A_WIDTH // B_WIDTH
    full = lambda a: pl.BlockSpec(a.shape, lambda i: (0,) * a.ndim)
    return pl.pallas_call(
        functools.partial(_fnet_kernel, n_ctx_tiles=n_ctx_tiles, seq=seq),
        grid=(m // tm,),
        in_specs=([pl.BlockSpec((tm, B_WIDTH), lambda i: (i, cb))] + [full(a) for a in consts]
                  + [pl.BlockSpec(memory_space=pl.ANY)]),
        out_specs=pl.BlockSpec((tm, B_WIDTH), lambda i: (i, A_WIDTH // B_WIDTH)),
        out_shape=jax.ShapeDtypeStruct(mix.shape, mix.dtype),
        input_output_aliases={1 + len(consts): 0},
        scratch_shapes=[pltpu.VMEM((2 * tm, B_WIDTH), F32)],
        compiler_params=_cparams(("arbitrary",)),
        name="fnet",
    )(z, *consts, mix)


def _seg_sum(x, e_ref):
    w = e_ref.shape[1]
    parts = [_dot_ones_rhs(x[:, b * w:(b + 1) * w], e_ref[...]) for b in range(x.shape[1] // w)]
    return parts[0] if len(parts) == 1 else jnp.concatenate(parts, axis=1)


def _token_shift_mix(z_ref, zc_ref, zp_ref, zn_ref, mu_ref, bounds, period, up_ok, dn_ok):
    tm = zc_ref.shape[0]
    row = lax.broadcasted_iota(jnp.int32, (tm, 1), 0)
    pos = row & (period - 1)
    lanes = 128

    def neighbour(kind, src):
        xs = zc_ref[:, src]
        if kind == 0:
            return jnp.where(pos == 0, 0.0, pltpu.roll(xs, 1, 0))
        if kind == 1:
            return jnp.where(pos == period - 1, 0.0, pltpu.roll(xs, tm - 1, 0))
        if kind == 2:
            return jnp.concatenate([jnp.where(up_ok, zp_ref[:, src], 0.0), xs[:tm - GRID_W]], axis=0)
        return jnp.concatenate([xs[GRID_W:], jnp.where(dn_ok, zn_ref[:, src], 0.0)], axis=0)

    edges = sorted({0, C_PAD} | {b // lanes * lanes for b in bounds} | {-(-b // lanes) * lanes for b in bounds})
    for lo, hi in zip(edges[:-1], edges[1:]):
        cols = slice(lo, hi)
        src = slice(Z_C0 + lo, Z_C0 + hi)
        kinds = [n for n in range(len(bounds) + 1)
                 if (bounds[n - 1] if n else 0) < hi and lo < (bounds[n] if n < len(bounds) else C_PAD)]
        zs = neighbour(kinds[-1], src)
        lane = lo + lax.broadcasted_iota(jnp.int32, (1, hi - lo), 1)
        for n in reversed(kinds[:-1]):
            zs = jnp.where(lane < bounds[n], neighbour(n, src), zs)
        xs = zc_ref[:, src]
        z_ref[:, cols] = xs + (zs - xs) * mu_ref[:, cols]


def _prep_kernel(zc_ref, zp_ref, zn_ref, mu_ref, w0_ref, w2_ref, a0_ref, a2_ref, kk_ref, ka_ref, rk_ref,
                 g2_ref, e_ref, lg_ref, lb_ref, sw_ref, sb_ref,
                 rvk_o, kf_o, kb_o, ld_o, bv_o, g_o, mix_o, z_ref, *,
                 n_ctx_tiles, tiles_per_lat):
    _spatial_gating(zc_ref[:, 0:A_WIDTH], zc_ref[:, A_WIDTH:2 * A_WIDTH], lg_ref, lb_ref, sw_ref, sb_ref, mix_o)
    i = pl.program_id(0)
    tm = zc_ref.shape[0]
    is_ctx = i < n_ctx_tiles
    q = (i - n_ctx_tiles) % tiles_per_lat
    half = C_IN // 2
    quarter = C_IN // 4

    @pl.when(is_ctx)
    def _():
        _token_shift_mix(z_ref, zc_ref, zp_ref, zn_ref, mu_ref, (half,), tm, False, False)

    @pl.when(jnp.logical_not(is_ctx))
    def _():
        _token_shift_mix(z_ref, zc_ref, zp_ref, zn_ref, mu_ref, (quarter, 2 * quarter, 3 * quarter), GRID_W,
                         q > 0, q < tiles_per_lat - 1)

    cw = C_WIDTH
    r = z_ref[:, 0:cw]
    k = z_ref[:, cw:2 * cw]
    v = z_ref[:, 2 * cw:3 * cw]
    wd = z_ref[:, 3 * cw:3 * cw + 128]
    ad = z_ref[:, 3 * cw + 128:3 * cw + 256]
    gd = z_ref[:, 3 * cw + 256:C_PAD]

    uw = _dot3_presplit(jnp.tanh(wd), w2_ref[...])
    ua = _dot3_presplit(ad, a2_ref[...])
    kk_raw = k * kk_ref[...]
    ss = _seg_sum(kk_raw * kk_raw, e_ref)
    kk = kk_raw * lax.rsqrt(jnp.maximum(ss, 1e-24))
    rvk_o[:, 0:cw] = r.astype(BF16)
    rvk_o[:, cw:2 * cw] = v.astype(BF16)
    rvk_o[:, 2 * cw:3 * cw] = kk.astype(BF16)
    km_sum = None
    for d, kd_o in enumerate((kf_o, kb_o)):
        u = uw[:, d * cw:(d + 1) * cw] + w0_ref[d:d + 1, :]
        ld_o[:, d * cw:(d + 1) * cw] = -float(np.exp(-0.5)) * _sigmoid(u)
        a = _sigmoid(ua[:, d * cw:(d + 1) * cw] + a0_ref[d:d + 1, :])
        km = k * (1.0 + (a - 1.0) * ka_ref[...])
        kd_o[:, 0:cw] = km.astype(BF16)
        kd_o[:, cw:2 * cw] = (kk * a).astype(BF16)
        km_sum = km if km_sum is None else km_sum + km
    bonus = _seg_sum(r * km_sum * rk_ref[...], e_ref)
    bv_o[...] = (bonus * v).astype(BF16)
    g_o[...] = _dot(_sigmoid(gd).astype(BF16), g2_ref[...]).astype(BF16)


def _rwkv_prep(z, p, layer, n_ctx_tiles, tiles_per_lat, tm):
    m = z.shape[0]
    hb = tm // GRID_W
    n_halo = m // GRID_W
    lay = lambda *shape: pl.BlockSpec((None,) + shape, lambda i: (layer,) + (0,) * len(shape))
    outs = [(3, BF16), (2, BF16), (2, BF16), (2, F32), (1, BF16), (1, BF16)]
    return pl.pallas_call(
        functools.partial(_prep_kernel, n_ctx_tiles=n_ctx_tiles, tiles_per_lat=tiles_per_lat),
        grid=(m // tm,),
        in_specs=[
            pl.BlockSpec((tm, Z_COLS), lambda i: (i, 0)),
            pl.BlockSpec((GRID_W, Z_COLS), lambda i: (jnp.where(i < n_ctx_tiles, 0, i * hb - 1), 0)),
            pl.BlockSpec((GRID_W, Z_COLS),
                         lambda i: (jnp.where(i < n_ctx_tiles, 0, jnp.minimum(i * hb + hb, n_halo - 1)), 0)),
            lay(1, C_PAD),
            lay(2, C_WIDTH),
            lay(384, 2 * C_WIDTH),
            lay(2, C_WIDTH),
            lay(384, 2 * C_WIDTH),
            lay(1, C_WIDTH),
            lay(1, C_WIDTH),
            lay(1, C_WIDTH),
            lay(256, C_WIDTH),
            pl.BlockSpec((512, 256), lambda i: (0, 0)),
            lay(1, A_WIDTH),
            lay(1, A_WIDTH),
            lay(A_HEADS, SGU_CHUNK, SGU_CHUNK),
            lay(SGU_CHUNK, A_WIDTH),
        ],
        out_specs=([pl.BlockSpec((tm, n * C_WIDTH), lambda i: (i, 0)) for n, _ in outs]
                   + [pl.BlockSpec((tm, A_WIDTH), lambda i: (i, 0))]),
        out_shape=([jax.ShapeDtypeStruct((m, n * C_WIDTH), dt) for n, dt in outs]
                   + [jax.ShapeDtypeStruct((m, MIX_WIDTH), BF16)]),
        scratch_shapes=[pltpu.VMEM((tm, C_PAD), F32)],
        compiler_params=_cparams(("arbitrary",)),
        name="rwkv_prep",
    )(z, z, z, p["mu"], p["w0"], p["w2"], p["a0"], p["a2"], p["k_k"], p["k_a"], p["r_k"], p["g2"],
      p["e256"], p["ln_g"], p["ln_b"], p["sgu_w"], p["sgu_bias"])


def _bmm(a, b):
    return jnp.einsum("uik,ukj->uij", a.astype(BF16), b.astype(BF16), preferred_element_type=F32)


def _bmm_nt(a, b):
    return jnp.einsum("uik,ujk->uij", a.astype(BF16), b.astype(BF16), preferred_element_type=F32)


def _block_diag(y, bd_mask):
    return jnp.where(bd_mask, jnp.concatenate([y] * UNIT_HEADS, axis=1), jnp.zeros((), y.dtype))


def _to_units(x):
    return jnp.stack([x[:, p * UNIT_W:(p + 1) * UNIT_W] for p in range(N_UNITS)], axis=0)


def _scan_chunks(streams, hs):
    t = SCAN_T
    w = UNIT_W
    ti = lax.broadcasted_iota(jnp.int32, (t, w), 0)
    si = lax.broadcasted_iota(jnp.int32, (t, w), 1) & (HEAD_DIM - 1)
    eye = jnp.where(si == ti, 1.0, 0.0)
    t_row = lax.broadcasted_iota(jnp.int32, (t, 3 * t), 0)
    t_col = lax.broadcasted_iota(jnp.int32, (t, 3 * t), 1) & (t - 1)
    bd_mask = (lax.broadcasted_iota(jnp.int32, (w, w), 0) // HEAD_DIM
               == lax.broadcasted_iota(jnp.int32, (w, w), 1) // HEAD_DIM)

    q, ai, ki, vb, vf, lhs, p_tot = [], [], [], [], [], [], []
    for reverse, r, v, kk, ld, km, ka in streams:
        tri = jnp.where((t_col >= t_row) if reverse else (t_col <= t_row), 1.0, 0.0).astype(BF16)
        cum = _dot_exact_lhs(tri, ld)
        c_end = cum[0:1, :] if reverse else cum[t - 1:t, :]
        p_end = jnp.exp(c_end - cum)
        p_inv = jnp.exp(-cum)
        q.append(_to_units(jnp.concatenate([kk * jnp.exp(cum - ld), r * jnp.exp(cum)], axis=0).astype(BF16)))
        ai.append(_to_units((ka * p_inv).astype(BF16)))
        ki.append(_to_units((km * p_inv).astype(BF16)))
        vb.append(_to_units(v.astype(BF16)))
        vf.append(_to_units(v))
        lhs.append(_to_units(jnp.concatenate([ka * p_end, km * p_end], axis=0).astype(BF16)))
        p_tot.append(_to_units(jnp.exp(c_end)))
    q, ai, ki, vb, vf, lhs, p_tot = (jnp.concatenate(a, axis=0) for a in (q, ai, ki, vb, vf, lhs, p_tot))

    def masked(x, strictly):
        parts = []
        for n, stream in enumerate(streams):
            if stream[0]:
                keep = si > ti if strictly else si >= ti
            else:
                keep = si < ti if strictly else si <= ti
            parts.append(jnp.where(keep, x[n * N_UNITS:(n + 1) * N_UNITS], 0.0))
        return jnp.concatenate(parts, axis=0)

    rr = jnp.concatenate([_block_diag(ai, bd_mask), _block_diag(ki, bd_mask)], axis=1)
    sc = _bmm_nt(q, rr)
    la = masked(sc[:, 0:t, 0:w], True)
    lk = masked(sc[:, 0:t, w:2 * w], True)
    ma = masked(sc[:, t:2 * t, 0:w], False)
    mk = masked(sc[:, t:2 * t, w:2 * w], False)

    n_pow = -la
    x_inv = eye + n_pow
    n_pow = _bmm(n_pow, _block_diag(n_pow.astype(BF16), bd_mask))
    for _ in range(int(np.log2(t)) - 2):
        both = _bmm(jnp.concatenate([n_pow, x_inv], axis=1), _block_diag(n_pow.astype(BF16), bd_mask))
        x_inv = x_inv + both[:, t:2 * t]
        n_pow = both[:, 0:t]
    x_inv = x_inv + _bmm(x_inv, _block_diag(n_pow.astype(BF16), bd_mask))

    hb = _bmm_nt(q, hs)
    lv = _bmm(jnp.concatenate([lk, mk], axis=1), _block_diag(vb, bd_mask))
    u = -_bmm(x_inv, _block_diag((hb[:, 0:t] + lv[:, 0:t]).astype(BF16), bd_mask))
    y = hb[:, t:2 * t] + lv[:, t:2 * t] + _bmm(ma, _block_diag(u.astype(BF16), bd_mask))
    rhs_t = jnp.swapaxes(jnp.concatenate([u, vf], axis=1), 1, 2)
    upd = _bmm(rhs_t, lhs)
    hs_new = p_tot * hs + jnp.where(bd_mask, upd, 0.0)
    return y, hs_new


def _scan_kernel(*refs, has_s0, emit_state, aliased_state):
    ins = refs[:6]
    pos = 6
    s0_ref = None
    if has_s0:
        s0_ref = refs[pos]
        pos += 1
    if aliased_state:
        pos += 1
    y_refs = refs[pos:pos + 2]
    pos += 2
    st_ref = None
    if emit_state:
        st_ref = refs[pos]
        pos += 1
    h_ref = refs[pos]
    c = pl.program_id(1)
    hd = HEAD_DIM

    n_group = h_ref.shape[0]

    @pl.when(c == 0)
    def _():
        if has_s0:
            zero = jnp.zeros((hd, hd), F32)
            for s in range(n_group):
                for d in range(2):
                    for p in range(N_UNITS):
                        rows = [jnp.concatenate([s0_ref[s, d, UNIT_HEADS * p + a] if a == b else zero
                                                 for b in range(UNIT_HEADS)], axis=1) for a in range(UNIT_HEADS)]
                        h_ref[s, d, p] = jnp.concatenate(rows, axis=0)
        else:
            h_ref[...] = jnp.zeros_like(h_ref)

    streams = []
    for s in range(n_group):
        for d in range(2):
            rvk_ref, kd_ref, ld_ref = ins[3 * d:3 * d + 3]
            r, v, kk = (rvk_ref[s, :, n * C_WIDTH:(n + 1) * C_WIDTH].astype(F32) for n in range(3))
            km, ka = (kd_ref[s, :, n * C_WIDTH:(n + 1) * C_WIDTH].astype(F32) for n in range(2))
            streams.append((d == 1, r, v, kk, ld_ref[s], km, ka))
    n_all = len(streams) * N_UNITS
    y, hs_new = _scan_chunks(streams, h_ref[...].reshape(n_all, UNIT_W, UNIT_W))
    h_ref[...] = hs_new.reshape(h_ref.shape)
    for s in range(n_group):
        for d, y_ref in enumerate(y_refs):
            for p in range(N_UNITS):
                y_ref[s, :, p * UNIT_W:(p + 1) * UNIT_W] = y[(2 * s + d) * N_UNITS + p].astype(BF16)

    if emit_state:
        @pl.when(c == pl.num_programs(1) - 1)
        def _():
            for s in range(n_group):
                for d in range(2):
                    for p in range(N_UNITS):
                        hs = h_ref[s, d, p]
                        for a in range(UNIT_HEADS):
                            st_ref[s, d, UNIT_HEADS * p + a] = hs[a * hd:(a + 1) * hd, a * hd:(a + 1) * hd]


def _rwkv_scan(arrs, n_seq, n_chunk, row0, layer, group, s0=None, state_shape=None, state_prev=None):
    seq_len = n_chunk * SCAN_T
    m_all = arrs[0].shape[0]
    assert row0 % (seq_len * group) == 0 and m_all % seq_len == 0 and n_seq % group == 0
    rvk, kf, kb, ld = (a.reshape(m_all // seq_len, seq_len, a.shape[1]) for a in arrs)
    g0 = row0 // (seq_len * group)
    fwd = lambda b, c: (b, c, 0)
    bwd = lambda b, c: (b, n_chunk - 1 - c, 0)
    blk = lambda imap, n=1: pl.BlockSpec((group, SCAN_T, n * C_WIDTH), imap)
    in_specs = [blk(lambda b, c: (g0 + b, c, 0), 3), blk(lambda b, c: (g0 + b, c, 0), 2),
                blk(lambda b, c: (g0 + b, c, 0)),
                blk(lambda b, c: (g0 + b, n_chunk - 1 - c, 0), 3), blk(lambda b, c: (g0 + b, n_chunk - 1 - c, 0), 2),
                blk(lambda b, c: (g0 + b, n_chunk - 1 - c, 1))]
    args = [rvk, kf, ld, rvk, kb, ld]
    st_spec = pl.BlockSpec((group, None, 2, N_HEADS, HEAD_DIM, HEAD_DIM), lambda b, c: (b, layer, 0, 0, 0, 0))
    if s0 is not None:
        in_specs.append(st_spec)
        args.append(s0)
    aliases = {}
    if state_prev is not None:
        aliases = {len(args): 2}
        in_specs.append(pl.BlockSpec(memory_space=pl.ANY))
        args.append(state_prev)
    y_shape = jax.ShapeDtypeStruct((n_seq, seq_len, C_WIDTH), BF16)
    out_specs = [blk(fwd), blk(bwd)]
    out_shape = [y_shape, y_shape]
    if state_shape is not None:
        out_specs.append(st_spec)
        out_shape.append(jax.ShapeDtypeStruct(state_shape, F32))
    outs = pl.pallas_call(
        functools.partial(_scan_kernel, has_s0=s0 is not None, emit_state=state_shape is not None,
                          aliased_state=state_prev is not None),
        grid=(n_seq // group, n_chunk),
        in_specs=in_specs,
        out_specs=out_specs,
        out_shape=out_shape,
        input_output_aliases=aliases,
        scratch_shapes=[pltpu.VMEM((group, 2, N_UNITS, UNIT_W, UNIT_W), F32)],
        compiler_params=_cparams(("arbitrary", "arbitrary")),
        name="rwkv_scan",
    )(*args)
    ys = [y.reshape(n_seq * seq_len, C_WIDTH) for y in outs[:2]]
    return ys + list(outs[2:])


def _post_kernel(yf1_ref, yb1_ref, yf2_ref, yb2_ref, bv_ref, g_ref, lg_ref, lb_ref, e_ref, mix_ref, o_ref, *,
                 n_ctx_tiles):
    del mix_ref
    is_ctx = pl.program_id(0) < n_ctx_tiles
    f32 = lambda ref: ref[...].astype(F32)
    y = jnp.where(is_ctx, f32(yf1_ref) + f32(yb1_ref), f32(yf2_ref) + f32(yb2_ref))
    inv = 1.0 / HEAD_DIM
    m = _seg_sum(y, e_ref) * inv
    yc = y - m
    var = _seg_sum(yc * yc, e_ref) * inv
    yn = yc * lax.rsqrt(var + LNX_EPS) * lg_ref[...] + lb_ref[...]
    o_ref[...] = ((yn + bv_ref[...]) * g_ref[...]).astype(BF16)


def _rwkv_post(y_ctx, y_lat, bv, g, lnx_g, lnx_b, e256, mix, layer, n_ctx_tiles, tm):
    m = bv.shape[0]
    n_tiles = m // tm
    row = pl.BlockSpec((tm, C_WIDTH), lambda i: (i, 0))
    row_ctx = pl.BlockSpec((tm, C_WIDTH), lambda i: (jnp.minimum(i, n_ctx_tiles - 1), 0))
    row_lat = pl.BlockSpec((tm, C_WIDTH), lambda i: (jnp.maximum(i - n_ctx_tiles, 0), 0))
    vec = pl.BlockSpec((None, 1, C_WIDTH), lambda i: (layer, 0, 0))
    return pl.pallas_call(
        functools.partial(_post_kernel, n_ctx_tiles=n_ctx_tiles),
        grid=(n_tiles,),
        in_specs=[row_ctx, row_ctx, row_lat, row_lat, row, row, vec, vec,
                  pl.BlockSpec((512, 256), lambda i: (0, 0)), pl.BlockSpec(memory_space=pl.ANY)],
        out_specs=pl.BlockSpec((tm, C_WIDTH), lambda i: (i, (A_WIDTH + B_WIDTH) // C_WIDTH)),
        out_shape=jax.ShapeDtypeStruct(mix.shape, mix.dtype),
        input_output_aliases={9: 0},
        compiler_params=_cparams(("arbitrary",)),
        name="rwkv_post",
    )(y_ctx[0], y_ctx[1], y_lat[0], y_lat[1], bv, g, lnx_g, lnx_b, e256, mix)


def kernel(x_prompt, x_sample, state_wkv, c, c_ctx, norm_g, w_mod, b_mod, ffn_w_in, ffn_w_out, w_in, w_out,
           sgu_ln_g, sgu_ln_b, sgu_w, sgu_b, shift_mu, decay_w0, decay_w2, iclr_a0, iclr_a2, k_k, k_a, r_k,
           gate_w2, lnx_g, lnx_b, final_g):
    batch, seq, d = x_prompt.shape
    dec_batch, dec_seq, _ = x_sample.shape
    depth = w_mod.shape[0]
    assert d == D_MODEL and dec_batch + 1 <= MOD_ROWS
    m_ctx = batch * seq
    tm = 1024
    ts = 512
    tp = 256
    assert seq == tp and dec_seq == tm and m_ctx % tm == 0 and dec_seq % GRID_W == 0

    def row_of_tile_fn(rows):
        n_ctx = m_ctx // rows
        per_lat = dec_seq // rows
        return lambda i: jnp.where(i < n_ctx, 0, 1 + (i - n_ctx) // per_lat)

    zeros_cw = jnp.zeros((depth, DECAY_RANK, C_WIDTH), F32)

    def both_dirs(w):
        top = jnp.concatenate([w[:, 0], zeros_cw], axis=-1)
        bot = jnp.concatenate([zeros_cw, w[:, 1]], axis=-1)
        return jnp.concatenate([top, bot], axis=1)

    e_np = (np.arange(256)[:, None] // HEAD_DIM == np.arange(256)[None, :] // HEAD_DIM)
    prep_params = {
        "mu": jnp.pad(shift_mu, ((0, 0), (0, C_PAD - C_IN))).reshape(depth, 1, C_PAD),
        "w0": decay_w0, "w2": _split_rows3(both_dirs(decay_w2)), "a0": iclr_a0, "a2": _split_rows3(both_dirs(iclr_a2)),
        "k_k": k_k.reshape(depth, 1, C_WIDTH), "k_a": k_a.reshape(depth, 1, C_WIDTH),
        "r_k": r_k.reshape(depth, 1, C_WIDTH),
        "g2": jnp.pad(gate_w2, ((0, 0), (0, 256 - GATE_RANK), (0, 0))).astype(BF16),
        "e256": jnp.asarray(np.concatenate([e_np, e_np], axis=0), BF16),
    }
    cd, sd = _dft_mats(B_GROUP_DIM)
    cs, ss = _dft_mats(seq)
    cl, sl = _dft_mats(dec_seq)
    fnet_consts = (_hi_lo(np.concatenate([cd, sd], axis=1)) + _hi_lo(np.concatenate([cs, -ss], axis=1))
                   + _hi_lo(np.concatenate([cl, -sl], axis=1)))
    prep_params.update({
        "sgu_bias": jnp.repeat(jnp.swapaxes(sgu_b, 1, 2), A_WIDTH // A_HEADS, axis=2),
        "sgu_w": sgu_w.astype(BF16),
        "ln_g": sgu_ln_g.reshape(depth, 1, A_WIDTH), "ln_b": sgu_ln_b.reshape(depth, 1, A_WIDTH),
    })
    lnx_g3 = lnx_g.reshape(depth, 1, C_WIDTH)
    lnx_b3 = lnx_b.reshape(depth, 1, C_WIDTH)
    norm_g4 = norm_g.reshape(depth, 3, 1, d)

    cond = jnp.concatenate([c_ctx[None, :], c, jnp.zeros((MOD_ROWS - 1 - dec_batch, d), F32)], axis=0)
    mod = _modulation(cond, w_mod, b_mod).reshape(depth, MOD_ROWS, N_MOD, 1, d)

    m = m_ctx + dec_batch * dec_seq
    n_ctx_t = m_ctx // tm
    n_lat_t = dec_batch * dec_seq // tm
    rot = row_of_tile_fn(tm)
    ctx_row = lambda i: 0
    lat_row = lambda i: 1 + i
    n_chunk_ctx = seq // SCAN_T
    n_chunk_lat = dec_seq // SCAN_T
    state_shape = (batch, depth, 2, N_HEADS, HEAD_DIM, HEAD_DIM)
    tk = 1024
    scan_group = lambda n_seq: 4 if n_seq % 4 == 0 else 1
    w_in_t = jnp.swapaxes(w_in, 1, 2).astype(BF16)
    w_out_b = w_out.astype(BF16)
    new_state = None
    x = None
    for l in range(depth):
        ffn_args = (mod, norm_g4, ffn_w_in, ffn_w_out, l)
        if l == 0:
            x = _ffn(x_prompt.reshape(m_ctx, d), *ffn_args, 0, 0, ctx_row, tm, n_ctx_t, out_rows=m)
            x = _ffn(x_sample.reshape(m - m_ctx, d), *ffn_args, 0, 0, lat_row, tm, n_lat_t, out_tile0=n_ctx_t,
                     out_rows=m, out_prev=x)
        else:
            x = _ffn(x, *ffn_args, 0, 0, rot, tm, n_ctx_t + n_lat_t)
        z = _in_proj(x, mod, norm_g4, w_in_t, l, rot, tm)
        *scan_in, bv, g, mix = _rwkv_prep(z, prep_params, l, m_ctx // tp, dec_seq // tp, tp)
        mix = _fnet(z, fnet_consts, mix, m_ctx // tm, seq, tm)
        yf_c, yb_c, new_state = _rwkv_scan(scan_in, batch, n_chunk_ctx, 0, l, scan_group(batch),
                                           state_shape=state_shape, state_prev=new_state)
        yf_l, yb_l = _rwkv_scan(scan_in, dec_batch, n_chunk_lat, m_ctx, l, scan_group(dec_batch), s0=state_wkv)
        mix = _rwkv_post((yf_c, yb_c), (yf_l, yb_l), bv, g, lnx_g3, lnx_b3, prep_params["e256"], mix, l,
                         m_ctx // ts, ts)
        x = _resid_matmul(x, mod, mix, w_out_b, pl.BlockSpec((None, tk, D_MODEL), lambda i, k: (l, k, 0)),
                          l, 5, 1.0, rot, tm, tk, "mix_out")
        if l < depth - 1:
            x = _ffn(x, *ffn_args, 2, 1, rot, tm, n_ctx_t + n_lat_t)
    fg = final_g.reshape(1, d)
    y_ctx = _ffn(x, *ffn_args, 2, 1, ctx_row, tm, n_ctx_t, final_g=fg)
    y_lat = _ffn(x, *ffn_args, 2, 1, lat_row, tm, n_lat_t, in_tile0=n_ctx_t, final_g=fg)
    return (y_ctx.reshape(batch, seq, d), y_lat.reshape(dec_batch, dec_seq, d), new_state)
```

```python
import functools

import numpy as np
import jax
import jax.numpy as jnp
from jax import lax
from jax.experimental import pallas as pl
from jax.experimental.pallas import tpu as pltpu

F32 = jnp.float32
BF16 = jnp.bfloat16

D_MODEL = 2048
GRID_W = 64
SGU_CHUNK = 128
A_HEADS = 4
A_WIDTH = 512
B_WIDTH = 512
B_GROUP_DIM = 128
C_WIDTH = 1024
MIX_WIDTH = A_WIDTH + B_WIDTH + C_WIDTH
HEAD_DIM = 64
N_HEADS = 16
UNIT_HEADS = 2
UNIT_W = UNIT_HEADS * HEAD_DIM
N_UNITS = N_HEADS // UNIT_HEADS
DECAY_RANK = 64
GATE_RANK = 160
C_IN = 3488
C_PAD = 3584
IN_COLS = 2 * A_WIDTH + B_WIDTH + C_IN
Z_BLOCK = 1024
Z_C0 = 2 * A_WIDTH + B_WIDTH
Z_COLS = Z_C0 + C_PAD
D_FF = 5632
N_MOD = 9
RMS_EPS = 1e-6
LN_EPS = 1e-5
LNX_EPS = 64e-5
SCAN_T = 64
MOD_ROWS = 8

VMEM_LIMIT = 56 * 1024 * 1024
FFN_VMEM_LIMIT = 60 * 1024 * 1024


def _cparams(sem):
    return pltpu.CompilerParams(dimension_semantics=sem, vmem_limit_bytes=VMEM_LIMIT)


def _dot(a, b):
    return jnp.dot(a, b, preferred_element_type=F32)


def _split3(x):
    hi = x.astype(BF16)
    r1 = x - hi.astype(F32)
    mid = r1.astype(BF16)
    lo = (r1 - mid.astype(F32)).astype(BF16)
    return hi, mid, lo


def _dot_ones_rhs(x, e2):
    hi = x.astype(BF16)
    lo = (x - hi.astype(F32)).astype(BF16)
    return _dot(jnp.concatenate([hi, lo], axis=1), e2)


def _dot_exact_lhs(e3, x):
    return _dot(e3, jnp.concatenate(_split3(x), axis=0))


def _split_rows3(b):
    bh = b.astype(BF16)
    bl = (b - bh.astype(F32)).astype(BF16)
    return jnp.concatenate([bh, bh, bl], axis=-2)


def _dot3_presplit(a, b3):
    ah = a.astype(BF16)
    al = (a - ah.astype(F32)).astype(BF16)
    return _dot(jnp.concatenate([ah, al, ah], axis=1), b3)


def _sigmoid(x):
    return 0.5 * jnp.tanh(0.5 * x) + 0.5


def _silu(x):
    return x * _sigmoid(x)


def _gelu_tanh(x):
    return 0.5 * x * (1.0 + jnp.tanh(0.7978845608028654 * (x + 0.044715 * (x * x * x))))


def _mod_kernel(c_ref, w_ref, b_ref, o_ref):
    s = _silu(c_ref[...]).astype(BF16)
    o_ref[...] = _dot(s, w_ref[...].astype(BF16)) + b_ref[...]


def _modulation(cond, w_mod, b_mod):
    depth = w_mod.shape[0]
    n = w_mod.shape[2]
    tn = 1024
    return pl.pallas_call(
        _mod_kernel,
        grid=(depth, n // tn),
        in_specs=[
            pl.BlockSpec((MOD_ROWS, D_MODEL), lambda l, j: (0, 0)),
            pl.BlockSpec((None, D_MODEL, tn), lambda l, j: (l, 0, j)),
            pl.BlockSpec((None, 1, tn), lambda l, j: (l, 0, j)),
        ],
        out_specs=pl.BlockSpec((None, MOD_ROWS, tn), lambda l, j: (l, 0, j)),
        out_shape=jax.ShapeDtypeStruct((depth, MOD_ROWS, n), F32),
        compiler_params=_cparams(("arbitrary", "arbitrary")),
        name="modulation",
    )(cond, w_mod, b_mod.reshape(depth, 1, n))


def _mod_spec(layer, slot, row_of_tile):
    return pl.BlockSpec((None, None, None, 1, D_MODEL),
                        lambda i, *_: (layer, row_of_tile(i), slot, 0, 0))


def _norm_spec(layer, slot):
    return pl.BlockSpec((None, None, 1, D_MODEL), lambda i, *_: (layer, slot, 0, 0))


def _modulated_norm(x, g_ref, sc_ref, sh_ref):
    rs = lax.rsqrt(jnp.mean(x * x, axis=-1, keepdims=True) + RMS_EPS)
    gain = g_ref[...] * (1.0 + sc_ref[...])
    return ((x * rs) * gain + sh_ref[...]).astype(BF16)


def _in_proj_kernel(x_ref, sh_ref, sc_ref, g_ref, wt_ref, o_ref, h_ref, *, valid_cols):
    j = pl.program_id(1)

    def step(first):
        if first:
            h = _modulated_norm(x_ref[...], g_ref, sc_ref, sh_ref)
            h_ref[...] = h
        else:
            h = h_ref[...]
        out = lax.dot_general(h, wt_ref[...].astype(BF16), (((1,), (1,)), ((), ())),
                              preferred_element_type=F32)
        tn = out.shape[1]
        if valid_cols % tn:
            col = j * tn + lax.broadcasted_iota(jnp.int32, (1, tn), 1)
            out = jnp.where(col < valid_cols, out, 0.0)
        o_ref[...] = out

    pl.when(j == 0)(lambda: step(True))
    pl.when(j > 0)(lambda: step(False))


def _in_proj(x, mod, norm_g, w_in_t, layer, row_of_tile, tm):
    m = x.shape[0]
    nb = Z_COLS // Z_BLOCK
    assert -(-IN_COLS // Z_BLOCK) == nb
    return pl.pallas_call(
        functools.partial(_in_proj_kernel, valid_cols=IN_COLS),
        grid=(m // tm, nb),
        in_specs=[
            pl.BlockSpec((tm, D_MODEL), lambda i, j: (i, 0)),
            _mod_spec(layer, 3, row_of_tile),
            _mod_spec(layer, 4, row_of_tile),
            _norm_spec(layer, 1),
            pl.BlockSpec((None, Z_BLOCK, D_MODEL), lambda i, j: (layer, j, 0)),
        ],
        out_specs=pl.BlockSpec((tm, Z_BLOCK), lambda i, j: (i, j)),
        out_shape=jax.ShapeDtypeStruct((m, Z_COLS), F32),
        scratch_shapes=[pltpu.VMEM((tm, D_MODEL), BF16)],
        compiler_params=_cparams(("arbitrary", "arbitrary")),
        name="in_proj",
    )(x, mod, mod, norm_g, w_in_t)


def _ffn_kernel(x_ref, sh_ref, sc_ref, gt_ref, g_ref, wg_ref, wu_ref, wo_ref, *rest, tn, final_norm, n_first):
    fg_ref = rest[0] if final_norm else None
    o_ref, h_ref = rest[-2:]
    j = pl.program_id(1)
    last = pl.num_programs(1) - 1
    if n_first is None:
        x_at = lambda cols: x_ref[:, cols]
    else:
        x_tail_ref = rest[-3]
        from_first = pl.program_id(0) < n_first
        x_at = lambda cols: jnp.where(from_first, x_ref[:, cols], x_tail_ref[:, cols])

    def step(first, final):
        if first:
            h = _modulated_norm(x_at(slice(None)), g_ref, sc_ref, sh_ref)
            h_ref[...] = h
        else:
            h = h_ref[...]
        gate = _dot(h, wg_ref[...].astype(BF16))
        up = _dot(h, wu_ref[...].astype(BF16))
        a = (_silu(gate) * up).astype(BF16)
        for n in range(o_ref.shape[1] // tn):
            cols = slice(n * tn, (n + 1) * tn)
            acc = _dot(a, wo_ref[:, cols].astype(BF16))
            if not first:
                acc = o_ref[:, cols] + acc
            if final:
                acc = x_at(cols) + (0.5 * gt_ref[:, cols]) * acc
            o_ref[:, cols] = acc
        if final and final_norm:
            y = o_ref[...]
            rs = lax.rsqrt(jnp.mean(y * y, axis=-1, keepdims=True) + RMS_EPS)
            o_ref[...] = y * rs * fg_ref[...]

    pl.when(j == 0)(lambda: step(True, False))
    pl.when(jnp.logical_and(j > 0, j < last))(lambda: step(False, False))
    pl.when(j == last)(lambda: step(False, True))


def _ffn(x, mod, norm_g, w_in, w_out, layer, sub, ffn_idx, row_of_tile, tm, n_tiles, in_tile0=0, x_tail=None,
         final_g=None, tf=256):
    nf = D_FF // tf
    args = [x, mod, mod, mod, norm_g, w_in, w_in, w_out]
    n_first = None
    if x_tail is None:
        x_spec = pl.BlockSpec((tm, D_MODEL), lambda i, j: (in_tile0 + i, 0))
    else:
        n_first = x.shape[0] // tm
        x_spec = pl.BlockSpec((tm, D_MODEL), lambda i, j: (jnp.minimum(i, n_first - 1), 0),
                              pipeline_mode=pl.Buffered(1))
    in_specs = [
        x_spec,
        _mod_spec(layer, 3 * sub, row_of_tile),
        _mod_spec(layer, 3 * sub + 1, row_of_tile),
        _mod_spec(layer, 3 * sub + 2, row_of_tile),
        _norm_spec(layer, sub),
        pl.BlockSpec((None, None, D_MODEL, tf), lambda i, j: (layer, ffn_idx, 0, j)),
        pl.BlockSpec((None, None, D_MODEL, tf), lambda i, j: (layer, ffn_idx, 0, nf + j)),
        pl.BlockSpec((None, None, tf, D_MODEL), lambda i, j: (layer, ffn_idx, j, 0)),
    ]
    if final_g is not None:
        args.append(final_g)
        in_specs.append(pl.BlockSpec((1, D_MODEL), lambda i, j: (0, 0)))
    if x_tail is not None:
        args.append(x_tail)
        in_specs.append(pl.BlockSpec((tm, D_MODEL), lambda i, j: (jnp.maximum(i - n_first, 0), 0),
                                     pipeline_mode=pl.Buffered(1)))
    return pl.pallas_call(
        functools.partial(_ffn_kernel, tn=512, final_norm=final_g is not None, n_first=n_first),
        grid=(n_tiles, nf),
        in_specs=in_specs,
        out_specs=pl.BlockSpec((tm, D_MODEL), lambda i, j: (i, 0)),
        out_shape=jax.ShapeDtypeStruct((n_tiles * tm, D_MODEL), F32),
        scratch_shapes=[pltpu.VMEM((tm, D_MODEL), BF16)],
        compiler_params=pltpu.CompilerParams(dimension_semantics=("arbitrary", "arbitrary"),
                                             vmem_limit_bytes=FFN_VMEM_LIMIT),
        name="ffn",
    )(*args)


def _resid_matmul_kernel(x_ref, gt_ref, a_ref, w_ref, o_ref, *, coef, tn):
    k = pl.program_id(1)
    last = pl.num_programs(1) - 1

    def step(first, final):
        a = a_ref[...]
        for n in range(o_ref.shape[1] // tn):
            cols = slice(n * tn, (n + 1) * tn)
            acc = _dot(a, w_ref[:, cols].astype(BF16))
            if not first:
                acc = o_ref[:, cols] + acc
            if final:
                acc = x_ref[:, cols] + (coef * gt_ref[:, cols]) * acc
            o_ref[:, cols] = acc

    pl.when(k == 0)(lambda: step(True, False))
    pl.when(jnp.logical_and(k > 0, k < last))(lambda: step(False, False))
    pl.when(k == last)(lambda: step(False, True))


def _resid_matmul(x, mod, a, w, w_spec, layer, gate_slot, coef, row_of_tile, tm, tk, name):
    m = x.shape[0]
    return pl.pallas_call(
        functools.partial(_resid_matmul_kernel, coef=coef, tn=512),
        grid=(m // tm, a.shape[1] // tk),
        in_specs=[
            pl.BlockSpec((tm, D_MODEL), lambda i, k: (i, 0)),
            _mod_spec(layer, gate_slot, row_of_tile),
            pl.BlockSpec((tm, tk), lambda i, k: (i, k)),
            w_spec,
        ],
        out_specs=pl.BlockSpec((tm, D_MODEL), lambda i, k: (i, 0)),
        out_shape=jax.ShapeDtypeStruct((m, D_MODEL), F32),
        compiler_params=_cparams(("arbitrary", "arbitrary")),
        name=name,
    )(x, mod, a, w)


def _spatial_gating(zu, zv, lg_ref, lb_ref, w_ref, bias_ref, o_ref):
    u = _gelu_tanh(zu)
    v = _gelu_tanh(zv)
    mu = jnp.mean(v, axis=-1, keepdims=True)
    vc = v - mu
    var = jnp.mean(vc * vc, axis=-1, keepdims=True)
    vn = (vc * lax.rsqrt(var + LN_EPS) * lg_ref[...] + lb_ref[...]).astype(BF16)
    hd = A_WIDTH // A_HEADS
    for c in range(u.shape[0] // SGU_CHUNK):
        rows = slice(c * SGU_CHUNK, (c + 1) * SGU_CHUNK)
        for h in range(A_HEADS):
            cols = slice(h * hd, (h + 1) * hd)
            mixed = _dot(w_ref[h], vn[rows, cols]) + bias_ref[:, cols]
            o_ref[rows, cols] = (u[rows, cols] * mixed).astype(BF16)


def _dft_mats(n):
    idx = np.arange(n)
    ang = 2.0 * np.pi * ((idx[:, None] * idx[None, :]) % n) / n
    return np.cos(ang) / np.sqrt(n), np.sin(ang) / np.sqrt(n)


def _hi_lo(a):
    a32 = jnp.asarray(a, F32)
    hi = a32.astype(BF16)
    lo = (a32 - hi.astype(F32)).astype(BF16)
    return hi, lo


def _fnet_kernel(z_ref, fdh_ref, fdl_ref, fsh_ref, fsl_ref, flh_ref, fll_ref, mix_ref, o_ref, t_ref, *,
                 n_ctx_tiles, seq):
    del mix_ref
    tm = z_ref.shape[0]
    x = z_ref[...]
    xh = x.astype(BF16)
    xl = (x - xh.astype(F32)).astype(BF16)
    gd = B_GROUP_DIM
    for g in range(B_WIDTH // gd):
        cols = slice(g * gd, (g + 1) * gd)
        t = _dot(xh[:, cols], fdh_ref[...]) + _dot(xl[:, cols], fdh_ref[...]) + _dot(xh[:, cols], fdl_ref[...])
        t_ref[0:tm, cols] = t[:, 0:gd]
        t_ref[tm:2 * tm, cols] = t[:, gd:2 * gd]

    def position_dft(fh_ref, fl_ref, rows_in, rows_out):
        tc = t_ref[rows_in[0], :]
        ts = t_ref[rows_in[1], :]
        tt = jnp.concatenate([tc, ts], axis=0)
        th = tt.astype(BF16)
        tl = (tt - th.astype(F32)).astype(BF16)
        out = _dot(fh_ref[...], th) + _dot(fl_ref[...], th) + _dot(fh_ref[...], tl)
        o_ref[rows_out, :] = out.astype(BF16)

    is_ctx = pl.program_id(0) < n_ctx_tiles

    @pl.when(is_ctx)
    def _():
        for s in range(tm // seq):
            r0 = slice(s * seq, (s + 1) * seq)
            r1 = slice(tm + s * seq, tm + (s + 1) * seq)
            position_dft(fsh_ref, fsl_ref, (r0, r1), r0)

    @pl.when(jnp.logical_not(is_ctx))
    def _():
        position_dft(flh_ref, fll_ref, (slice(0, tm), slice(tm, 2 * tm)), slice(0, tm))


def _fnet(z, consts, mix, n_ctx_tiles, seq, tm):
    m = z.shape[0]
    cb = 2 * A_WIDTH // B_WIDTH
    full = lambda a: pl.BlockSpec(a.shape, lambda i: (0,) * a.ndim)
    return pl.pallas_call(
        functools.partial(_fnet_kernel, n_ctx_tiles=n_ctx_tiles, seq=seq),
        grid=(m // tm,),
        in_specs=([pl.BlockSpec((tm, B_WIDTH), lambda i: (i, cb))] + [full(a) for a in consts]
                  + [pl.BlockSpec(memory_space=pl.ANY)]),
        out_specs=pl.BlockSpec((tm, B_WIDTH), lambda i: (i, A_WIDTH // B_WIDTH)),
        out_shape=jax.ShapeDtypeStruct(mix.shape, mix.dtype),
        input_output_aliases={1 + len(consts): 0},
        scratch_shapes=[pltpu.VMEM((2 * tm, B_WIDTH), F32)],
        compiler_params=_cparams(("arbitrary",)),
        name="fnet",
    )(z, *consts, mix)


def _seg_sum(x, e_ref):
    w = e_ref.shape[1]
    parts = [_dot_ones_rhs(x[:, b * w:(b + 1) * w], e_ref[...]) for b in range(x.shape[1] // w)]
    return parts[0] if len(parts) == 1 else jnp.concatenate(parts, axis=1)


def _token_shift_mix(z_ref, zc_ref, zp_ref, zn_ref, mu_ref, bounds, period, up_ok, dn_ok):
    tm = zc_ref.shape[0]
    row = lax.broadcasted_iota(jnp.int32, (tm, 1), 0)
    pos = row & (period - 1)
    lanes = 128

    def neighbour(kind, src):
        xs = zc_ref[:, src]
        if kind == 0:
            return jnp.where(pos == 0, 0.0, pltpu.roll(xs, 1, 0))
        if kind == 1:
            return jnp.where(pos == period - 1, 0.0, pltpu.roll(xs, tm - 1, 0))
        if kind == 2:
            return jnp.concatenate([jnp.where(up_ok, zp_ref[:, src], 0.0), xs[:tm - GRID_W]], axis=0)
        return jnp.concatenate([xs[GRID_W:], jnp.where(dn_ok, zn_ref[:, src], 0.0)], axis=0)

    edges = sorted({0, C_PAD} | {b // lanes * lanes for b in bounds} | {-(-b // lanes) * lanes for b in bounds})
    for lo, hi in zip(edges[:-1], edges[1:]):
        cols = slice(lo, hi)
        src = slice(Z_C0 + lo, Z_C0 + hi)
        kinds = [n for n in range(len(bounds) + 1)
                 if (bounds[n - 1] if n else 0) < hi and lo < (bounds[n] if n < len(bounds) else C_PAD)]
        zs = neighbour(kinds[-1], src)
        lane = lo + lax.broadcasted_iota(jnp.int32, (1, hi - lo), 1)
        for n in reversed(kinds[:-1]):
            zs = jnp.where(lane < bounds[n], neighbour(n, src), zs)
        xs = zc_ref[:, src]
        z_ref[:, cols] = xs + (zs - xs) * mu_ref[:, cols]


def _prep_kernel(zc_ref, zp_ref, zn_ref, mu_ref, w0_ref, w2_ref, a0_ref, a2_ref, kk_ref, ka_ref, rk_ref,
                 g2_ref, e_ref, lg_ref, lb_ref, sw_ref, sb_ref,
                 rvk_o, kf_o, kb_o, ld_o, bv_o, g_o, mix_o, z_ref, *,
                 n_ctx_tiles, tiles_per_lat):
    _spatial_gating(zc_ref[:, 0:A_WIDTH], zc_ref[:, A_WIDTH:2 * A_WIDTH], lg_ref, lb_ref, sw_ref, sb_ref, mix_o)
    mix_o[:, A_WIDTH:] = jnp.zeros((mix_o.shape[0], MIX_WIDTH - A_WIDTH), BF16)
    i = pl.program_id(0)
    tm = zc_ref.shape[0]
    is_ctx = i < n_ctx_tiles
    q = (i - n_ctx_tiles) % tiles_per_lat
    half = C_IN // 2
    quarter = C_IN // 4

    @pl.when(is_ctx)
    def _():
        _token_shift_mix(z_ref, zc_ref, zp_ref, zn_ref, mu_ref, (half,), tm, False, False)

    @pl.when(jnp.logical_not(is_ctx))
    def _():
        _token_shift_mix(z_ref, zc_ref, zp_ref, zn_ref, mu_ref, (quarter, 2 * quarter, 3 * quarter), GRID_W,
                         q > 0, q < tiles_per_lat - 1)

    cw = C_WIDTH
    r = z_ref[:, 0:cw]
    k = z_ref[:, cw:2 * cw]
    v = z_ref[:, 2 * cw:3 * cw]
    wd = z_ref[:, 3 * cw:3 * cw + 128]
    ad = z_ref[:, 3 * cw + 128:3 * cw + 256]
    gd = z_ref[:, 3 * cw + 256:C_PAD]

    uw = _dot3_presplit(jnp.tanh(wd), w2_ref[...])
    ua = _dot3_presplit(ad, a2_ref[...])
    kk_raw = k * kk_ref[...]
    ss = _seg_sum(kk_raw * kk_raw, e_ref)
    kk = kk_raw * lax.rsqrt(jnp.maximum(ss, 1e-24))
    rvk_o[:, 0:cw] = r.astype(BF16)
    rvk_o[:, cw:2 * cw] = v.astype(BF16)
    rvk_o[:, 2 * cw:3 * cw] = kk.astype(BF16)
    km_sum = None
    for d, kd_o in enumerate((kf_o, kb_o)):
        u = uw[:, d * cw:(d + 1) * cw] + w0_ref[d:d + 1, :]
        ld_o[:, d * cw:(d + 1) * cw] = -float(np.exp(-0.5)) * _sigmoid(u)
        a = _sigmoid(ua[:, d * cw:(d + 1) * cw] + a0_ref[d:d + 1, :])
        km = k * (1.0 + (a - 1.0) * ka_ref[...])
        kd_o[:, 0:cw] = km.astype(BF16)
        kd_o[:, cw:2 * cw] = (kk * a).astype(BF16)
        km_sum = km if km_sum is None else km_sum + km
    bonus = _seg_sum(r * km_sum * rk_ref[...], e_ref)
    bv_o[...] = (bonus * v).astype(BF16)
    g_o[...] = _dot(_sigmoid(gd).astype(BF16), g2_ref[...]).astype(BF16)


def _rwkv_prep(z, p, layer, n_ctx_tiles, tiles_per_lat, tm):
    m = z.shape[0]
    hb = tm // GRID_W
    n_halo = m // GRID_W
    lay = lambda *shape: pl.BlockSpec((None,) + shape, lambda i: (layer,) + (0,) * len(shape))
    outs = [(3, BF16), (2, BF16), (2, BF16), (2, F32), (1, BF16), (1, BF16)]
    return pl.pallas_call(
        functools.partial(_prep_kernel, n_ctx_tiles=n_ctx_tiles, tiles_per_lat=tiles_per_lat),
        grid=(m // tm,),
        in_specs=[
            pl.BlockSpec((tm, Z_COLS), lambda i: (i, 0)),
            pl.BlockSpec((GRID_W, Z_COLS), lambda i: (jnp.where(i < n_ctx_tiles, 0, i * hb - 1), 0)),
            pl.BlockSpec((GRID_W, Z_COLS),
                         lambda i: (jnp.where(i < n_ctx_tiles, 0, jnp.minimum(i * hb + hb, n_halo - 1)), 0)),
            lay(1, C_PAD),
            lay(2, C_WIDTH),
            lay(384, 2 * C_WIDTH),
            lay(2, C_WIDTH),
            lay(384, 2 * C_WIDTH),
            lay(1, C_WIDTH),
            lay(1, C_WIDTH),
            lay(1, C_WIDTH),
            lay(256, C_WIDTH),
            pl.BlockSpec((512, 256), lambda i: (0, 0)),
            lay(1, A_WIDTH),
            lay(1, A_WIDTH),
            lay(A_HEADS, SGU_CHUNK, SGU_CHUNK),
            lay(SGU_CHUNK, A_WIDTH),
        ],
        out_specs=([pl.BlockSpec((tm, n * C_WIDTH), lambda i: (i, 0)) for n, _ in outs]
                   + [pl.BlockSpec((tm, MIX_WIDTH), lambda i: (i, 0))]),
        out_shape=([jax.ShapeDtypeStruct((m, n * C_WIDTH), dt) for n, dt in outs]
                   + [jax.ShapeDtypeStruct((m, MIX_WIDTH), BF16)]),
        scratch_shapes=[pltpu.VMEM((tm, C_PAD), F32)],
        compiler_params=_cparams(("arbitrary",)),
        name="rwkv_prep",
    )(z, z, z, p["mu"], p["w0"], p["w2"], p["a0"], p["a2"], p["k_k"], p["k_a"], p["r_k"], p["g2"],
      p["e256"], p["ln_g"], p["ln_b"], p["sgu_w"], p["sgu_bias"])


def _bmm(a, b):
    return jnp.einsum("uik,ukj->uij", a.astype(BF16), b.astype(BF16), preferred_element_type=F32)


def _bmm_nt(a, b):
    return jnp.einsum("uik,ujk->uij", a.astype(BF16), b.astype(BF16), preferred_element_type=F32)


def _block_diag(y, bd_mask):
    return jnp.where(bd_mask, jnp.concatenate([y] * UNIT_HEADS, axis=1), jnp.zeros((), y.dtype))


def _to_units(x):
    return jnp.stack([x[:, p * UNIT_W:(p + 1) * UNIT_W] for p in range(N_UNITS)], axis=0)


def _scan_chunks(streams, hs):
    t = SCAN_T
    w = UNIT_W
    ti = lax.broadcasted_iota(jnp.int32, (t, w), 0)
    si = lax.broadcasted_iota(jnp.int32, (t, w), 1) & (HEAD_DIM - 1)
    eye = jnp.where(si == ti, 1.0, 0.0)
    t_row = lax.broadcasted_iota(jnp.int32, (t, 3 * t), 0)
    t_col = lax.broadcasted_iota(jnp.int32, (t, 3 * t), 1) & (t - 1)
    bd_mask = (lax.broadcasted_iota(jnp.int32, (w, w), 0) // HEAD_DIM
               == lax.broadcasted_iota(jnp.int32, (w, w), 1) // HEAD_DIM)

    q, ai, ki, vb, vf, lhs, p_tot = [], [], [], [], [], [], []
    for reverse, r, v, kk, ld, km, ka in streams:
        tri = jnp.where((t_col >= t_row) if reverse else (t_col <= t_row), 1.0, 0.0).astype(BF16)
        cum = _dot_exact_lhs(tri, ld)
        c_end = cum[0:1, :] if reverse else cum[t - 1:t, :]
        p_end = jnp.exp(c_end - cum)
        p_inv = jnp.exp(-cum)
        q.append(_to_units(jnp.concatenate([kk * jnp.exp(cum - ld), r * jnp.exp(cum)], axis=0).astype(BF16)))
        ai.append(_to_units((ka * p_inv).astype(BF16)))
        ki.append(_to_units((km * p_inv).astype(BF16)))
        vb.append(_to_units(v.astype(BF16)))
        vf.append(_to_units(v))
        lhs.append(_to_units(jnp.concatenate([ka * p_end, km * p_end], axis=0).astype(BF16)))
        p_tot.append(_to_units(jnp.exp(c_end)))
    q, ai, ki, vb, vf, lhs, p_tot = (jnp.concatenate(a, axis=0) for a in (q, ai, ki, vb, vf, lhs, p_tot))

    def masked(x, strictly):
        parts = []
        for n, stream in enumerate(streams):
            if stream[0]:
                keep = si > ti if strictly else si >= ti
            else:
                keep = si < ti if strictly else si <= ti
            parts.append(jnp.where(keep, x[n * N_UNITS:(n + 1) * N_UNITS], 0.0))
        return jnp.concatenate(parts, axis=0)

    rr = jnp.concatenate([_block_diag(ai, bd_mask), _block_diag(ki, bd_mask)], axis=1)
    sc = _bmm_nt(q, rr)
    la = masked(sc[:, 0:t, 0:w], True)
    lk = masked(sc[:, 0:t, w:2 * w], True)
    ma = masked(sc[:, t:2 * t, 0:w], False)
    mk = masked(sc[:, t:2 * t, w:2 * w], False)

    n_pow = -la
    x_inv = eye + n_pow
    n_pow = _bmm(n_pow, _block_diag(n_pow.astype(BF16), bd_mask))
    for _ in range(int(np.log2(t)) - 2):
        both = _bmm(jnp.concatenate([n_pow, x_inv], axis=1), _block_diag(n_pow.astype(BF16), bd_mask))
        x_inv = x_inv + both[:, t:2 * t]
        n_pow = both[:, 0:t]
    x_inv = x_inv + _bmm(x_inv, _block_diag(n_pow.astype(BF16), bd_mask))

    hb = _bmm_nt(q, hs)
    lv = _bmm(jnp.concatenate([lk, mk], axis=1), _block_diag(vb, bd_mask))
    u = -_bmm(x_inv, _block_diag((hb[:, 0:t] + lv[:, 0:t]).astype(BF16), bd_mask))
    y = hb[:, t:2 * t] + lv[:, t:2 * t] + _bmm(ma, _block_diag(u.astype(BF16), bd_mask))
    rhs_t = jnp.swapaxes(jnp.concatenate([u, vf], axis=1), 1, 2)
    upd = _bmm(rhs_t, lhs)
    hs_new = p_tot * hs + jnp.where(bd_mask, upd, 0.0)
    return y, hs_new


def _scan_kernel(*refs, has_s0, emit_state):
    ins = refs[:6]
    pos = 6
    s0_ref = None
    if has_s0:
        s0_ref = refs[pos]
        pos += 1
    y_refs = refs[pos:pos + 2]
    pos += 2
    st_ref = None
    if emit_state:
        st_ref = refs[pos]
        pos += 1
    h_ref = refs[pos]
    c = pl.program_id(1)
    hd = HEAD_DIM

    n_group = h_ref.shape[0]

    @pl.when(c == 0)
    def _():
        if has_s0:
            zero = jnp.zeros((hd, hd), F32)
            for s in range(n_group):
                for d in range(2):
                    for p in range(N_UNITS):
                        rows = [jnp.concatenate([s0_ref[s, d, UNIT_HEADS * p + a] if a == b else zero
                                                 for b in range(UNIT_HEADS)], axis=1) for a in range(UNIT_HEADS)]
                        h_ref[s, d, p] = jnp.concatenate(rows, axis=0)
        else:
            h_ref[...] = jnp.zeros_like(h_ref)

    streams = []
    for s in range(n_group):
        for d in range(2):
            rvk_ref, kd_ref, ld_ref = ins[3 * d:3 * d + 3]
            r, v, kk = (rvk_ref[s, :, n * C_WIDTH:(n + 1) * C_WIDTH].astype(F32) for n in range(3))
            km, ka = (kd_ref[s, :, n * C_WIDTH:(n + 1) * C_WIDTH].astype(F32) for n in range(2))
            streams.append((d == 1, r, v, kk, ld_ref[s], km, ka))
    n_all = len(streams) * N_UNITS
    y, hs_new = _scan_chunks(streams, h_ref[...].reshape(n_all, UNIT_W, UNIT_W))
    h_ref[...] = hs_new.reshape(h_ref.shape)
    for s in range(n_group):
        for d, y_ref in enumerate(y_refs):
            for p in range(N_UNITS):
                y_ref[s, :, p * UNIT_W:(p + 1) * UNIT_W] = y[(2 * s + d) * N_UNITS + p].astype(BF16)

    if emit_state:
        @pl.when(c == pl.num_programs(1) - 1)
        def _():
            for s in range(n_group):
                for d in range(2):
                    for p in range(N_UNITS):
                        hs = h_ref[s, d, p]
                        for a in range(UNIT_HEADS):
                            st_ref[s, d, UNIT_HEADS * p + a] = hs[a * hd:(a + 1) * hd, a * hd:(a + 1) * hd]


def _rwkv_scan(arrs, n_seq, n_chunk, row0, layer, group, s0=None, emit_state=False):
    seq_len = n_chunk * SCAN_T
    m_all = arrs[0].shape[0]
    assert row0 % (seq_len * group) == 0 and m_all % seq_len == 0 and n_seq % group == 0
    rvk, kf, kb, ld = (a.reshape(m_all // seq_len, seq_len, a.shape[1]) for a in arrs)
    g0 = row0 // (seq_len * group)
    fwd = lambda b, c: (b, c, 0)
    bwd = lambda b, c: (b, n_chunk - 1 - c, 0)
    blk = lambda imap, n=1: pl.BlockSpec((group, SCAN_T, n * C_WIDTH), imap)
    in_specs = [blk(lambda b, c: (g0 + b, c, 0), 3), blk(lambda b, c: (g0 + b, c, 0), 2),
                blk(lambda b, c: (g0 + b, c, 0)),
                blk(lambda b, c: (g0 + b, n_chunk - 1 - c, 0), 3), blk(lambda b, c: (g0 + b, n_chunk - 1 - c, 0), 2),
                blk(lambda b, c: (g0 + b, n_chunk - 1 - c, 1))]
    args = [rvk, kf, ld, rvk, kb, ld]
    st_dims = (2, N_HEADS, HEAD_DIM, HEAD_DIM)
    if s0 is not None:
        in_specs.append(pl.BlockSpec((group, None) + st_dims, lambda b, c: (b, layer, 0, 0, 0, 0)))
        args.append(s0)
    y_shape = jax.ShapeDtypeStruct((n_seq, seq_len, C_WIDTH), BF16)
    out_specs = [blk(fwd), blk(bwd)]
    out_shape = [y_shape, y_shape]
    if emit_state:
        out_specs.append(pl.BlockSpec((group,) + st_dims, lambda b, c: (b, 0, 0, 0, 0)))
        out_shape.append(jax.ShapeDtypeStruct((n_seq,) + st_dims, F32))
    outs = pl.pallas_call(
        functools.partial(_scan_kernel, has_s0=s0 is not None, emit_state=emit_state),
        grid=(n_seq // group, n_chunk),
        in_specs=in_specs,
        out_specs=out_specs,
        out_shape=out_shape,
        scratch_shapes=[pltpu.VMEM((group, 2, N_UNITS, UNIT_W, UNIT_W), F32)],
        compiler_params=_cparams(("arbitrary", "arbitrary")),
        name="rwkv_scan",
    )(*args)
    ys = [y.reshape(n_seq * seq_len, C_WIDTH) for y in outs[:2]]
    return ys + list(outs[2:])


def _post_kernel(yf1_ref, yb1_ref, yf2_ref, yb2_ref, bv_ref, g_ref, lg_ref, lb_ref, e_ref, mix_ref, o_ref, *,
                 n_ctx_tiles):
    del mix_ref
    is_ctx = pl.program_id(0) < n_ctx_tiles
    f32 = lambda ref: ref[...].astype(F32)
    y = jnp.where(is_ctx, f32(yf1_ref) + f32(yb1_ref), f32(yf2_ref) + f32(yb2_ref))
    inv = 1.0 / HEAD_DIM
    m = _seg_sum(y, e_ref) * inv
    yc = y - m
    var = _seg_sum(yc * yc, e_ref) * inv
    yn = yc * lax.rsqrt(var + LNX_EPS) * lg_ref[...] + lb_ref[...]
    o_ref[...] = ((yn + bv_ref[...]) * g_ref[...]).astype(BF16)


def _rwkv_post(y_ctx, y_lat, bv, g, lnx_g, lnx_b, e256, mix, layer, n_ctx_tiles, tm):
    m = bv.shape[0]
    n_tiles = m // tm
    row = pl.BlockSpec((tm, C_WIDTH), lambda i: (i, 0))
    row_ctx = pl.BlockSpec((tm, C_WIDTH), lambda i: (jnp.minimum(i, n_ctx_tiles - 1), 0))
    row_lat = pl.BlockSpec((tm, C_WIDTH), lambda i: (jnp.maximum(i - n_ctx_tiles, 0), 0))
    vec = pl.BlockSpec((None, 1, C_WIDTH), lambda i: (layer, 0, 0))
    return pl.pallas_call(
        functools.partial(_post_kernel, n_ctx_tiles=n_ctx_tiles),
        grid=(n_tiles,),
        in_specs=[row_ctx, row_ctx, row_lat, row_lat, row, row, vec, vec,
                  pl.BlockSpec((512, 256), lambda i: (0, 0)), pl.BlockSpec(memory_space=pl.ANY)],
        out_specs=pl.BlockSpec((tm, C_WIDTH), lambda i: (i, (A_WIDTH + B_WIDTH) // C_WIDTH)),
        out_shape=jax.ShapeDtypeStruct(mix.shape, mix.dtype),
        input_output_aliases={9: 0},
        compiler_params=_cparams(("arbitrary",)),
        name="rwkv_post",
    )(y_ctx[0], y_ctx[1], y_lat[0], y_lat[1], bv, g, lnx_g, lnx_b, e256, mix)


def kernel(x_prompt, x_sample, state_wkv, c, c_ctx, norm_g, w_mod, b_mod, ffn_w_in, ffn_w_out, w_in, w_out,
           sgu_ln_g, sgu_ln_b, sgu_w, sgu_b, shift_mu, decay_w0, decay_w2, iclr_a0, iclr_a2, k_k, k_a, r_k,
           gate_w2, lnx_g, lnx_b, final_g):
    batch, seq, d = x_prompt.shape
    dec_batch, dec_seq, _ = x_sample.shape
    depth = w_mod.shape[0]
    assert d == D_MODEL and dec_batch + 1 <= MOD_ROWS
    m_ctx = batch * seq
    tm = 1024
    ts = 512
    tp = 256
    assert seq == tp and dec_seq == tm and m_ctx % tm == 0 and dec_seq % GRID_W == 0

    def row_of_tile_fn(rows):
        n_ctx = m_ctx // rows
        per_lat = dec_seq // rows
        return lambda i: jnp.where(i < n_ctx, 0, 1 + (i - n_ctx) // per_lat)

    zeros_cw = jnp.zeros((depth, DECAY_RANK, C_WIDTH), F32)

    def both_dirs(w):
        top = jnp.concatenate([w[:, 0], zeros_cw], axis=-1)
        bot = jnp.concatenate([zeros_cw, w[:, 1]], axis=-1)
        return jnp.concatenate([top, bot], axis=1)

    e_np = (np.arange(256)[:, None] // HEAD_DIM == np.arange(256)[None, :] // HEAD_DIM)
    prep_params = {
        "mu": jnp.pad(shift_mu, ((0, 0), (0, C_PAD - C_IN))).reshape(depth, 1, C_PAD),
        "w0": decay_w0, "w2": _split_rows3(both_dirs(decay_w2)), "a0": iclr_a0, "a2": _split_rows3(both_dirs(iclr_a2)),
        "k_k": k_k.reshape(depth, 1, C_WIDTH), "k_a": k_a.reshape(depth, 1, C_WIDTH),
        "r_k": r_k.reshape(depth, 1, C_WIDTH),
        "g2": jnp.pad(gate_w2, ((0, 0), (0, 256 - GATE_RANK), (0, 0))).astype(BF16),
        "e256": jnp.asarray(np.concatenate([e_np, e_np], axis=0), BF16),
    }
    cd, sd = _dft_mats(B_GROUP_DIM)
    cs, ss = _dft_mats(seq)
    cl, sl = _dft_mats(dec_seq)
    fnet_consts = (_hi_lo(np.concatenate([cd, sd], axis=1)) + _hi_lo(np.concatenate([cs, -ss], axis=1))
                   + _hi_lo(np.concatenate([cl, -sl], axis=1)))
    prep_params.update({
        "sgu_bias": jnp.repeat(jnp.swapaxes(sgu_b, 1, 2), A_WIDTH // A_HEADS, axis=2),
        "sgu_w": sgu_w.astype(BF16),
        "ln_g": sgu_ln_g.reshape(depth, 1, A_WIDTH), "ln_b": sgu_ln_b.reshape(depth, 1, A_WIDTH),
    })
    lnx_g3 = lnx_g.reshape(depth, 1, C_WIDTH)
    lnx_b3 = lnx_b.reshape(depth, 1, C_WIDTH)
    norm_g4 = norm_g.reshape(depth, 3, 1, d)

    cond = jnp.concatenate([c_ctx[None, :], c, jnp.zeros((MOD_ROWS - 1 - dec_batch, d), F32)], axis=0)
    mod = _modulation(cond, w_mod, b_mod).reshape(depth, MOD_ROWS, N_MOD, 1, d)

    m = m_ctx + dec_batch * dec_seq
    n_ctx_t = m_ctx // tm
    n_lat_t = dec_batch * dec_seq // tm
    rot = row_of_tile_fn(tm)
    ctx_row = lambda i: 0
    lat_row = lambda i: 1 + i
    n_chunk_ctx = seq // SCAN_T
    n_chunk_lat = dec_seq // SCAN_T
    tk = 1024
    scan_group = lambda n_seq: 4 if n_seq % 4 == 0 else 1
    w_in_t = jnp.swapaxes(w_in, 1, 2).astype(BF16)
    w_out_b = w_out.astype(BF16)
    states = []
    x = None
    for l in range(depth):
        ffn_args = (mod, norm_g4, ffn_w_in, ffn_w_out, l)
        if l == 0:
            x = _ffn(x_prompt.reshape(m_ctx, d), *ffn_args, 0, 0, rot, tm, n_ctx_t + n_lat_t,
                     x_tail=x_sample.reshape(m - m_ctx, d))
        else:
            x = _ffn(x, *ffn_args, 0, 0, rot, tm, n_ctx_t + n_lat_t)
        z = _in_proj(x, mod, norm_g4, w_in_t, l, rot, tm)
        *scan_in, bv, g, mix = _rwkv_prep(z, prep_params, l, m_ctx // tp, dec_seq // tp, tp)
        mix = _fnet(z, fnet_consts, mix, m_ctx // tm, seq, tm)
        yf_c, yb_c, state = _rwkv_scan(scan_in, batch, n_chunk_ctx, 0, l, scan_group(batch), emit_state=True)
        states.append(state)
        yf_l, yb_l = _rwkv_scan(scan_in, dec_batch, n_chunk_lat, m_ctx, l, scan_group(dec_batch), s0=state_wkv)
        mix = _rwkv_post((yf_c, yb_c), (yf_l, yb_l), bv, g, lnx_g3, lnx_b3, prep_params["e256"], mix, l,
                         m_ctx // ts, ts)
        x = _resid_matmul(x, mod, mix, w_out_b, pl.BlockSpec((None, tk, D_MODEL), lambda i, k: (l, k, 0)),
                          l, 5, 1.0, rot, tm, tk, "mix_out")
        if l < depth - 1:
            x = _ffn(x, *ffn_args, 2, 1, rot, tm, n_ctx_t + n_lat_t)
    fg = final_g.reshape(1, d)
    y_ctx = _ffn(x, *ffn_args, 2, 1, ctx_row, tm, n_ctx_t, final_g=fg)
    y_lat = _ffn(x, *ffn_args, 2, 1, lat_row, tm, n_lat_t, in_tile0=n_ctx_t, final_g=fg)
    return (y_ctx.reshape(batch, seq, d), y_lat.reshape(dec_batch, dec_seq, d), jnp.stack(states, axis=1))
```

```python
import functools

import numpy as np
import jax
import jax.numpy as jnp
from jax import lax
from jax.experimental import pallas as pl
from jax.experimental.pallas import tpu as pltpu

F32 = jnp.float32
BF16 = jnp.bfloat16

D_MODEL = 2048
GRID_W = 64
SGU_CHUNK = 128
A_HEADS = 4
A_WIDTH = 512
B_WIDTH = 512
B_GROUP_DIM = 128
C_WIDTH = 1024
MIX_WIDTH = A_WIDTH + B_WIDTH + C_WIDTH
HEAD_DIM = 64
N_HEADS = 16
UNIT_HEADS = 2
UNIT_W = UNIT_HEADS * HEAD_DIM
N_UNITS = N_HEADS // UNIT_HEADS
DECAY_RANK = 64
GATE_RANK = 160
C_IN = 3488
C_PAD = 3584
IN_COLS = 2 * A_WIDTH + B_WIDTH + C_IN
Z_BLOCK = 1024
Z_C0 = 2 * A_WIDTH + B_WIDTH
Z_COLS = Z_C0 + C_PAD
D_FF = 5632
N_MOD = 9
RMS_EPS = 1e-6
LN_EPS = 1e-5
LNX_EPS = 64e-5
SCAN_T = 64
MOD_ROWS = 8

VMEM_LIMIT = 56 * 1024 * 1024
FFN_VMEM_LIMIT = 60 * 1024 * 1024


def _cparams(sem):
    return pltpu.CompilerParams(dimension_semantics=sem, vmem_limit_bytes=VMEM_LIMIT)


def _dot(a, b):
    return jnp.dot(a, b, preferred_element_type=F32)


def _split3(x):
    hi = x.astype(BF16)
    r1 = x - hi.astype(F32)
    mid = r1.astype(BF16)
    lo = (r1 - mid.astype(F32)).astype(BF16)
    return hi, mid, lo


def _dot_ones_rhs(x, e2):
    hi = x.astype(BF16)
    lo = (x - hi.astype(F32)).astype(BF16)
    return _dot(jnp.concatenate([hi, lo], axis=1), e2)


def _dot_exact_lhs(e3, x):
    return _dot(e3, jnp.concatenate(_split3(x), axis=0))


def _split_rows3(b):
    bh = b.astype(BF16)
    bl = (b - bh.astype(F32)).astype(BF16)
    return jnp.concatenate([bh, bh, bl], axis=-2)


def _dot3_presplit(a, b3):
    ah = a.astype(BF16)
    al = (a - ah.astype(F32)).astype(BF16)
    return _dot(jnp.concatenate([ah, al, ah], axis=1), b3)


def _sigmoid(x):
    return 0.5 * jnp.tanh(0.5 * x) + 0.5


def _silu(x):
    return x * _sigmoid(x)


def _gelu_tanh(x):
    return 0.5 * x * (1.0 + jnp.tanh(0.7978845608028654 * (x + 0.044715 * (x * x * x))))


def _mod_kernel(c_ref, w_ref, b_ref, o_ref):
    s = _silu(c_ref[...]).astype(BF16)
    o_ref[...] = _dot(s, w_ref[...].astype(BF16)) + b_ref[...]


def _modulation(cond, w_mod, b_mod):
    depth = w_mod.shape[0]
    n = w_mod.shape[2]
    tn = 1024
    return pl.pallas_call(
        _mod_kernel,
        grid=(depth, n // tn),
        in_specs=[
            pl.BlockSpec((MOD_ROWS, D_MODEL), lambda l, j: (0, 0)),
            pl.BlockSpec((None, D_MODEL, tn), lambda l, j: (l, 0, j)),
            pl.BlockSpec((None, 1, tn), lambda l, j: (l, 0, j)),
        ],
        out_specs=pl.BlockSpec((None, MOD_ROWS, tn), lambda l, j: (l, 0, j)),
        out_shape=jax.ShapeDtypeStruct((depth, MOD_ROWS, n), F32),
        compiler_params=_cparams(("arbitrary", "arbitrary")),
        name="modulation",
    )(cond, w_mod, b_mod.reshape(depth, 1, n))


def _mod_spec(layer, slot, row_of_tile):
    return pl.BlockSpec((None, None, None, 1, D_MODEL),
                        lambda i, *_: (layer, row_of_tile(i), slot, 0, 0))


def _norm_spec(layer, slot):
    return pl.BlockSpec((None, None, 1, D_MODEL), lambda i, *_: (layer, slot, 0, 0))


def _modulated_norm(x, g_ref, sc_ref, sh_ref):
    rs = lax.rsqrt(jnp.mean(x * x, axis=-1, keepdims=True) + RMS_EPS)
    gain = g_ref[...] * (1.0 + sc_ref[...])
    return ((x * rs) * gain + sh_ref[...]).astype(BF16)


def _in_proj_kernel(x_ref, sh_ref, sc_ref, g_ref, wt_ref, o_ref, h_ref, *, valid_cols):
    j = pl.program_id(1)

    def step(first):
        if first:
            h = _modulated_norm(x_ref[...], g_ref, sc_ref, sh_ref)
            h_ref[...] = h
        else:
            h = h_ref[...]
        out = lax.dot_general(h, wt_ref[...].astype(BF16), (((1,), (1,)), ((), ())),
                              preferred_element_type=F32)
        tn = out.shape[1]
        if valid_cols % tn:
            col = j * tn + lax.broadcasted_iota(jnp.int32, (1, tn), 1)
            out = jnp.where(col < valid_cols, out, 0.0)
        o_ref[...] = out

    pl.when(j == 0)(lambda: step(True))
    pl.when(j > 0)(lambda: step(False))


def _in_proj(x, mod, norm_g, w_in_t, layer, row_of_tile, tm):
    m = x.shape[0]
    nb = Z_COLS // Z_BLOCK
    assert -(-IN_COLS // Z_BLOCK) == nb
    return pl.pallas_call(
        functools.partial(_in_proj_kernel, valid_cols=IN_COLS),
        grid=(m // tm, nb),
        in_specs=[
            pl.BlockSpec((tm, D_MODEL), lambda i, j: (i, 0)),
            _mod_spec(layer, 3, row_of_tile),
            _mod_spec(layer, 4, row_of_tile),
            _norm_spec(layer, 1),
            pl.BlockSpec((None, Z_BLOCK, D_MODEL), lambda i, j: (layer, j, 0)),
        ],
        out_specs=pl.BlockSpec((tm, Z_BLOCK), lambda i, j: (i, j)),
        out_shape=jax.ShapeDtypeStruct((m, Z_COLS), F32),
        scratch_shapes=[pltpu.VMEM((tm, D_MODEL), BF16)],
        compiler_params=_cparams(("arbitrary", "arbitrary")),
        name="in_proj",
    )(x, mod, mod, norm_g, w_in_t)


def _ffn_kernel(x_ref, sh_ref, sc_ref, gt_ref, g_ref, wg_ref, wu_ref, wo_ref, *rest, tn, final_norm, n_first):
    fg_ref = rest[0] if final_norm else None
    o_ref, h_ref = rest[-2:]
    j = pl.program_id(1)
    last = pl.num_programs(1) - 1
    if n_first is None:
        x_at = lambda cols: x_ref[:, cols]
    else:
        x_tail_ref = rest[-3]
        from_first = pl.program_id(0) < n_first
        x_at = lambda cols: jnp.where(from_first, x_ref[:, cols], x_tail_ref[:, cols])

    def step(first, final):
        if first:
            h = _modulated_norm(x_at(slice(None)), g_ref, sc_ref, sh_ref)
            h_ref[...] = h
        else:
            h = h_ref[...]
        gate = _dot(h, wg_ref[...].astype(BF16))
        up = _dot(h, wu_ref[...].astype(BF16))
        a = (_silu(gate) * up).astype(BF16)
        for n in range(o_ref.shape[1] // tn):
            cols = slice(n * tn, (n + 1) * tn)
            acc = _dot(a, wo_ref[:, cols].astype(BF16))
            if not first:
                acc = o_ref[:, cols] + acc
            if final:
                acc = x_at(cols) + (0.5 * gt_ref[:, cols]) * acc
            o_ref[:, cols] = acc
        if final and final_norm:
            y = o_ref[...]
            rs = lax.rsqrt(jnp.mean(y * y, axis=-1, keepdims=True) + RMS_EPS)
            o_ref[...] = y * rs * fg_ref[...]

    pl.when(j == 0)(lambda: step(True, False))
    pl.when(jnp.logical_and(j > 0, j < last))(lambda: step(False, False))
    pl.when(j == last)(lambda: step(False, True))


def _ffn(x, mod, norm_g, w_in, w_out, layer, sub, ffn_idx, row_of_tile, tm, n_tiles, in_tile0=0, x_tail=None,
         final_g=None, tf=256):
    nf = D_FF // tf
    args = [x, mod, mod, mod, norm_g, w_in, w_in, w_out]
    n_first = None
    if x_tail is None:
        x_spec = pl.BlockSpec((tm, D_MODEL), lambda i, j: (in_tile0 + i, 0))
    else:
        n_first = x.shape[0] // tm
        x_spec = pl.BlockSpec((tm, D_MODEL), lambda i, j: (jnp.minimum(i, n_first - 1), 0),
                              pipeline_mode=pl.Buffered(1))
    in_specs = [
        x_spec,
        _mod_spec(layer, 3 * sub, row_of_tile),
        _mod_spec(layer, 3 * sub + 1, row_of_tile),
        _mod_spec(layer, 3 * sub + 2, row_of_tile),
        _norm_spec(layer, sub),
        pl.BlockSpec((None, None, D_MODEL, tf), lambda i, j: (layer, ffn_idx, 0, j)),
        pl.BlockSpec((None, None, D_MODEL, tf), lambda i, j: (layer, ffn_idx, 0, nf + j)),
        pl.BlockSpec((None, None, tf, D_MODEL), lambda i, j: (layer, ffn_idx, j, 0)),
    ]
    if final_g is not None:
        args.append(final_g)
        in_specs.append(pl.BlockSpec((1, D_MODEL), lambda i, j: (0, 0)))
    if x_tail is not None:
        args.append(x_tail)
        in_specs.append(pl.BlockSpec((tm, D_MODEL), lambda i, j: (jnp.maximum(i - n_first, 0), 0),
                                     pipeline_mode=pl.Buffered(1)))
    return pl.pallas_call(
        functools.partial(_ffn_kernel, tn=512, final_norm=final_g is not None, n_first=n_first),
        grid=(n_tiles, nf),
        in_specs=in_specs,
        out_specs=pl.BlockSpec((tm, D_MODEL), lambda i, j: (i, 0)),
        out_shape=jax.ShapeDtypeStruct((n_tiles * tm, D_MODEL), F32),
        scratch_shapes=[pltpu.VMEM((tm, D_MODEL), BF16)],
        compiler_params=pltpu.CompilerParams(dimension_semantics=("arbitrary", "arbitrary"),
                                             vmem_limit_bytes=FFN_VMEM_LIMIT),
        name="ffn",
    )(*args)


def _resid_matmul_kernel(x_ref, gt_ref, a_ref, w_ref, o_ref, *, coef, tn):
    k = pl.program_id(1)
    last = pl.num_programs(1) - 1

    def step(first, final):
        a = a_ref[...]
        for n in range(o_ref.shape[1] // tn):
            cols = slice(n * tn, (n + 1) * tn)
            acc = _dot(a, w_ref[:, cols].astype(BF16))
            if not first:
                acc = o_ref[:, cols] + acc
            if final:
                acc = x_ref[:, cols] + (coef * gt_ref[:, cols]) * acc
            o_ref[:, cols] = acc

    pl.when(k == 0)(lambda: step(True, False))
    pl.when(jnp.logical_and(k > 0, k < last))(lambda: step(False, False))
    pl.when(k == last)(lambda: step(False, True))


def _resid_matmul(x, mod, a, w, w_spec, layer, gate_slot, coef, row_of_tile, tm, tk, name):
    m = x.shape[0]
    return pl.pallas_call(
        functools.partial(_resid_matmul_kernel, coef=coef, tn=512),
        grid=(m // tm, a.shape[1] // tk),
        in_specs=[
            pl.BlockSpec((tm, D_MODEL), lambda i, k: (i, 0)),
            _mod_spec(layer, gate_slot, row_of_tile),
            pl.BlockSpec((tm, tk), lambda i, k: (i, k)),
            w_spec,
        ],
        out_specs=pl.BlockSpec((tm, D_MODEL), lambda i, k: (i, 0)),
        out_shape=jax.ShapeDtypeStruct((m, D_MODEL), F32),
        compiler_params=_cparams(("arbitrary", "arbitrary")),
        name=name,
    )(x, mod, a, w)


def _spatial_gating(zu, zv, lg_ref, lb_ref, w_ref, bias_ref, o_ref):
    u = _gelu_tanh(zu)
    v = _gelu_tanh(zv)
    mu = jnp.mean(v, axis=-1, keepdims=True)
    vc = v - mu
    var = jnp.mean(vc * vc, axis=-1, keepdims=True)
    vn = (vc * lax.rsqrt(var + LN_EPS) * lg_ref[...] + lb_ref[...]).astype(BF16)
    hd = A_WIDTH // A_HEADS
    for c in range(u.shape[0] // SGU_CHUNK):
        rows = slice(c * SGU_CHUNK, (c + 1) * SGU_CHUNK)
        for h in range(A_HEADS):
            cols = slice(h * hd, (h + 1) * hd)
            mixed = _dot(w_ref[h], vn[rows, cols]) + bias_ref[:, cols]
            o_ref[rows, cols] = (u[rows, cols] * mixed).astype(BF16)


def _dft_mats(n):
    idx = np.arange(n)
    ang = 2.0 * np.pi * ((idx[:, None] * idx[None, :]) % n) / n
    return np.cos(ang) / np.sqrt(n), np.sin(ang) / np.sqrt(n)


def _hi_lo(a):
    a32 = jnp.asarray(a, F32)
    hi = a32.astype(BF16)
    lo = (a32 - hi.astype(F32)).astype(BF16)
    return hi, lo


def _fnet_kernel(z_ref, fdh_ref, fdl_ref, fsh_ref, fsl_ref, flh_ref, fll_ref, mix_ref, o_ref, t_ref, *,
                 n_ctx_tiles, seq):
    del mix_ref
    tm = z_ref.shape[0]
    x = z_ref[...]
    xh = x.astype(BF16)
    xl = (x - xh.astype(F32)).astype(BF16)
    gd = B_GROUP_DIM
    for g in range(B_WIDTH // gd):
        cols = slice(g * gd, (g + 1) * gd)
        t = _dot(xh[:, cols], fdh_ref[...]) + _dot(xl[:, cols], fdh_ref[...]) + _dot(xh[:, cols], fdl_ref[...])
        t_ref[0:tm, cols] = t[:, 0:gd]
        t_ref[tm:2 * tm, cols] = t[:, gd:2 * gd]

    def position_dft(fh_ref, fl_ref, rows_in, rows_out):
        tc = t_ref[rows_in[0], :]
        ts = t_ref[rows_in[1], :]
        tt = jnp.concatenate([tc, ts], axis=0)
        th = tt.astype(BF16)
        tl = (tt - th.astype(F32)).astype(BF16)
        out = _dot(fh_ref[...], th) + _dot(fl_ref[...], th) + _dot(fh_ref[...], tl)
        o_ref[rows_out, :] = out.astype(BF16)

    is_ctx = pl.program_id(0) < n_ctx_tiles

    @pl.when(is_ctx)
    def _():
        for s in range(tm // seq):
            r0 = slice(s * seq, (s + 1) * seq)
            r1 = slice(tm + s * seq, tm + (s + 1) * seq)
            position_dft(fsh_ref, fsl_ref, (r0, r1), r0)

    @pl.when(jnp.logical_not(is_ctx))
    def _():
        position_dft(flh_ref, fll_ref, (slice(0, tm), slice(tm, 2 * tm)), slice(0, tm))


def _fnet(z, consts, mix, n_ctx_tiles, seq, tm):
    m = z.shape[0]
    cb = 2 * A_WIDTH // B_WIDTH
    full = lambda a: pl.BlockSpec(a.shape, lambda i: (0,) * a.ndim)
    return pl.pallas_call(
        functools.partial(_fnet_kernel, n_ctx_tiles=n_ctx_tiles, seq=seq),
        grid=(m // tm,),
        in_specs=([pl.BlockSpec((tm, B_WIDTH), lambda i: (i, cb))] + [full(a) for a in consts]
                  + [pl.BlockSpec(memory_space=pl.ANY)]),
        out_specs=pl.BlockSpec((tm, B_WIDTH), lambda i: (i, A_WIDTH // B_WIDTH)),
        out_shape=jax.ShapeDtypeStruct(mix.shape, mix.dtype),
        input_output_aliases={1 + len(consts): 0},
        scratch_shapes=[pltpu.VMEM((2 * tm, B_WIDTH), F32)],
        compiler_params=_cparams(("arbitrary",)),
        name="fnet",
    )(z, *consts, mix)


def _seg_sum(x, e_ref):
    w = e_ref.shape[1]
    parts = [_dot_ones_rhs(x[:, b * w:(b + 1) * w], e_ref[...]) for b in range(x.shape[1] // w)]
    return parts[0] if len(parts) == 1 else jnp.concatenate(parts, axis=1)


def _token_shift_mix(z_ref, zc_ref, zp_ref, zn_ref, mu_ref, bounds, period, up_ok, dn_ok):
    tm = zc_ref.shape[0]
    row = lax.broadcasted_iota(jnp.int32, (tm, 1), 0)
    pos = row & (period - 1)
    lanes = 128

    def neighbour(kind, src):
        xs = zc_ref[:, src]
        if kind == 0:
            return jnp.where(pos == 0, 0.0, pltpu.roll(xs, 1, 0))
        if kind == 1:
            return jnp.where(pos == period - 1, 0.0, pltpu.roll(xs, tm - 1, 0))
        if kind == 2:
            return jnp.concatenate([jnp.where(up_ok, zp_ref[:, src], 0.0), xs[:tm - GRID_W]], axis=0)
        return jnp.concatenate([xs[GRID_W:], jnp.where(dn_ok, zn_ref[:, src], 0.0)], axis=0)

    edges = sorted({0, C_PAD} | {b // lanes * lanes for b in bounds} | {-(-b // lanes) * lanes for b in bounds})
    for lo, hi in zip(edges[:-1], edges[1:]):
        cols = slice(lo, hi)
        src = slice(Z_C0 + lo, Z_C0 + hi)
        kinds = [n for n in range(len(bounds) + 1)
                 if (bounds[n - 1] if n else 0) < hi and lo < (bounds[n] if n < len(bounds) else C_PAD)]
        zs = neighbour(kinds[-1], src)
        lane = lo + lax.broadcasted_iota(jnp.int32, (1, hi - lo), 1)
        for n in reversed(kinds[:-1]):
            zs = jnp.where(lane < bounds[n], neighbour(n, src), zs)
        xs = zc_ref[:, src]
        z_ref[:, cols] = xs + (zs - xs) * mu_ref[:, cols]


def _prep_kernel(zc_ref, zp_ref, zn_ref, mu_ref, w0_ref, w2_ref, a0_ref, a2_ref, kk_ref, ka_ref, rk_ref,
                 g2_ref, e_ref, lg_ref, lb_ref, sw_ref, sb_ref,
                 rvk_o, kf_o, kb_o, ld_o, bv_o, g_o, mix_o, z_ref, *,
                 n_ctx_tiles, tiles_per_lat):
    _spatial_gating(zc_ref[:, 0:A_WIDTH], zc_ref[:, A_WIDTH:2 * A_WIDTH], lg_ref, lb_ref, sw_ref, sb_ref, mix_o)
    mix_o[:, A_WIDTH:] = jnp.zeros((mix_o.shape[0], MIX_WIDTH - A_WIDTH), BF16)
    i = pl.program_id(0)
    tm = zc_ref.shape[0]
    is_ctx = i < n_ctx_tiles
    q = (i - n_ctx_tiles) % tiles_per_lat
    half = C_IN // 2
    quarter = C_IN // 4

    @pl.when(is_ctx)
    def _():
        _token_shift_mix(z_ref, zc_ref, zp_ref, zn_ref, mu_ref, (half,), tm, False, False)

    @pl.when(jnp.logical_not(is_ctx))
    def _():
        _token_shift_mix(z_ref, zc_ref, zp_ref, zn_ref, mu_ref, (quarter, 2 * quarter, 3 * quarter), GRID_W,
                         q > 0, q < tiles_per_lat - 1)

    cw = C_WIDTH
    r = z_ref[:, 0:cw]
    k = z_ref[:, cw:2 * cw]
    v = z_ref[:, 2 * cw:3 * cw]
    wd = z_ref[:, 3 * cw:3 * cw + 128]
    ad = z_ref[:, 3 * cw + 128:3 * cw + 256]
    gd = z_ref[:, 3 * cw + 256:C_PAD]

    uw = _dot3_presplit(jnp.tanh(wd), w2_ref[...])
    ua = _dot3_presplit(ad, a2_ref[...])
    kk_raw = k * kk_ref[...]
    ss = _seg_sum(kk_raw * kk_raw, e_ref)
    kk = kk_raw * lax.rsqrt(jnp.maximum(ss, 1e-24))
    rvk_o[:, 0:cw] = r.astype(BF16)
    rvk_o[:, cw:2 * cw] = v.astype(BF16)
    rvk_o[:, 2 * cw:3 * cw] = kk.astype(BF16)
    km_sum = None
    for d, kd_o in enumerate((kf_o, kb_o)):
        u = uw[:, d * cw:(d + 1) * cw] + w0_ref[d:d + 1, :]
        ld_o[:, d * cw:(d + 1) * cw] = -float(np.exp(-0.5)) * _sigmoid(u)
        a = _sigmoid(ua[:, d * cw:(d + 1) * cw] + a0_ref[d:d + 1, :])
        km = k * (1.0 + (a - 1.0) * ka_ref[...])
        kd_o[:, 0:cw] = km.astype(BF16)
        kd_o[:, cw:2 * cw] = (kk * a).astype(BF16)
        km_sum = km if km_sum is None else km_sum + km
    bonus = _seg_sum(r * km_sum * rk_ref[...], e_ref)
    bv_o[...] = (bonus * v).astype(BF16)
    g_o[...] = _dot(_sigmoid(gd).astype(BF16), g2_ref[...]).astype(BF16)


def _rwkv_prep(z, p, layer, n_ctx_tiles, tiles_per_lat, tm):
    m = z.shape[0]
    hb = tm // GRID_W
    n_halo = m // GRID_W
    lay = lambda *shape: pl.BlockSpec((None,) + shape, lambda i: (layer,) + (0,) * len(shape))
    outs = [(3, BF16), (2, BF16), (2, BF16), (2, F32), (1, BF16), (1, BF16)]
    return pl.pallas_call(
        functools.partial(_prep_kernel, n_ctx_tiles=n_ctx_tiles, tiles_per_lat=tiles_per_lat),
        grid=(m // tm,),
        in_specs=[
            pl.BlockSpec((tm, Z_COLS), lambda i: (i, 0)),
            pl.BlockSpec((GRID_W, Z_COLS), lambda i: (jnp.where(i < n_ctx_tiles, 0, i * hb - 1), 0)),
            pl.BlockSpec((GRID_W, Z_COLS),
                         lambda i: (jnp.where(i < n_ctx_tiles, 0, jnp.minimum(i * hb + hb, n_halo - 1)), 0)),
            lay(1, C_PAD),
            lay(2, C_WIDTH),
            lay(384, 2 * C_WIDTH),
            lay(2, C_WIDTH),
            lay(384, 2 * C_WIDTH),
            lay(1, C_WIDTH),
            lay(1, C_WIDTH),
            lay(1, C_WIDTH),
            lay(256, C_WIDTH),
            pl.BlockSpec((512, 256), lambda i: (0, 0)),
            lay(1, A_WIDTH),
            lay(1, A_WIDTH),
            lay(A_HEADS, SGU_CHUNK, SGU_CHUNK),
            lay(SGU_CHUNK, A_WIDTH),
        ],
        out_specs=([pl.BlockSpec((tm, n * C_WIDTH), lambda i: (i, 0)) for n, _ in outs]
                   + [pl.BlockSpec((tm, MIX_WIDTH), lambda i: (i, 0))]),
        out_shape=([jax.ShapeDtypeStruct((m, n * C_WIDTH), dt) for n, dt in outs]
                   + [jax.ShapeDtypeStruct((m, MIX_WIDTH), BF16)]),
        scratch_shapes=[pltpu.VMEM((tm, C_PAD), F32)],
        compiler_params=_cparams(("arbitrary",)),
        name="rwkv_prep",
    )(z, z, z, p["mu"], p["w0"], p["w2"], p["a0"], p["a2"], p["k_k"], p["k_a"], p["r_k"], p["g2"],
      p["e256"], p["ln_g"], p["ln_b"], p["sgu_w"], p["sgu_bias"])


def _bmm(a, b):
    return jnp.einsum("uik,ukj->uij", a.astype(BF16), b.astype(BF16), preferred_element_type=F32)


def _bmm_nt(a, b):
    return jnp.einsum("uik,ujk->uij", a.astype(BF16), b.astype(BF16), preferred_element_type=F32)


def _block_diag(y, bd_mask):
    return jnp.where(bd_mask, jnp.concatenate([y] * UNIT_HEADS, axis=1), jnp.zeros((), y.dtype))


def _to_units(x):
    return jnp.stack([x[:, p * UNIT_W:(p + 1) * UNIT_W] for p in range(N_UNITS)], axis=0)


def _scan_chunks(streams, hs):
    t = SCAN_T
    w = UNIT_W
    ti = lax.broadcasted_iota(jnp.int32, (t, w), 0)
    si = lax.broadcasted_iota(jnp.int32, (t, w), 1) & (HEAD_DIM - 1)
    eye = jnp.where(si == ti, 1.0, 0.0)
    t_row = lax.broadcasted_iota(jnp.int32, (t, 3 * t), 0)
    t_col = lax.broadcasted_iota(jnp.int32, (t, 3 * t), 1) & (t - 1)
    bd_mask = (lax.broadcasted_iota(jnp.int32, (w, w), 0) // HEAD_DIM
               == lax.broadcasted_iota(jnp.int32, (w, w), 1) // HEAD_DIM)

    q, ai, ki, vb, vf, lhs, p_tot = [], [], [], [], [], [], []
    for reverse, r, v, kk, ld, km, ka in streams:
        tri = jnp.where((t_col >= t_row) if reverse else (t_col <= t_row), 1.0, 0.0).astype(BF16)
        cum = _dot_exact_lhs(tri, ld)
        c_end = cum[0:1, :] if reverse else cum[t - 1:t, :]
        p_end = jnp.exp(c_end - cum)
        p_inv = jnp.exp(-cum)
        q.append(_to_units(jnp.concatenate([kk * jnp.exp(cum - ld), r * jnp.exp(cum)], axis=0).astype(BF16)))
        ai.append(_to_units((ka * p_inv).astype(BF16)))
        ki.append(_to_units((km * p_inv).astype(BF16)))
        vb.append(_to_units(v.astype(BF16)))
        vf.append(_to_units(v))
        lhs.append(_to_units(jnp.concatenate([ka * p_end, km * p_end], axis=0).astype(BF16)))
        p_tot.append(_to_units(jnp.exp(c_end)))
    q, ai, ki, vb, vf, lhs, p_tot = (jnp.concatenate(a, axis=0) for a in (q, ai, ki, vb, vf, lhs, p_tot))

    def masked(x, strictly):
        parts = []
        for n, stream in enumerate(streams):
            if stream[0]:
                keep = si > ti if strictly else si >= ti
            else:
                keep = si < ti if strictly else si <= ti
            parts.append(jnp.where(keep, x[n * N_UNITS:(n + 1) * N_UNITS], 0.0))
        return jnp.concatenate(parts, axis=0)

    rr = jnp.concatenate([_block_diag(ai, bd_mask), _block_diag(ki, bd_mask)], axis=1)
    sc = _bmm_nt(q, rr)
    la = masked(sc[:, 0:t, 0:w], True)
    lk = masked(sc[:, 0:t, w:2 * w], True)
    ma = masked(sc[:, t:2 * t, 0:w], False)
    mk = masked(sc[:, t:2 * t, w:2 * w], False)

    n_pow = -la
    x_inv = eye + n_pow
    n_pow = _bmm(n_pow, _block_diag(n_pow.astype(BF16), bd_mask))
    for _ in range(int(np.log2(t)) - 2):
        both = _bmm(jnp.concatenate([n_pow, x_inv], axis=1), _block_diag(n_pow.astype(BF16), bd_mask))
        x_inv = x_inv + both[:, t:2 * t]
        n_pow = both[:, 0:t]
    x_inv = x_inv + _bmm(x_inv, _block_diag(n_pow.astype(BF16), bd_mask))

    hb = _bmm_nt(q, hs)
    lv = _bmm(jnp.concatenate([lk, mk], axis=1), _block_diag(vb, bd_mask))
    u = -_bmm(x_inv, _block_diag((hb[:, 0:t] + lv[:, 0:t]).astype(BF16), bd_mask))
    y = hb[:, t:2 * t] + lv[:, t:2 * t] + _bmm(ma, _block_diag(u.astype(BF16), bd_mask))
    rhs_t = jnp.swapaxes(jnp.concatenate([u, vf], axis=1), 1, 2)
    upd = _bmm(rhs_t, lhs)
    hs_new = p_tot * hs + jnp.where(bd_mask, upd, 0.0)
    return y, hs_new


def _scan_kernel(*refs, has_s0, state_layer, state_fresh):
    ins = refs[:6]
    pos = 6
    s0_ref = None
    if has_s0:
        s0_ref = refs[pos]
        pos += 1
    if state_layer is not None and not state_fresh:
        pos += 1
    y_refs = refs[pos:pos + 2]
    pos += 2
    st_ref = None
    if state_layer is not None:
        st_ref = refs[pos]
        pos += 1
    h_ref = refs[pos]
    c = pl.program_id(1)
    hd = HEAD_DIM

    n_group = h_ref.shape[0]

    @pl.when(c == 0)
    def _():
        if has_s0:
            zero = jnp.zeros((hd, hd), F32)
            for s in range(n_group):
                for d in range(2):
                    for p in range(N_UNITS):
                        rows = [jnp.concatenate([s0_ref[s, d, UNIT_HEADS * p + a] if a == b else zero
                                                 for b in range(UNIT_HEADS)], axis=1) for a in range(UNIT_HEADS)]
                        h_ref[s, d, p] = jnp.concatenate(rows, axis=0)
        else:
            h_ref[...] = jnp.zeros_like(h_ref)

    streams = []
    for s in range(n_group):
        for d in range(2):
            rvk_ref, kd_ref, ld_ref = ins[3 * d:3 * d + 3]
            r, v, kk = (rvk_ref[s, :, n * C_WIDTH:(n + 1) * C_WIDTH].astype(F32) for n in range(3))
            km, ka = (kd_ref[s, :, n * C_WIDTH:(n + 1) * C_WIDTH].astype(F32) for n in range(2))
            streams.append((d == 1, r, v, kk, ld_ref[s], km, ka))
    n_all = len(streams) * N_UNITS
    y, hs_new = _scan_chunks(streams, h_ref[...].reshape(n_all, UNIT_W, UNIT_W))
    h_ref[...] = hs_new.reshape(h_ref.shape)
    for s in range(n_group):
        for d, y_ref in enumerate(y_refs):
            for p in range(N_UNITS):
                y_ref[s, :, p * UNIT_W:(p + 1) * UNIT_W] = y[(2 * s + d) * N_UNITS + p].astype(BF16)

    if state_layer is not None:
        @pl.when(c == pl.num_programs(1) - 1)
        def _():
            if state_fresh:
                st_ref[...] = jnp.zeros_like(st_ref)
            for s in range(n_group):
                for d in range(2):
                    for p in range(N_UNITS):
                        hs = h_ref[s, d, p]
                        for a in range(UNIT_HEADS):
                            block = hs[a * hd:(a + 1) * hd, a * hd:(a + 1) * hd]
                            if state_fresh:
                                st_ref[s, state_layer, d, UNIT_HEADS * p + a] = block
                            else:
                                st_ref[s, d, UNIT_HEADS * p + a] = block


def _rwkv_scan(arrs, n_seq, n_chunk, row0, layer, group, s0=None, state_depth=None, state_prev=None):
    seq_len = n_chunk * SCAN_T
    m_all = arrs[0].shape[0]
    assert row0 % (seq_len * group) == 0 and m_all % seq_len == 0 and n_seq % group == 0
    rvk, kf, kb, ld = (a.reshape(m_all // seq_len, seq_len, a.shape[1]) for a in arrs)
    g0 = row0 // (seq_len * group)
    fwd = lambda b, c: (b, c, 0)
    bwd = lambda b, c: (b, n_chunk - 1 - c, 0)
    blk = lambda imap, n=1: pl.BlockSpec((group, SCAN_T, n * C_WIDTH), imap)
    in_specs = [blk(lambda b, c: (g0 + b, c, 0), 3), blk(lambda b, c: (g0 + b, c, 0), 2),
                blk(lambda b, c: (g0 + b, c, 0)),
                blk(lambda b, c: (g0 + b, n_chunk - 1 - c, 0), 3), blk(lambda b, c: (g0 + b, n_chunk - 1 - c, 0), 2),
                blk(lambda b, c: (g0 + b, n_chunk - 1 - c, 1))]
    args = [rvk, kf, ld, rvk, kb, ld]
    st_dims = (2, N_HEADS, HEAD_DIM, HEAD_DIM)
    if s0 is not None:
        in_specs.append(pl.BlockSpec((group, None) + st_dims, lambda b, c: (b, layer, 0, 0, 0, 0)))
        args.append(s0)
    aliases = {}
    if state_prev is not None:
        aliases = {len(args): 2}
        in_specs.append(pl.BlockSpec(memory_space=pl.ANY))
        args.append(state_prev)
    y_shape = jax.ShapeDtypeStruct((n_seq, seq_len, C_WIDTH), BF16)
    out_specs = [blk(fwd), blk(bwd)]
    out_shape = [y_shape, y_shape]
    if state_depth is not None:
        if state_prev is None:
            out_specs.append(pl.BlockSpec((group, state_depth) + st_dims, lambda b, c: (b, 0, 0, 0, 0, 0)))
        else:
            out_specs.append(pl.BlockSpec((group, None) + st_dims, lambda b, c: (b, layer, 0, 0, 0, 0)))
        out_shape.append(jax.ShapeDtypeStruct((n_seq, state_depth) + st_dims, F32))
    outs = pl.pallas_call(
        functools.partial(_scan_kernel, has_s0=s0 is not None,
                          state_layer=layer if state_depth is not None else None,
                          state_fresh=state_prev is None),
        grid=(n_seq // group, n_chunk),
        in_specs=in_specs,
        out_specs=out_specs,
        out_shape=out_shape,
        input_output_aliases=aliases,
        scratch_shapes=[pltpu.VMEM((group, 2, N_UNITS, UNIT_W, UNIT_W), F32)],
        compiler_params=_cparams(("arbitrary", "arbitrary")),
        name="rwkv_scan",
    )(*args)
    ys = [y.reshape(n_seq * seq_len, C_WIDTH) for y in outs[:2]]
    return ys + list(outs[2:])


def _post_kernel(yf1_ref, yb1_ref, yf2_ref, yb2_ref, bv_ref, g_ref, lg_ref, lb_ref, e_ref, mix_ref, o_ref, *,
                 n_ctx_tiles):
    del mix_ref
    is_ctx = pl.program_id(0) < n_ctx_tiles
    f32 = lambda ref: ref[...].astype(F32)
    y = jnp.where(is_ctx, f32(yf1_ref) + f32(yb1_ref), f32(yf2_ref) + f32(yb2_ref))
    inv = 1.0 / HEAD_DIM
    m = _seg_sum(y, e_ref) * inv
    yc = y - m
    var = _seg_sum(yc * yc, e_ref) * inv
    yn = yc * lax.rsqrt(var + LNX_EPS) * lg_ref[...] + lb_ref[...]
    o_ref[...] = ((yn + bv_ref[...]) * g_ref[...]).astype(BF16)


def _rwkv_post(y_ctx, y_lat, bv, g, lnx_g, lnx_b, e256, mix, layer, n_ctx_tiles, tm):
    m = bv.shape[0]
    n_tiles = m // tm
    row = pl.BlockSpec((tm, C_WIDTH), lambda i: (i, 0))
    row_ctx = pl.BlockSpec((tm, C_WIDTH), lambda i: (jnp.minimum(i, n_ctx_tiles - 1), 0))
    row_lat = pl.BlockSpec((tm, C_WIDTH), lambda i: (jnp.maximum(i - n_ctx_tiles, 0), 0))
    vec = pl.BlockSpec((None, 1, C_WIDTH), lambda i: (layer, 0, 0))
    return pl.pallas_call(
        functools.partial(_post_kernel, n_ctx_tiles=n_ctx_tiles),
        grid=(n_tiles,),
        in_specs=[row_ctx, row_ctx, row_lat, row_lat, row, row, vec, vec,
                  pl.BlockSpec((512, 256), lambda i: (0, 0)), pl.BlockSpec(memory_space=pl.ANY)],
        out_specs=pl.BlockSpec((tm, C_WIDTH), lambda i: (i, (A_WIDTH + B_WIDTH) // C_WIDTH)),
        out_shape=jax.ShapeDtypeStruct(mix.shape, mix.dtype),
        input_output_aliases={9: 0},
        compiler_params=_cparams(("arbitrary",)),
        name="rwkv_post",
    )(y_ctx[0], y_ctx[1], y_lat[0], y_lat[1], bv, g, lnx_g, lnx_b, e256, mix)


def kernel(x_prompt, x_sample, state_wkv, c, c_ctx, norm_g, w_mod, b_mod, ffn_w_in, ffn_w_out, w_in, w_out,
           sgu_ln_g, sgu_ln_b, sgu_w, sgu_b, shift_mu, decay_w0, decay_w2, iclr_a0, iclr_a2, k_k, k_a, r_k,
           gate_w2, lnx_g, lnx_b, final_g):
    batch, seq, d = x_prompt.shape
    dec_batch, dec_seq, _ = x_sample.shape
    depth = w_mod.shape[0]
    assert d == D_MODEL and dec_batch + 1 <= MOD_ROWS
    m_ctx = batch * seq
    tm = 1024
    ts = 512
    tp = 256
    assert seq == tp and dec_seq == tm and m_ctx % tm == 0 and dec_seq % GRID_W == 0

    def row_of_tile_fn(rows):
        n_ctx = m_ctx // rows
        per_lat = dec_seq // rows
        return lambda i: jnp.where(i < n_ctx, 0, 1 + (i - n_ctx) // per_lat)

    zeros_cw = jnp.zeros((depth, DECAY_RANK, C_WIDTH), F32)

    def both_dirs(w):
        top = jnp.concatenate([w[:, 0], zeros_cw], axis=-1)
        bot = jnp.concatenate([zeros_cw, w[:, 1]], axis=-1)
        return jnp.concatenate([top, bot], axis=1)

    e_np = (np.arange(256)[:, None] // HEAD_DIM == np.arange(256)[None, :] // HEAD_DIM)
    prep_params = {
        "mu": jnp.pad(shift_mu, ((0, 0), (0, C_PAD - C_IN))).reshape(depth, 1, C_PAD),
        "w0": decay_w0, "w2": _split_rows3(both_dirs(decay_w2)), "a0": iclr_a0, "a2": _split_rows3(both_dirs(iclr_a2)),
        "k_k": k_k.reshape(depth, 1, C_WIDTH), "k_a": k_a.reshape(depth, 1, C_WIDTH),
        "r_k": r_k.reshape(depth, 1, C_WIDTH),
        "g2": jnp.pad(gate_w2, ((0, 0), (0, 256 - GATE_RANK), (0, 0))).astype(BF16),
        "e256": jnp.asarray(np.concatenate([e_np, e_np], axis=0), BF16),
    }
    cd, sd = _dft_mats(B_GROUP_DIM)
    cs, ss = _dft_mats(seq)
    cl, sl = _dft_mats(dec_seq)
    fnet_consts = (_hi_lo(np.concatenate([cd, sd], axis=1)) + _hi_lo(np.concatenate([cs, -ss], axis=1))
                   + _hi_lo(np.concatenate([cl, -sl], axis=1)))
    prep_params.update({
        "sgu_bias": jnp.repeat(jnp.swapaxes(sgu_b, 1, 2), A_WIDTH // A_HEADS, axis=2),
        "sgu_w": sgu_w.astype(BF16),
        "ln_g": sgu_ln_g.reshape(depth, 1, A_WIDTH), "ln_b": sgu_ln_b.reshape(depth, 1, A_WIDTH),
    })
    lnx_g3 = lnx_g.reshape(depth, 1, C_WIDTH)
    lnx_b3 = lnx_b.reshape(depth, 1, C_WIDTH)
    norm_g4 = norm_g.reshape(depth, 3, 1, d)

    cond = jnp.concatenate([c_ctx[None, :], c, jnp.zeros((MOD_ROWS - 1 - dec_batch, d), F32)], axis=0)
    mod = _modulation(cond, w_mod, b_mod).reshape(depth, MOD_ROWS, N_MOD, 1, d)

    m = m_ctx + dec_batch * dec_seq
    n_ctx_t = m_ctx // tm
    n_lat_t = dec_batch * dec_seq // tm
    rot = row_of_tile_fn(tm)
    ctx_row = lambda i: 0
    lat_row = lambda i: 1 + i
    n_chunk_ctx = seq // SCAN_T
    n_chunk_lat = dec_seq // SCAN_T
    tk = 1024
    scan_group = lambda n_seq: 4 if n_seq % 4 == 0 else 1
    w_in_t = jnp.swapaxes(w_in, 1, 2).astype(BF16)
    w_out_b = w_out.astype(BF16)
    new_state = None
    x = None
    for l in range(depth):
        ffn_args = (mod, norm_g4, ffn_w_in, ffn_w_out, l)
        if l == 0:
            x = _ffn(x_prompt.reshape(m_ctx, d), *ffn_args, 0, 0, rot, tm, n_ctx_t + n_lat_t,
                     x_tail=x_sample.reshape(m - m_ctx, d))
        else:
            x = _ffn(x, *ffn_args, 0, 0, rot, tm, n_ctx_t + n_lat_t)
        z = _in_proj(x, mod, norm_g4, w_in_t, l, rot, tm)
        *scan_in, bv, g, mix = _rwkv_prep(z, prep_params, l, m_ctx // tp, dec_seq // tp, tp)
        mix = _fnet(z, fnet_consts, mix, m_ctx // tm, seq, tm)
        yf_c, yb_c, new_state = _rwkv_scan(scan_in, batch, n_chunk_ctx, 0, l, scan_group(batch),
                                           state_depth=depth, state_prev=new_state)
        yf_l, yb_l = _rwkv_scan(scan_in, dec_batch, n_chunk_lat, m_ctx, l, scan_group(dec_batch), s0=state_wkv)
        mix = _rwkv_post((yf_c, yb_c), (yf_l, yb_l), bv, g, lnx_g3, lnx_b3, prep_params["e256"], mix, l,
                         m_ctx // ts, ts)
        x = _resid_matmul(x, mod, mix, w_out_b, pl.BlockSpec((None, tk, D_MODEL), lambda i, k: (l, k, 0)),
                          l, 5, 1.0, rot, tm, tk, "mix_out")
        if l < depth - 1:
            x = _ffn(x, *ffn_args, 2, 1, rot, tm, n_ctx_t + n_lat_t)
    fg = final_g.reshape(1, d)
    y_ctx = _ffn(x, *ffn_args, 2, 1, ctx_row, tm, n_ctx_t, final_g=fg)
    y_lat = _ffn(x, *ffn_args, 2, 1, lat_row, tm, n_lat_t, in_tile0=n_ctx_t, final_g=fg)
    return (y_ctx.reshape(batch, seq, d), y_lat.reshape(dec_batch, dec_seq, d), new_state)
```
